```python
import math
import jax
import jax.numpy as jnp
from jax import lax
import numpy as np

D_MODEL = 1024
BATCH = 16
SEQ = 2048
DEPTH = 1
DEC_BATCH = 128
DEC_SEQ = 8
PAST_LEN = 8192
PAGE_SIZE = 128

NSA_HEADS = 8
NSA_KV_HEADS = 2
NSA_GROUP = NSA_HEADS // NSA_KV_HEADS
HEAD_DIM = 64
NSA_WIDTH = NSA_HEADS * HEAD_DIM
KV_WIDTH = NSA_KV_HEADS * HEAD_DIM
CMP_BLOCK = 32
CMP_STRIDE = 16
CMP_HIDDEN = 2 * HEAD_DIM
SEL_BLOCK = 64
N_SEL = 16
WINDOW = 512
Q_BLOCK = 128
SEL_FORCE = 1e9
GLA_HEADS = 4
GLA_DK = 64
GLA_DV = 128
GLA_K_WIDTH = GLA_HEADS * GLA_DK
GLA_V_WIDTH = GLA_HEADS * GLA_DV
GLA_RANK = 16
GLA_TAU = 16.0
GLA_CHUNK = 64
N_EXPERTS = 32
TOP_K = 4
D_FF = D_MODEL
SWIGLU_ALPHA = 1.702
SWIGLU_LIMIT = 7.0
MOE_BLOCK = 256
REL_BUCKETS = 32
REL_EXACT = REL_BUCKETS // 2
REL_MAX_DIST = 1024
RMS_EPS = 1e-6
NEG_INF = -1e30
IN_SPLITS = (NSA_WIDTH, 6 * KV_WIDTH, 3 * NSA_HEADS, GLA_K_WIDTH, GLA_K_WIDTH, GLA_V_WIDTH, GLA_RANK, GLA_V_WIDTH, D_MODEL, D_MODEL)
N_IN = NSA_WIDTH + 6 * KV_WIDTH + 3 * NSA_HEADS + 2 * GLA_K_WIDTH + 2 * GLA_V_WIDTH + GLA_RANK + 2 * D_MODEL

kernel_name = 'nsa_gla_gated_parallel_moe_decoder_step'


def rmsnorm(x, g):
    x32 = x.astype(jnp.float32)
    y = x32 * lax.rsqrt(jnp.mean(x32 * x32, axis=-1, keepdims=True) + RMS_EPS)
    return (y * g.astype(jnp.float32)).astype(x.dtype)


def rel_bucket(dist):
    n = jnp.maximum(dist, 0)
    scaled = jnp.log(jnp.maximum(n, 1).astype(jnp.float32) / REL_EXACT) / math.log(REL_MAX_DIST / REL_EXACT)
    large = REL_EXACT + (scaled * (REL_BUCKETS - REL_EXACT)).astype(jnp.int32)
    return jnp.where(n < REL_EXACT, n, jnp.minimum(large, REL_BUCKETS - 1))


def masked_softmax(s, mask):
    s = jnp.where(mask, s, NEG_INF)
    m = jnp.max(s, axis=-1, keepdims=True)
    p = jnp.where(mask, jnp.exp(s - m), 0.0)
    return p / jnp.maximum(jnp.sum(p, axis=-1, keepdims=True), 1e-20)


def pad_seq(x, mult, axis):
    extra = (-x.shape[axis]) % mult
    widths = [(0, 0)] * x.ndim
    widths[axis] = (0, extra)
    return jnp.pad(x, widths)


def cmp_end_positions(n_cmp):
    return jnp.arange(n_cmp, dtype=jnp.int32) * CMP_STRIDE + (CMP_BLOCK - 1)


def compress(k_raw, pe, w1, w2):
    rc = CMP_BLOCK // CMP_STRIDE
    n_ch = k_raw.shape[0] // CMP_STRIDE
    n_cmp = n_ch - rc + 1
    c = k_raw[:n_ch * CMP_STRIDE].reshape(n_ch, CMP_STRIDE, NSA_KV_HEADS, HEAD_DIM)
    pe_r = pe.reshape(rc, CMP_STRIDE, HEAD_DIM)
    w1_r = w1.reshape(rc, CMP_STRIDE, HEAD_DIM, CMP_HIDDEN)
    h = None
    for r in range(rc):
        term = jnp.einsum('nlgd,ldh->ngh', c[r:r + n_cmp] + pe_r[r][None, :, None, :], w1_r[r])
        h = term if h is None else h + term
    return jnp.einsum('ngh,hd->ngd', jax.nn.gelu(h), w2)


def block_importance(p, n_sel):
    rc = CMP_BLOCK // CMP_STRIDE
    rs = SEL_BLOCK // CMP_STRIDE
    total = rs * n_sel + rs + rc
    widths = [(0, 0)] * (p.ndim - 1) + [(rc - 1, total - (rc - 1) - p.shape[-1])]
    pp = jnp.pad(p, widths)
    imp = None
    for o in range(rs + rc - 1):
        w = min(o - (rc - 1) + rc, rs) - max(o - (rc - 1), 0)
        term = w * pp[..., o:o + rs * n_sel:rs]
        imp = term if imp is None else imp + term
    return imp


def nsa_core(q, qpos, gates, kc, vc, cend, ks, vs, kw, vw, kwpos, rel_table):
    nq = q.shape[0]
    G, R, dh = NSA_KV_HEADS, NSA_GROUP, HEAD_DIM
    qg = q.reshape(nq, G, R, dh) * (dh ** -0.5)
    tab = rel_table.astype(jnp.float32).reshape(REL_BUCKETS, G, R)
    dist_c = qpos[:, None] - cend[None, :]
    s_c = jnp.einsum('qgrd,cgd->qgrc', qg, kc).astype(jnp.float32) + tab[rel_bucket(dist_c)].transpose(0, 2, 3, 1)
    p_c = masked_softmax(s_c, (dist_c >= 0)[:, None, None, :])
    o_c = jnp.einsum('qgrc,cgd->qgrd', p_c.astype(vc.dtype), vc)
    n_sel = ks.shape[0] // SEL_BLOCK
    imp = block_importance(jnp.sum(p_c, axis=2), n_sel)
    blk = jnp.arange(n_sel)
    cur = (qpos // SEL_BLOCK)[:, None]
    forced = (blk == 0) | (blk == cur) | (blk == cur - 1)
    causal = (blk * SEL_BLOCK) <= qpos[:, None]
    score = jnp.where(causal[:, None], jnp.where(forced[:, None], SEL_FORCE, imp), -SEL_FORCE)
    n_top = min(N_SEL, n_sel)
    _, idx = lax.top_k(score, n_top)
    gi = jnp.arange(G)[None, :, None]
    kb = ks.reshape(n_sel, SEL_BLOCK, G, dh).transpose(2, 0, 1, 3)
    vb = vs.reshape(n_sel, SEL_BLOCK, G, dh).transpose(2, 0, 1, 3)
    k_g = kb[gi, idx]
    v_g = vb[gi, idx]
    kpos = idx[..., None] * SEL_BLOCK + jnp.arange(SEL_BLOCK)
    dist_s = qpos[:, None, None, None] - kpos
    bias_s = tab.transpose(1, 0, 2)[gi[..., None], rel_bucket(dist_s)]
    s_s = jnp.einsum('qgrd,qgnld->qgrnl', qg, k_g).astype(jnp.float32) + bias_s.transpose(0, 1, 4, 2, 3)
    n_keys = n_top * SEL_BLOCK
    p_s = masked_softmax(s_s.reshape(nq, G, R, n_keys), (dist_s >= 0).reshape(nq, G, 1, n_keys))
    o_s = jnp.einsum('qgrk,qgkd->qgrd', p_s.astype(vs.dtype), v_g.reshape(nq, G, n_keys, dh))
    dist_w = qpos[:, None] - kwpos[None, :]
    s_w = jnp.einsum('qgrd,kgd->qgrk', qg, kw).astype(jnp.float32) + tab[rel_bucket(dist_w)].transpose(0, 2, 3, 1)
    mask_w = (dist_w >= 0) & (dist_w < WINDOW) & (kwpos >= 0)[None, :]
    p_w = masked_softmax(s_w, mask_w[:, None, None, :])
    o_w = jnp.einsum('qgrk,kgd->qgrd', p_w.astype(vw.dtype), vw)
    g = gates.reshape(nq, G, R, 3)
    o = g[..., 0:1] * o_c + g[..., 1:2] * o_s + g[..., 2:3] * o_w
    return o.reshape(nq, NSA_WIDTH)


def nsa_prompt(f, p, rel_table):
    q, gates = f['q'], f['gates']
    bsz, t = q.shape[0], q.shape[1]
    comp = jax.vmap(compress, in_axes=(0, None, None, None))
    kc = comp(f['k_cmp'], p['nsa_pe_k'], p['nsa_w1_k'], p['nsa_w2_k'])
    vc = comp(f['v_cmp'], p['nsa_pe_v'], p['nsa_w1_v'], p['nsa_w2_v'])
    cend = cmp_end_positions(kc.shape[1])
    ks = pad_seq(f['k_sel'], SEL_BLOCK, 1)
    vs = pad_seq(f['v_sel'], SEL_BLOCK, 1)
    win_pad = ((0, 0), (WINDOW, 0), (0, 0), (0, 0))
    kw = jnp.pad(f['k_win'], win_pad)
    vw = jnp.pad(f['v_win'], win_pad)
    n_qb = t // Q_BLOCK
    b_idx = jnp.repeat(jnp.arange(bsz), n_qb)
    qb_idx = jnp.tile(jnp.arange(n_qb), bsz)

    def one(item):
        b, qb = item
        q0 = qb * Q_BLOCK
        qpos = q0 + jnp.arange(Q_BLOCK)
        kwpos = q0 - WINDOW + jnp.arange(WINDOW + Q_BLOCK)
        return nsa_core(
            lax.dynamic_slice_in_dim(q[b], q0, Q_BLOCK, 0), qpos,
            lax.dynamic_slice_in_dim(gates[b], q0, Q_BLOCK, 0),
            kc[b], vc[b], cend, ks[b], vs[b],
            lax.dynamic_slice_in_dim(kw[b], q0, WINDOW + Q_BLOCK, 0),
            lax.dynamic_slice_in_dim(vw[b], q0, WINDOW + Q_BLOCK, 0),
            kwpos, rel_table)

    out = lax.map(one, (b_idx, qb_idx))
    return out.reshape(bsz, t, NSA_WIDTH)


def nsa_sample(f, pool_kc, pool_vc, pool_ks, pool_vs, buf_kw, buf_vw, page_table, p, rel_table):
    s = f['q'].shape[1]
    n_pages = page_table.shape[1]
    past = n_pages * PAGE_SIZE
    w_buf = buf_kw.shape[1]
    qpos = past + jnp.arange(s)
    kwpos = past - w_buf + jnp.arange(w_buf + s)

    def past_rows(pool, pt):
        return pool[pt].reshape(past, NSA_KV_HEADS, HEAD_DIM)

    def one(args):
        q_b, g_b, kcn, vcn, ksn, vsn, kwn, vwn, bkw, bvw, pt = args
        kc = compress(jnp.concatenate([past_rows(pool_kc, pt), kcn.astype(pool_kc.dtype)], 0), p['nsa_pe_k'], p['nsa_w1_k'], p['nsa_w2_k'])
        vc = compress(jnp.concatenate([past_rows(pool_vc, pt), vcn.astype(pool_vc.dtype)], 0), p['nsa_pe_v'], p['nsa_w1_v'], p['nsa_w2_v'])
        cend = cmp_end_positions(kc.shape[0])
        ks = pad_seq(jnp.concatenate([past_rows(pool_ks, pt), ksn.astype(pool_ks.dtype)], 0), SEL_BLOCK, 0)
        vs = pad_seq(jnp.concatenate([past_rows(pool_vs, pt), vsn.astype(pool_vs.dtype)], 0), SEL_BLOCK, 0)
        kw = jnp.concatenate([bkw, kwn.astype(bkw.dtype)], 0)
        vw = jnp.concatenate([bvw, vwn.astype(bvw.dtype)], 0)
        return nsa_core(q_b, qpos, g_b, kc, vc, cend, ks, vs, kw, vw, kwpos, rel_table)

    xs = (f['q'], f['gates'], f['k_cmp'], f['v_cmp'], f['k_sel'], f['v_sel'], f['k_win'], f['v_win'], buf_kw, buf_vw, page_table)
    return lax.map(one, xs)


def gla_chunk(S, q, k, v, lg):
    q = q.astype(jnp.float32)
    k = k.astype(jnp.float32)
    v = v.astype(jnp.float32)
    b = jnp.cumsum(lg.astype(jnp.float32), axis=1)
    c = q.shape[1]
    causal = jnp.tril(jnp.ones((c, c), dtype=bool))
    diff = b[:, :, None] - b[:, None, :]
    decay = jnp.exp(jnp.where(causal[None, :, :, None, None], diff, -jnp.inf))
    att = jnp.einsum('bthk,bshk,btshk->bths', q, k, decay)
    o = jnp.einsum('bthk,bhkv->bthv', q * jnp.exp(b), S) + jnp.einsum('bths,bshv->bthv', att, v)
    b_last = b[:, -1]
    S_new = S * jnp.exp(b_last)[..., None] + jnp.einsum('bshk,bshv->bhkv', k * jnp.exp(b_last[:, None] - b), v)
    return S_new, o


def gla_prompt(q, k, v, lg):
    bsz, t = q.shape[0], q.shape[1]
    c = min(GLA_CHUNK, t)
    n_c = t // c

    def to_chunks(x):
        return x.reshape(bsz, n_c, c, *x.shape[2:]).swapaxes(0, 1)

    S0 = jnp.zeros((bsz, GLA_HEADS, GLA_DK, GLA_DV), jnp.float32)
    S_fin, o = lax.scan(lambda S, xs: gla_chunk(S, xs[0], xs[1], xs[2], xs[3]), S0,
                        (to_chunks(q), to_chunks(k), to_chunks(v), to_chunks(lg)))
    return o.swapaxes(0, 1).reshape(bsz, t, GLA_HEADS, GLA_DV), S_fin


def gla_output(o, r, g_norm):
    o = o * lax.rsqrt(jnp.mean(o * o, axis=-1, keepdims=True) + RMS_EPS) * g_norm.astype(jnp.float32)
    o = o * jax.nn.silu(r.astype(jnp.float32).reshape(o.shape))
    return o.reshape(*o.shape[:-2], GLA_V_WIDTH)


def split_projection(h, p):
    z = h @ p['w_in']
    lead = z.shape[:-1]
    cols = []
    off = 0
    for n in IN_SPLITS:
        cols.append(z[..., off:off + n])
        off += n
    q, kv, g_nsa, q_l, k_l, v_l, a_lr, r_l, m_a, m_b = cols
    kv = kv.reshape(*lead, 6, NSA_KV_HEADS, HEAD_DIM)
    lg = jax.nn.log_sigmoid((a_lr @ p['gla_w_alpha'] + p['gla_b_alpha']).astype(jnp.float32)) / GLA_TAU
    return dict(
        q=q.reshape(*lead, NSA_HEADS, HEAD_DIM),
        gates=jax.nn.sigmoid(g_nsa).reshape(*lead, NSA_HEADS, 3),
        k_cmp=kv[..., 0, :, :], v_cmp=kv[..., 1, :, :],
        k_sel=kv[..., 2, :, :], v_sel=kv[..., 3, :, :],
        k_win=kv[..., 4, :, :], v_win=kv[..., 5, :, :],
        q_l=(q_l * GLA_DK ** -0.5).reshape(*lead, GLA_HEADS, GLA_DK),
        k_l=k_l.reshape(*lead, GLA_HEADS, GLA_DK),
        v_l=v_l.reshape(*lead, GLA_HEADS, GLA_DV),
        lg=lg.reshape(*lead, GLA_HEADS, GLA_DK),
        r=r_l, m_a=m_a, m_b=m_b)


def merge_branches(o_nsa, o_gla, m_a, m_b, p):
    y_a = o_nsa @ p['w_branch_nsa']
    y_b = o_gla.astype(o_nsa.dtype) @ p['w_branch_gla']
    m = jax.nn.sigmoid(m_a) * y_a + jax.nn.sigmoid(m_b) * y_b
    return m @ p['w_out']


def moe(x, p):
    n, d = x.shape
    logits = (x @ p['router_w'] + p['router_b']).astype(jnp.float32)
    top_v, top_i = lax.top_k(logits, TOP_K)
    gate = jax.nn.softmax(top_v, axis=-1)
    nk = n * TOP_K
    e_flat = top_i.reshape(nk)
    tok_flat = jnp.repeat(jnp.arange(n, dtype=jnp.int32), TOP_K)
    order = jnp.argsort(e_flat)
    e_sorted = e_flat[order]
    counts = jnp.bincount(e_flat, length=N_EXPERTS)
    starts = jnp.cumsum(counts) - counts
    padded = (counts + MOE_BLOCK - 1) // MOE_BLOCK * MOE_BLOCK
    pends = jnp.cumsum(padded)
    pstarts = pends - padded
    dest = pstarts[e_sorted] + jnp.arange(nk) - starts[e_sorted]
    n_blocks = (nk + N_EXPERTS * (MOE_BLOCK - 1) + MOE_BLOCK - 1) // MOE_BLOCK
    rows = n_blocks * MOE_BLOCK
    row_tok = jnp.full((rows,), n, jnp.int32).at[dest].set(tok_flat[order])
    row_w = jnp.zeros((rows,), jnp.float32).at[dest].set(gate.reshape(nk)[order])
    block_e = jnp.minimum(jnp.searchsorted(pends, jnp.arange(n_blocks) * MOE_BLOCK, side='right'), N_EXPERTS - 1)
    x_pad = jnp.concatenate([x, jnp.zeros((1, d), x.dtype)], axis=0)

    def expert_block(args):
        tok, e = args
        xb = x_pad[tok]
        g = xb @ p['exp_w_gate'][e] + p['exp_b_gate'][e]
        u = xb @ p['exp_w_up'][e] + p['exp_b_up'][e]
        g = jnp.minimum(g, SWIGLU_LIMIT)
        u = jnp.clip(u, -SWIGLU_LIMIT, SWIGLU_LIMIT)
        h = (u + 1.0) * (g * jax.nn.sigmoid(SWIGLU_ALPHA * g))
        return h @ p['exp_w_down'][e] + p['exp_b_down'][e]

    y_rows = lax.map(expert_block, (row_tok.reshape(n_blocks, MOE_BLOCK), block_e)).reshape(rows, d)
    y = jax.ops.segment_sum(y_rows.astype(jnp.float32) * row_w[:, None], row_tok, num_segments=n + 1)
    return y[:n].astype(x.dtype)


def prompt_layer(x, p, rel_table):
    b, t, d = x.shape
    f = split_projection(rmsnorm(x, p['norm_mix']), p)
    o_nsa = nsa_prompt(f, p, rel_table)
    o_gla, s_gla = gla_prompt(f['q_l'], f['k_l'], f['v_l'], f['lg'])
    x = x + merge_branches(o_nsa, gla_output(o_gla, f['r'], p['gla_norm']), f['m_a'], f['m_b'], p)
    x = x + moe(rmsnorm(x, p['norm_ffn']).reshape(b * t, d), p).reshape(b, t, d)
    w_len = min(WINDOW, t)
    return x, (f['k_cmp'], f['v_cmp'], f['k_sel'], f['v_sel'], f['k_win'][:, t - w_len:], f['v_win'][:, t - w_len:], s_gla)


def sample_layer(x, pool_kc, pool_vc, pool_ks, pool_vs, buf_kw, buf_vw, gla_state, page_table, p, rel_table):
    b, s, d = x.shape
    f = split_projection(rmsnorm(x, p['norm_mix']), p)
    o_nsa = nsa_sample(f, pool_kc, pool_vc, pool_ks, pool_vs, buf_kw, buf_vw, page_table, p, rel_table)
    s_gla, o_gla = gla_chunk(gla_state.astype(jnp.float32), f['q_l'], f['k_l'], f['v_l'], f['lg'])
    x = x + merge_branches(o_nsa, gla_output(o_gla, f['r'], p['gla_norm']), f['m_a'], f['m_b'], p)
    x = x + moe(rmsnorm(x, p['norm_ffn']).reshape(b * s, d), p).reshape(b, s, d)
    new_kw = jnp.concatenate([buf_kw, f['k_win'].astype(buf_kw.dtype)], axis=1)[:, s:]
    new_vw = jnp.concatenate([buf_vw, f['v_win'].astype(buf_vw.dtype)], axis=1)[:, s:]
    return x, (f['k_cmp'], f['v_cmp'], f['k_sel'], f['v_sel'], new_kw, new_vw, s_gla)


def setup_inputs(seed: int = 0) -> dict:
    key = jax.random.key(seed)
    k = jax.random.split(key, 40)
    f32 = jnp.float32
    n_pages = PAST_LEN // PAGE_SIZE
    n_phys = (DEC_BATCH * n_pages * 5) // 4
    w_buf = min(WINDOW, PAST_LEN)

    def nrm(kk, shape, scale):
        return jax.random.normal(kk, shape, f32) * scale

    def gain(kk, shape):
        return 1.0 + 0.02 * jax.random.normal(kk, shape, f32)

    pool = (DEPTH, n_phys, PAGE_SIZE, NSA_KV_HEADS, HEAD_DIM)
    win = (DEPTH, DEC_BATCH, w_buf, NSA_KV_HEADS, HEAD_DIM)
    perm = jax.random.permutation(k[9], n_phys)
    page_table = perm[:DEC_BATCH * n_pages].reshape(DEC_BATCH, n_pages).astype(jnp.int32)
    return {
        'x_prompt': nrm(k[0], (BATCH, SEQ, D_MODEL), 1.0),
        'x_sample': nrm(k[1], (DEC_BATCH, DEC_SEQ, D_MODEL), 1.0),
        'cache_cmp_k': nrm(k[2], pool, 1.0),
        'cache_cmp_v': nrm(k[3], pool, 1.0),
        'cache_sel_k': nrm(k[4], pool, 1.0),
        'cache_sel_v': nrm(k[5], pool, 1.0),
        'state_win_k': nrm(k[6], win, 1.0),
        'state_win_v': nrm(k[7], win, 1.0),
        'state_gla': nrm(k[8], (DEPTH, DEC_BATCH, GLA_HEADS, GLA_DK, GLA_DV), 1.0),
        'page_table': page_table,
        'rel_bias': nrm(k[10], (REL_BUCKETS, NSA_HEADS), 0.5),
        'norm_mix': gain(k[11], (DEPTH, D_MODEL)),
        'w_in': nrm(k[12], (DEPTH, D_MODEL, N_IN), D_MODEL ** -0.5),
        'nsa_pe_k': nrm(k[13], (DEPTH, CMP_BLOCK, HEAD_DIM), 0.5),
        'nsa_pe_v': nrm(k[14], (DEPTH, CMP_BLOCK, HEAD_DIM), 0.5),
        'nsa_w1_k': nrm(k[15], (DEPTH, CMP_BLOCK, HEAD_DIM, CMP_HIDDEN), (CMP_BLOCK * HEAD_DIM) ** -0.5),
        'nsa_w1_v': nrm(k[16], (DEPTH, CMP_BLOCK, HEAD_DIM, CMP_HIDDEN), (CMP_BLOCK * HEAD_DIM) ** -0.5),
        'nsa_w2_k': nrm(k[17], (DEPTH, CMP_HIDDEN, HEAD_DIM), CMP_HIDDEN ** -0.5),
        'nsa_w2_v': nrm(k[18], (DEPTH, CMP_HIDDEN, HEAD_DIM), CMP_HIDDEN ** -0.5),
        'gla_w_alpha': nrm(k[19], (DEPTH, GLA_RANK, GLA_K_WIDTH), GLA_RANK ** -0.5),
        'gla_b_alpha': nrm(k[20], (DEPTH, GLA_K_WIDTH), 0.1),
        'gla_norm': gain(k[21], (DEPTH, GLA_DV)),
        'w_branch_nsa': nrm(k[22], (DEPTH, NSA_WIDTH, D_MODEL), NSA_WIDTH ** -0.5),
        'w_branch_gla': nrm(k[23], (DEPTH, GLA_V_WIDTH, D_MODEL), GLA_V_WIDTH ** -0.5),
        'w_out': nrm(k[24], (DEPTH, D_MODEL, D_MODEL), D_MODEL ** -0.5),
        'norm_ffn': gain(k[25], (DEPTH, D_MODEL)),
        'router_w': nrm(k[26], (DEPTH, D_MODEL, N_EXPERTS), D_MODEL ** -0.5),
        'router_b': nrm(k[27], (DEPTH, N_EXPERTS), 0.01),
        'exp_w_gate': nrm(k[28], (DEPTH, N_EXPERTS, D_MODEL, D_FF), D_MODEL ** -0.5),
        'exp_b_gate': nrm(k[29], (DEPTH, N_EXPERTS, D_FF), 0.02),
        'exp_w_up': nrm(k[30], (DEPTH, N_EXPERTS, D_MODEL, D_FF), D_MODEL ** -0.5),
        'exp_b_up': nrm(k[31], (DEPTH, N_EXPERTS, D_FF), 0.02),
        'exp_w_down': nrm(k[32], (DEPTH, N_EXPERTS, D_FF, D_MODEL), D_FF ** -0.5),
        'exp_b_down': nrm(k[33], (DEPTH, N_EXPERTS, D_MODEL), 0.02),
        'norm_final': gain(k[34], (D_MODEL,)),
    }


def reference(x_prompt, x_sample, cache_cmp_k, cache_cmp_v, cache_sel_k, cache_sel_v, state_win_k, state_win_v, state_gla, page_table, rel_bias, norm_mix, w_in, nsa_pe_k, nsa_pe_v, nsa_w1_k, nsa_w1_v, nsa_w2_k, nsa_w2_v, gla_w_alpha, gla_b_alpha, gla_norm, w_branch_nsa, w_branch_gla, w_out, norm_ffn, router_w, router_b, exp_w_gate, exp_b_gate, exp_w_up, exp_b_up, exp_w_down, exp_b_down, norm_final):
    xp = x_prompt
    xs = x_sample
    p_states = []
    s_states = []
    for l in range(DEPTH):
        p = dict(norm_mix=norm_mix[l], w_in=w_in[l], nsa_pe_k=nsa_pe_k[l], nsa_pe_v=nsa_pe_v[l],
                 nsa_w1_k=nsa_w1_k[l], nsa_w1_v=nsa_w1_v[l], nsa_w2_k=nsa_w2_k[l], nsa_w2_v=nsa_w2_v[l],
                 gla_w_alpha=gla_w_alpha[l], gla_b_alpha=gla_b_alpha[l], gla_norm=gla_norm[l],
                 w_branch_nsa=w_branch_nsa[l], w_branch_gla=w_branch_gla[l], w_out=w_out[l],
                 norm_ffn=norm_ffn[l], router_w=router_w[l], router_b=router_b[l],
                 exp_w_gate=exp_w_gate[l], exp_b_gate=exp_b_gate[l], exp_w_up=exp_w_up[l],
                 exp_b_up=exp_b_up[l], exp_w_down=exp_w_down[l], exp_b_down=exp_b_down[l])
        xp, st_p = prompt_layer(xp, p, rel_bias)
        xs, st_s = sample_layer(xs, cache_cmp_k[l], cache_cmp_v[l], cache_sel_k[l], cache_sel_v[l],
                                state_win_k[l], state_win_v[l], state_gla[l], page_table, p, rel_bias)
        p_states.append(st_p)
        s_states.append(st_s)
    y_prompt = rmsnorm(xp, norm_final)
    y_sample = rmsnorm(xs, norm_final)
    p_cmp_k, p_cmp_v, p_sel_k, p_sel_v, p_win_k, p_win_v, p_gla = [jnp.stack(t, axis=0) for t in zip(*p_states)]
    s_cmp_k, s_cmp_v, s_sel_k, s_sel_v, s_win_k, s_win_v, s_gla = [jnp.stack(t, axis=0) for t in zip(*s_states)]
    return (y_prompt, y_sample, p_cmp_k, p_cmp_v, p_sel_k, p_sel_v, p_win_k, p_win_v, p_gla, s_cmp_k, s_cmp_v, s_sel_k, s_sel_v, s_win_k, s_win_v, s_gla)
```

```python
import functools
import math

import numpy as np
import jax
import jax.numpy as jnp
from jax import lax
from jax.experimental import pallas as pl
from jax.experimental.pallas import tpu as pltpu

F32 = jnp.float32
BF16 = jnp.bfloat16
I32 = jnp.int32
HI = lax.Precision.HIGHEST

D_MODEL = 1024
PAGE_SIZE = 128
NSA_HEADS = 8
NSA_KV_HEADS = 2
NSA_GROUP = NSA_HEADS // NSA_KV_HEADS
HEAD_DIM = 64
NSA_WIDTH = NSA_HEADS * HEAD_DIM
KV_WIDTH = NSA_KV_HEADS * HEAD_DIM
CMP_BLOCK = 32
CMP_STRIDE = 16
CMP_HIDDEN = 2 * HEAD_DIM
SEL_BLOCK = 64
N_SEL = 16
WINDOW = 512
Q_BLOCK = 128
SEL_FORCE = 1e9
GLA_HEADS = 4
GLA_DK = 64
GLA_DV = 128
GLA_K_WIDTH = GLA_HEADS * GLA_DK
GLA_V_WIDTH = GLA_HEADS * GLA_DV
GLA_RANK = 16
GLA_TAU = 16.0
GLA_CHUNK = 64
N_EXPERTS = 32
TOP_K = 4
D_FF = D_MODEL
SWIGLU_ALPHA = 1.702
SWIGLU_LIMIT = 7.0
REL_BUCKETS = 32
REL_EXACT = REL_BUCKETS // 2
REL_MAX_DIST = 1024
RMS_EPS = 1e-6
NEG_INF = -1e30
IN_SPLITS = (NSA_WIDTH, 6 * KV_WIDTH, 3 * NSA_HEADS, GLA_K_WIDTH, GLA_K_WIDTH, GLA_V_WIDTH, GLA_RANK,
             GLA_V_WIDTH, D_MODEL, D_MODEL)

LANES = 128
SUBLANES = 8
VMEM_LIMIT = 56 * 1024 * 1024

TM_PROJ = 256
MOE_ROWS = 256
TD_DISPATCH = 512
TC_COMBINE = 256
SEL_CHUNK = 512
LG_PAD = LANES


def _nt(a, b, **kw):
    return lax.dot_general(a, b, (((1,), (1,)), ((), ())), preferred_element_type=F32, **kw)


def _tn(a, b, **kw):
    return lax.dot_general(a, b, (((0,), (0,)), ((), ())), preferred_element_type=F32, **kw)


def _dot(a, b, **kw):
    return jnp.dot(a, b, preferred_element_type=F32, **kw)


def _params(sem, vmem=VMEM_LIMIT):
    return pltpu.CompilerParams(dimension_semantics=sem, vmem_limit_bytes=vmem)


def _masked_softmax_parts(s, valid):
    s = jnp.where(valid, s, NEG_INF)
    m = jnp.max(s, axis=-1, keepdims=True)
    e = jnp.where(valid, jnp.exp(s - m), 0.0)
    return e, jnp.maximum(jnp.sum(e, axis=-1, keepdims=True), 1e-20)


_OFF_Q = 0
_OFF_KV = _OFF_Q + NSA_WIDTH
_OFF_QL = _OFF_KV + 6 * KV_WIDTH
_OFF_KL = _OFF_QL + GLA_K_WIDTH
_OFF_VL = _OFF_KL + GLA_K_WIDTH
_OFF_R = _OFF_VL + GLA_V_WIDTH
_OFF_MA = _OFF_R + GLA_V_WIDTH
_OFF_MB = _OFF_MA + D_MODEL
_OFF_GA = _OFF_MB + D_MODEL
_N_PROJ = _OFF_GA + LG_PAD


def _inproj_kernel(x_ref, g_ref, w_ref, wal_ref, bal_ref,
                   q_o, kck_o, kcv_o, ksk_o, ksv_o, kwk_o, kwv_o, gt_o, ql_o, kl_o, vl_o, lg_o, r_o, ma_o, mb_o):
    x = x_ref[...]
    xn = x * lax.rsqrt(jnp.mean(x * x, axis=-1, keepdims=True) + RMS_EPS)
    xn = (xn * g_ref[...]).astype(BF16)

    def mm(lo, n):
        return _dot(xn, w_ref[:, lo:lo + n])

    q = mm(_OFF_Q, NSA_WIDTH) * (HEAD_DIM ** -0.5)
    for r in range(NSA_GROUP):
        q_o[r] = q[:, r * LANES:(r + 1) * LANES]
    for j, o in enumerate((kck_o, kcv_o, ksk_o, ksv_o, kwk_o, kwv_o)):
        o[...] = mm(_OFF_KV + j * KV_WIDTH, KV_WIDTH)
    ql_o[...] = mm(_OFF_QL, GLA_K_WIDTH) * (GLA_DK ** -0.5)
    kl_o[...] = mm(_OFF_KL, GLA_K_WIDTH)
    vl_o[...] = mm(_OFF_VL, GLA_V_WIDTH)
    r_o[...] = mm(_OFF_R, GLA_V_WIDTH)
    ma_o[...] = mm(_OFF_MA, D_MODEL)
    mb_o[...] = mm(_OFF_MB, D_MODEL)
    ga = mm(_OFF_GA, LG_PAD)
    gt_o[...] = jax.nn.sigmoid(ga)
    al = _dot(ga, wal_ref[...], precision=HI) + bal_ref[...]
    lg_o[...] = (jnp.minimum(al, 0.0) - jnp.log1p(jnp.exp(-jnp.abs(al)))) * (1.0 / GLA_TAU)


def _in_projection(x, norm_g, w_p, w_al, b_al):
    n = x.shape[0]
    tm = TM_PROJ
    assert n % tm == 0
    row = lambda w: pl.BlockSpec((tm, w), lambda i: (i, 0))
    full = lambda a: pl.BlockSpec(a.shape, lambda i: (0,) * a.ndim)
    widths = (KV_WIDTH,) * 6 + (LG_PAD, GLA_K_WIDTH, GLA_K_WIDTH, GLA_V_WIDTH, GLA_K_WIDTH, GLA_V_WIDTH,
                                 D_MODEL, D_MODEL)
    out_shape = [jax.ShapeDtypeStruct((NSA_GROUP, n, LANES), F32)] + [jax.ShapeDtypeStruct((n, w), F32) for w in widths]
    out_specs = [pl.BlockSpec((NSA_GROUP, tm, LANES), lambda i: (0, i, 0))] + [row(w) for w in widths]
    return pl.pallas_call(
        _inproj_kernel,
        grid=(n // tm,),
        in_specs=[row(D_MODEL), full(norm_g), full(w_p), full(w_al), full(b_al)],
        out_specs=out_specs,
        out_shape=out_shape,
        compiler_params=_params(("arbitrary",)),
        name="in_projection",
    )(x, norm_g, w_p, w_al, b_al)


def _gelu_tanh(x):
    return 0.5 * x * (1.0 + jnp.tanh(math.sqrt(2.0 / math.pi) * (x + 0.044715 * (x * x * x))))


def _compress_rows(src, n_ch, wcat_ref, pe_ref, w1f_ref, w2_ref):
    hid2 = NSA_KV_HEADS * CMP_HIDDEN
    acc = jnp.zeros((n_ch, 2 * hid2), F32)
    for l in range(CMP_STRIDE):
        xl = src[pl.ds(l, n_ch, stride=CMP_STRIDE), :].astype(BF16)
        acc = acc + _dot(xl, wcat_ref[l])
    bias = _dot(pe_ref[...], w1f_ref[...], precision=HI)[0:1]
    bias2 = jnp.concatenate([bias] * NSA_KV_HEADS, axis=1)
    nxt = pltpu.roll(acc[:, hid2:], n_ch - 1, 0)
    h = acc[:, :hid2] + nxt + bias2
    return _dot(_gelu_tanh(h).astype(BF16), w2_ref[...])


def _compress_prompt_kernel(k_ref, v_ref, wk_ref, wv_ref, pek_ref, pev_ref, w1k_ref, w1v_ref, w2k_ref, w2v_ref,
                            kc_o, vc_o):
    n_ch = kc_o.shape[0]
    kc_o[...] = _compress_rows(k_ref, n_ch, wk_ref, pek_ref, w1k_ref, w2k_ref)
    vc_o[...] = _compress_rows(v_ref, n_ch, wv_ref, pev_ref, w1v_ref, w2v_ref)


def _compress_prompt(k_cmp, v_cmp, cw, bsz, t):
    n_ch = t // CMP_STRIDE
    full = lambda a: pl.BlockSpec(a.shape, lambda b: (0,) * a.ndim)
    seq = pl.BlockSpec((t, KV_WIDTH), lambda b: (b, 0))
    out = pl.BlockSpec((n_ch, KV_WIDTH), lambda b: (b, 0))
    ws = (cw["wcat_k"], cw["wcat_v"], cw["pe_k"], cw["pe_v"], cw["w1f_k"], cw["w1f_v"], cw["w2_k"], cw["w2_v"])
    return pl.pallas_call(
        _compress_prompt_kernel,
        grid=(bsz,),
        in_specs=[seq, seq] + [full(w) for w in ws],
        out_specs=[out, out],
        out_shape=[jax.ShapeDtypeStruct((bsz * n_ch, KV_WIDTH), F32)] * 2,
        compiler_params=_params(("arbitrary",)),
        name="nsa_compress_prompt",
    )(k_cmp, v_cmp, *ws)


def _paged_fetch(pt_ref, pools, bufs, sems, seq, slot, n_pages):
    def body(p, _):
        pg = pt_ref[seq * n_pages + p]
        for j, (pool, buf) in enumerate(zip(pools, bufs)):
            pltpu.make_async_copy(pool.at[pl.ds(pg * PAGE_SIZE, PAGE_SIZE)],
                                  buf.at[slot, pl.ds(p * PAGE_SIZE, PAGE_SIZE)], sems.at[j, slot]).start()
        return 0
    lax.fori_loop(0, n_pages, body, 0)


def _paged_wait(pools, bufs, sems, slot, n_pages):
    for j, (pool, buf) in enumerate(zip(pools, bufs)):
        pltpu.make_async_copy(pool.at[pl.ds(0, n_pages * PAGE_SIZE)], buf.at[slot], sems.at[j, slot]).wait()


def _compress_sample_kernel(pt_ref, pk_hbm, pv_hbm, wk_ref, wv_ref, pek_ref, pev_ref, w1k_ref, w1v_ref,
                            w2k_ref, w2v_ref, kc_o, vc_o, bufk, bufv, sems, *, n_pages):
    s = pl.program_id(0)
    n_seq = pl.num_programs(0)
    slot = s % 2
    pools, bufs = (pk_hbm, pv_hbm), (bufk, bufv)

    @pl.when(s == 0)
    def _():
        _paged_fetch(pt_ref, pools, bufs, sems, s, slot, n_pages)

    @pl.when(s + 1 < n_seq)
    def _():
        _paged_fetch(pt_ref, pools, bufs, sems, s + 1, 1 - slot, n_pages)

    _paged_wait(pools, bufs, sems, slot, n_pages)
    n_ch = kc_o.shape[0]
    kc_o[...] = _compress_rows(bufk.at[slot], n_ch, wk_ref, pek_ref, w1k_ref, w2k_ref)
    vc_o[...] = _compress_rows(bufv.at[slot], n_ch, wv_ref, pev_ref, w1v_ref, w2v_ref)


def _compress_sample(pt_flat, pool_k, pool_v, cw, n_seq, n_pages):
    past = n_pages * PAGE_SIZE
    n_ch = past // CMP_STRIDE
    full = lambda a: pl.BlockSpec(a.shape, lambda s, pt: (0,) * a.ndim)
    hbm = pl.BlockSpec(memory_space=pl.ANY)
    out = pl.BlockSpec((n_ch, KV_WIDTH), lambda s, pt: (s, 0))
    ws = (cw["wcat_k"], cw["wcat_v"], cw["pe_k"], cw["pe_v"], cw["w1f_k"], cw["w1f_v"], cw["w2_k"], cw["w2_v"])
    return pl.pallas_call(
        functools.partial(_compress_sample_kernel, n_pages=n_pages),
        grid_spec=pltpu.PrefetchScalarGridSpec(
            num_scalar_prefetch=1,
            grid=(n_seq,),
            in_specs=[hbm, hbm] + [full(w) for w in ws],
            out_specs=[out, out],
            scratch_shapes=[pltpu.VMEM((2, past, KV_WIDTH), F32), pltpu.VMEM((2, past, KV_WIDTH), F32),
                            pltpu.SemaphoreType.DMA((2, 2))],
        ),
        out_shape=[jax.ShapeDtypeStruct((n_seq * n_ch, KV_WIDTH), F32)] * 2,
        compiler_params=_params(("arbitrary",)),
        name="nsa_compress_sample",
    )(pt_flat, pool_k, pool_v, *ws)


def _select_blocks(score, n_sel):
    blk = lax.broadcasted_iota(I32, score.shape, 1)
    cnt = jnp.zeros(score.shape, F32)
    for i in range(n_sel):
        col = score[:, i:i + 1]
        ahead = (col > score) | ((col == score) & (i < blk))
        cnt = cnt + jnp.where(ahead, 1.0, 0.0)
    n_top = min(N_SEL, n_sel)
    return jnp.where((cnt < n_top) & (blk < n_sel), 1.0, 0.0)


def _block_scores(imp, qpos, n_sel):
    blk = lax.broadcasted_iota(I32, imp.shape, 1)
    cur = qpos // SEL_BLOCK
    forced = (blk == 0) | (blk == cur) | (blk == cur - 1)
    causal = (blk * SEL_BLOCK) <= qpos
    score = jnp.where(causal, jnp.where(forced, SEL_FORCE, imp), -SEL_FORCE)
    return jnp.where(blk < n_sel, score, -3e38)


def _gate_mix(gexp, o_c, o_s, o_w, r):
    out = None
    for c, o in enumerate((o_c, o_s, o_w)):
        term = gexp[:, c * NSA_WIDTH + r * LANES:c * NSA_WIDTH + (r + 1) * LANES] * o
        out = term if out is None else out + term
    return out


def _nsa_prompt_kernel(q_ref, gt_ref, kc_ref, vc_ref, ks_ref, vs_ref, kw_ref, vw_ref,
                       gr_ref, bc_ref, mimp_ref, esel_ref, eg_ref, o_ref,
                       ksb, vsb, kwb, vwb, kcb, vcb, oc_s, os_s, ow_s, msk_s, *, voff_blk):
    qb = pl.program_id(1)
    t = ks_ref.shape[0]
    n_ch = kc_ref.shape[0]
    n_sel = t // SEL_BLOCK
    tk = msk_s.shape[3]
    n_kc = t // tk
    tiles = tk // LANES

    @pl.when(qb == 0)
    def _():
        ksb[...] = ks_ref[...].astype(BF16)
        vsb[...] = vs_ref[...].astype(BF16)
        kwb[0:WINDOW, :] = jnp.zeros((WINDOW, KV_WIDTH), BF16)
        vwb[0:WINDOW, :] = jnp.zeros((WINDOW, KV_WIDTH), BF16)
        kwb[WINDOW:, :] = kw_ref[...].astype(BF16)
        vwb[WINDOW:, :] = vw_ref[...].astype(BF16)
        kcb[...] = kc_ref[...].astype(BF16)
        vcb[...] = vc_ref[...].astype(BF16)

    q0 = qb * Q_BLOCK
    lane = lax.broadcasted_iota(I32, (Q_BLOCK, LANES), 1)
    qpos = q0 + lax.broadcasted_iota(I32, (Q_BLOCK, 1), 0)
    upper = lane >= HEAD_DIM

    def masked_q(g, r):
        keep = upper if g == 1 else jnp.logical_not(upper)
        return jnp.where(keep, q_ref[r], 0.0).astype(BF16)

    cend = lax.broadcasted_iota(I32, (Q_BLOCK, n_ch), 1) * CMP_STRIDE + (CMP_BLOCK - 1)
    valid_c = qpos >= cend
    for g in range(NSA_KV_HEADS):
        psum = jnp.zeros((Q_BLOCK, n_ch), F32)
        for r in range(NSA_GROUP):
            h = g * NSA_GROUP + r
            s = _nt(masked_q(g, r), kcb[...]) + bc_ref[0, h]
            e, den = _masked_softmax_parts(s, valid_c)
            p = e / den
            psum = psum + p
            oc_s[h] = _dot(p.astype(BF16), vcb[...])
        imp = _dot(psum, mimp_ref[...], precision=HI)
        sel = _select_blocks(_block_scores(imp, qpos, n_sel), n_sel).astype(BF16)
        for c in range(n_kc):
            msk_s[g, c] = _dot(sel, esel_ref[:, c * tk:(c + 1) * tk])

    n_live = (q0 + Q_BLOCK + tk - 1) // tk
    wpos = q0 - WINDOW + lax.broadcasted_iota(I32, (Q_BLOCK, WINDOW + Q_BLOCK), 1)
    dist_w = qpos - wpos
    valid_w = (dist_w >= 0) & (dist_w < WINDOW) & (wpos >= 0)
    w_blk = voff_blk - WINDOW // LANES
    for g in range(NSA_KV_HEADS):
        for r in range(NSA_GROUP):
            h = g * NSA_GROUP + r
            qm = masked_q(g, r)

            def chunk(kt, carry, qm=qm, h=h, g=g):
                m, l, acc = carry
                k0 = pl.multiple_of(kt * tk, tk)
                s = _nt(qm, ksb[pl.ds(k0, tk), :])
                base = voff_blk - qb + kt * tiles
                s = s + jnp.concatenate([gr_ref[base + j, h] for j in range(tiles)], axis=1)
                kpos = k0 + lax.broadcasted_iota(I32, (Q_BLOCK, tk), 1)
                valid = (msk_s[g, kt] > 0.5) & (kpos <= qpos)
                s = jnp.where(valid, s, NEG_INF)
                m_new = jnp.maximum(m, jnp.max(s, axis=-1, keepdims=True))
                alpha = jnp.exp(m - m_new)
                e = jnp.where(valid, jnp.exp(s - m_new), 0.0)
                l = alpha * l + jnp.sum(e, axis=-1, keepdims=True)
                acc = alpha * acc + _dot(e.astype(BF16), vsb[pl.ds(k0, tk), :])
                return m_new, l, acc

            init = (jnp.full((Q_BLOCK, 1), NEG_INF, F32), jnp.zeros((Q_BLOCK, 1), F32),
                    jnp.zeros((Q_BLOCK, KV_WIDTH), F32))
            _, l, acc = lax.fori_loop(0, n_live, chunk, init)
            os_s[h] = acc / jnp.maximum(l, 1e-20)

            s = _nt(qm, kwb[pl.ds(pl.multiple_of(q0, Q_BLOCK), WINDOW + Q_BLOCK), :])
            s = s + jnp.concatenate([gr_ref[w_blk + j, h] for j in range((WINDOW + Q_BLOCK) // LANES)], axis=1)
            e, den = _masked_softmax_parts(s, valid_w)
            ow_s[h] = _dot(e.astype(BF16), vwb[pl.ds(pl.multiple_of(q0, Q_BLOCK), WINDOW + Q_BLOCK), :]) / den

    gexp = _dot(gt_ref[...], eg_ref[...], precision=HI)
    for r in range(NSA_GROUP):
        pick = lambda ref: jnp.where(upper, ref[NSA_GROUP + r], ref[r])
        o_ref[r] = _gate_mix(gexp, pick(oc_s), pick(os_s), pick(ow_s), r)


def _nsa_prompt(q4, gates, kc, vc, ks, vs, kw, vw, tabs, bsz, t):
    n_qb = t // Q_BLOCK
    n_ch = t // CMP_STRIDE
    tk = min(SEL_CHUNK, t)
    full = lambda a: pl.BlockSpec(a.shape, lambda b, i: (0,) * a.ndim)
    seq = pl.BlockSpec((t, KV_WIDTH), lambda b, i: (b, 0))
    cseq = pl.BlockSpec((n_ch, KV_WIDTH), lambda b, i: (b, 0))
    qspec = pl.BlockSpec((NSA_GROUP, Q_BLOCK, LANES), lambda b, i: (0, b * n_qb + i, 0))
    gr, bc, mimp, esel, eg = tabs["gr"], tabs["bc"], tabs["mimp"], tabs["esel"], tabs["eg"]
    head_tile = pltpu.VMEM((NSA_HEADS, Q_BLOCK, KV_WIDTH), F32)
    return pl.pallas_call(
        functools.partial(_nsa_prompt_kernel, voff_blk=tabs["voff"] // LANES),
        grid=(bsz, n_qb),
        in_specs=[qspec, pl.BlockSpec((Q_BLOCK, LG_PAD), lambda b, i: (b * n_qb + i, 0)),
                  cseq, cseq, seq, seq, seq, seq, full(gr),
                  pl.BlockSpec((1,) + bc.shape[1:], lambda b, i: (i, 0, 0, 0)), full(mimp), full(esel), full(eg)],
        out_specs=qspec,
        out_shape=jax.ShapeDtypeStruct((NSA_GROUP, bsz * t, LANES), F32),
        scratch_shapes=[pltpu.VMEM((t, KV_WIDTH), BF16), pltpu.VMEM((t, KV_WIDTH), BF16),
                        pltpu.VMEM((t + WINDOW, KV_WIDTH), BF16), pltpu.VMEM((t + WINDOW, KV_WIDTH), BF16),
                        pltpu.VMEM((n_ch, KV_WIDTH), BF16), pltpu.VMEM((n_ch, KV_WIDTH), BF16),
                        head_tile, head_tile, head_tile,
                        pltpu.VMEM((NSA_KV_HEADS, t // tk, Q_BLOCK, tk), F32)],
        compiler_params=_params(("arbitrary", "arbitrary")),
        name="nsa_attention_prompt",
    )(q4, gates, kc, vc, ks, vs, kw, vw, gr, bc, mimp, esel, eg)


def _nsa_sample_kernel(pt_ref, q_ref, gt_ref, kc_ref, vc_ref, ksn_ref, vsn_ref, kwn_ref, vwn_ref, bkw_ref, bvw_ref,
                       pks_hbm, pvs_hbm, gs_ref, gn_ref, bcs_ref, bws_ref, mimp_ref, esel_ref, eg_ref, o_ref,
                       bufk, bufv, sems, ksb, vsb, *, n_pages):
    sq = pl.program_id(0)
    n_seq = pl.num_programs(0)
    slot = sq % 2
    pools, bufs = (pks_hbm, pvs_hbm), (bufk, bufv)
    past = n_pages * PAGE_SIZE
    s_new = ksn_ref.shape[0]
    rows = NSA_GROUP * s_new
    n_cmp_rows = kc_ref.shape[0]
    n_sel = (past + s_new + SEL_BLOCK - 1) // SEL_BLOCK
    n_past_blk = past // SEL_BLOCK
    w_buf = bkw_ref.shape[0]

    @pl.when(sq == 0)
    def _():
        _paged_fetch(pt_ref, pools, bufs, sems, sq, slot, n_pages)

    @pl.when(sq + 1 < n_seq)
    def _():
        _paged_fetch(pt_ref, pools, bufs, sems, sq + 1, 1 - slot, n_pages)

    _paged_wait(pools, bufs, sems, slot, n_pages)
    ksb[...] = bufk[slot].astype(BF16)
    vsb[...] = bufv[slot].astype(BF16)

    lane = lax.broadcasted_iota(I32, (rows, LANES), 1)
    upper = lane >= HEAD_DIM
    qi = lax.broadcasted_iota(I32, (rows, 1), 0) % s_new
    pad_new = lambda ref: jnp.concatenate([ref[...], jnp.zeros((LANES - s_new, KV_WIDTH), F32)], axis=0).astype(BF16)
    ksn, vsn, kwn, vwn = pad_new(ksn_ref), pad_new(vsn_ref), pad_new(kwn_ref), pad_new(vwn_ref)
    kcb, vcb = kc_ref[...].astype(BF16), vc_ref[...].astype(BF16)
    bkw, bvw = bkw_ref[...].astype(BF16), bvw_ref[...].astype(BF16)
    new_causal = (lane < s_new) & (lane <= qi)

    def stacked_q(g):
        keep = upper if g == 1 else jnp.logical_not(upper)
        qs = jnp.concatenate([q_ref[r] for r in range(NSA_GROUP)], axis=0)
        return jnp.where(keep, qs, 0.0).astype(BF16)

    cend = lax.broadcasted_iota(I32, (rows, n_cmp_rows), 1) * CMP_STRIDE + (CMP_BLOCK - 1)
    valid_c = (past + qi) >= cend
    o_c, psums = [], []
    for g in range(NSA_KV_HEADS):
        s = _nt(stacked_q(g), kcb) + bcs_ref[g]
        e, den = _masked_softmax_parts(s, valid_c)
        p = e / den
        o_c.append(_dot(p.astype(BF16), vcb))
        ps = p[0:s_new]
        for r in range(1, NSA_GROUP):
            ps = ps + p[r * s_new:(r + 1) * s_new]
        psums.append(ps)
    imp = _dot(jnp.concatenate(psums, axis=0), mimp_ref[...], precision=HI)
    qpos_sel = past + lax.broadcasted_iota(I32, (NSA_KV_HEADS * s_new, 1), 0) % s_new
    sel = _select_blocks(_block_scores(imp, qpos_sel, n_sel), n_sel)
    mask_past = _dot(sel[:, :n_past_blk].astype(BF16), esel_ref[...])
    sel_new = sel[:, n_past_blk:n_past_blk + 1]

    gexp = _dot(gt_ref[...], eg_ref[...], precision=HI)
    o_s, o_w = [], []
    for g in range(NSA_KV_HEADS):
        qg = stacked_q(g)
        tile_rows = lambda a: jnp.concatenate([a[g * s_new:(g + 1) * s_new]] * NSA_GROUP, axis=0)
        valid_p = tile_rows(mask_past) > 0.5
        s_p = jnp.where(valid_p, _nt(qg, ksb[...]) + gs_ref[g], NEG_INF)
        valid_n = new_causal & (tile_rows(sel_new) > 0.5)
        s_n = jnp.where(valid_n, _nt(qg, ksn) + gn_ref[g], NEG_INF)
        m = jnp.maximum(jnp.max(s_p, axis=-1, keepdims=True), jnp.max(s_n, axis=-1, keepdims=True))
        e_p = jnp.where(valid_p, jnp.exp(s_p - m), 0.0)
        e_n = jnp.where(valid_n, jnp.exp(s_n - m), 0.0)
        den = jnp.maximum(jnp.sum(e_p, axis=-1, keepdims=True) + jnp.sum(e_n, axis=-1, keepdims=True), 1e-20)
        o_s.append((_dot(e_p.astype(BF16), vsb[...]) + _dot(e_n.astype(BF16), vsn)) / den)
        jb = lax.broadcasted_iota(I32, (rows, w_buf), 1)
        dist_b = w_buf + qi - jb
        valid_b = (dist_b < WINDOW) & (past - w_buf + jb >= 0)
        s_b = jnp.where(valid_b, _nt(qg, bkw) + bws_ref[g], NEG_INF)
        s_n = jnp.where(new_causal, _nt(qg, kwn) + gn_ref[g], NEG_INF)
        m = jnp.maximum(jnp.max(s_b, axis=-1, keepdims=True), jnp.max(s_n, axis=-1, keepdims=True))
        e_b = jnp.where(valid_b, jnp.exp(s_b - m), 0.0)
        e_n = jnp.where(new_causal, jnp.exp(s_n - m), 0.0)
        den = jnp.maximum(jnp.sum(e_b, axis=-1, keepdims=True) + jnp.sum(e_n, axis=-1, keepdims=True), 1e-20)
        o_w.append((_dot(e_b.astype(BF16), bvw) + _dot(e_n.astype(BF16), vwn)) / den)

    up8 = upper[0:s_new]
    for r in range(NSA_GROUP):
        pick = lambda o: jnp.where(up8, o[1][r * s_new:(r + 1) * s_new], o[0][r * s_new:(r + 1) * s_new])
        o_ref[r] = _gate_mix(gexp, pick(o_c), pick(o_s), pick(o_w), r)


def _nsa_sample(pt_flat, q4, gates, kc, vc, ksn, vsn, kwn, vwn, buf_kw, buf_vw, pool_ks, pool_vs, tabs,
                n_seq, s_new, n_pages):
    past = n_pages * PAGE_SIZE
    n_ch = past // CMP_STRIDE
    w_buf = buf_kw.shape[0] // n_seq
    full = lambda a: pl.BlockSpec(a.shape, lambda s, pt: (0,) * a.ndim)
    hbm = pl.BlockSpec(memory_space=pl.ANY)
    rows = lambda n, w: pl.BlockSpec((n, w), lambda s, pt: (s, 0))
    qspec = pl.BlockSpec((NSA_GROUP, s_new, LANES), lambda s, pt: (0, s, 0))
    consts = (tabs["gs"], tabs["gn"], tabs["bcs"], tabs["bws"], tabs["mimp_s"], tabs["esel_s"], tabs["eg"])
    return pl.pallas_call(
        functools.partial(_nsa_sample_kernel, n_pages=n_pages),
        grid_spec=pltpu.PrefetchScalarGridSpec(
            num_scalar_prefetch=1,
            grid=(n_seq,),
            in_specs=[qspec, rows(s_new, LG_PAD), rows(n_ch, KV_WIDTH), rows(n_ch, KV_WIDTH)]
                     + [rows(s_new, KV_WIDTH)] * 4 + [rows(w_buf, KV_WIDTH)] * 2 + [hbm, hbm]
                     + [full(c) for c in consts],
            out_specs=qspec,
            scratch_shapes=[pltpu.VMEM((2, past, KV_WIDTH), F32), pltpu.VMEM((2, past, KV_WIDTH), F32),
                            pltpu.SemaphoreType.DMA((2, 2)),
                            pltpu.VMEM((past, KV_WIDTH), BF16), pltpu.VMEM((past, KV_WIDTH), BF16)],
        ),
        out_shape=jax.ShapeDtypeStruct((NSA_GROUP, n_seq * s_new, LANES), F32),
        compiler_params=_params(("arbitrary",)),
        name="nsa_attention_sample",
    )(pt_flat, q4, gates, kc, vc, ksn, vsn, kwn, vwn, buf_kw, buf_vw, pool_ks, pool_vs, *consts)


def _gla_kernel(q_ref, k_ref, v_ref, lg_ref, r_ref, gn_ref, s0_ref, o_ref, sfin_ref, s_scr):
    c = pl.program_id(1)
    n_c = pl.num_programs(1)
    cl = q_ref.shape[0]

    @pl.when(c == 0)
    def _():
        s_scr[...] = s0_ref[...]

    row_t = lax.broadcasted_iota(I32, (cl, LANES), 0)
    lane_t = lax.broadcasted_iota(I32, (cl, LANES), 1)
    causal = lax.broadcasted_iota(I32, (cl, cl), 1) <= lax.broadcasted_iota(I32, (cl, cl), 0)
    row_s = lax.broadcasted_iota(I32, (LANES, GLA_DV), 0)
    pair = LANES // GLA_DK
    for p in range(GLA_HEADS // pair):
        cols = slice(p * LANES, (p + 1) * LANES)
        lg = lg_ref[:, cols]
        b = lg
        sh = 1
        while sh < cl:
            b = b + jnp.where(row_t >= sh, pltpu.roll(b, sh, 0), 0.0)
            sh *= 2
        b_last = b[cl - 1:cl, :]
        qt = q_ref[:, cols] * jnp.exp(b)
        kp = k_ref[:, cols]
        kt = (kp * jnp.exp(-b)).astype(BF16)
        khat = (kp * jnp.exp(b_last - b)).astype(BF16)
        dec = jnp.exp(jnp.broadcast_to(b_last, (LANES, LANES))).T
        s_old = s_scr[cols, :]
        s_bf = s_old.astype(BF16)
        upd = jnp.zeros((LANES, GLA_DV), F32)
        for hh in range(pair):
            h = p * pair + hh
            vcols = slice(h * GLA_DV, (h + 1) * GLA_DV)
            mine = (lane_t >= GLA_DK) if hh == 1 else (lane_t < GLA_DK)
            qm = jnp.where(mine, qt, 0.0).astype(BF16)
            att = jnp.where(causal, _nt(qm, kt), 0.0)
            vh = v_ref[:, vcols].astype(BF16)
            o = _dot(qm, s_bf) + _dot(att.astype(BF16), vh)
            o = o * lax.rsqrt(jnp.mean(o * o, axis=-1, keepdims=True) + RMS_EPS) * gn_ref[...]
            rh = r_ref[:, vcols]
            o_ref[:, vcols] = o * (rh * jax.nn.sigmoid(rh))
            u = _tn(khat, vh)
            upd = jnp.where((row_s >= GLA_DK) == (hh == 1), u, upd)
        s_scr[cols, :] = s_old * dec + upd

    @pl.when(c == n_c - 1)
    def _():
        sfin_ref[...] = s_scr[...]


def _gla(q_l, k_l, v_l, lg, r, g_norm, s0, bsz, n_c, cl):
    srows = GLA_HEADS * GLA_DK
    blk = lambda w: pl.BlockSpec((cl, w), lambda b, c: (b * n_c + c, 0))
    st = pl.BlockSpec((srows, GLA_DV), lambda b, c: (b, 0))
    return pl.pallas_call(
        _gla_kernel,
        grid=(bsz, n_c),
        in_specs=[blk(GLA_K_WIDTH), blk(GLA_K_WIDTH), blk(GLA_V_WIDTH), blk(GLA_K_WIDTH), blk(GLA_V_WIDTH),
                  pl.BlockSpec(g_norm.shape, lambda b, c: (0, 0)), st],
        out_specs=[blk(GLA_V_WIDTH), st],
        out_shape=[jax.ShapeDtypeStruct((bsz * n_c * cl, GLA_V_WIDTH), F32),
                   jax.ShapeDtypeStruct((bsz * srows, GLA_DV), F32)],
        scratch_shapes=[pltpu.VMEM((srows, GLA_DV), F32)],
        compiler_params=_params(("arbitrary", "arbitrary")),
        name="gla",
    )(q_l, k_l, v_l, lg, r, g_norm, s0)


def _merge_kernel(x_ref, on_ref, og_ref, ma_ref, mb_ref, wn_ref, wg_ref, wo_ref, gf_ref, rw_ref, rb_ref,
                  x1_o, h2t_o, ei_o, gw_o, rk_o, cnt_o, carry):
    i = pl.program_id(0)
    tm = x_ref.shape[0]

    @pl.when(i == 0)
    def _():
        carry[...] = jnp.zeros(carry.shape, F32)

    on = jnp.concatenate([on_ref[r] for r in range(NSA_GROUP)], axis=1).astype(BF16)
    ya = _dot(on, wn_ref[...])
    yb = _dot(og_ref[...].astype(BF16), wg_ref[...])
    m = jax.nn.sigmoid(ma_ref[...]) * ya + jax.nn.sigmoid(mb_ref[...]) * yb
    x1 = x_ref[...] + _dot(m.astype(BF16), wo_ref[...])
    x1_o[...] = x1
    h2 = x1 * lax.rsqrt(jnp.mean(x1 * x1, axis=-1, keepdims=True) + RMS_EPS) * gf_ref[...]
    for s in range(D_MODEL // LANES):
        h2t_o[pl.ds(s, tm, stride=SUBLANES), :] = h2[:, s * LANES:(s + 1) * LANES]

    logits = _dot(h2, rw_ref[...], precision=HI) + rb_ref[...]
    lane = lax.broadcasted_iota(I32, (tm, LANES), 1)
    lane_f = lane.astype(F32)
    work = logits
    vals, idxs = [], []
    for _ in range(TOP_K):
        mk = jnp.max(work, axis=-1, keepdims=True)
        ik = jnp.min(jnp.where(work == mk, lane_f, float(LANES)), axis=-1, keepdims=True)
        vals.append(mk)
        idxs.append(ik)
        work = jnp.where(lane_f == ik, -jnp.inf, work)
    es = [jnp.exp(v - vals[0]) for v in vals]
    den = es[0]
    for e in es[1:]:
        den = den + e
    onehot = jnp.zeros((tm, LANES), F32)
    for ik in idxs:
        onehot = onehot + jnp.where(lane_f == ik, 1.0, 0.0)
    below = lax.broadcasted_iota(I32, (tm, tm), 1) < lax.broadcasted_iota(I32, (tm, tm), 0)
    before = _dot(jnp.where(below, 1.0, 0.0).astype(BF16), onehot.astype(BF16)) + carry[0:1, :]
    ei = jnp.zeros((tm, LANES), F32)
    gw = jnp.zeros((tm, LANES), F32)
    rk = jnp.zeros((tm, LANES), F32)
    for k in range(TOP_K):
        rank_k = jnp.sum(jnp.where(lane_f == idxs[k], before, 0.0), axis=-1, keepdims=True)
        ei = jnp.where(lane == k, idxs[k], ei)
        gw = jnp.where(lane == k, es[k] / den, gw)
        rk = jnp.where(lane == k, rank_k, rk)
    ei_o[...] = ei.astype(I32)
    gw_o[...] = gw
    rk_o[...] = rk.astype(I32)
    carry[0:1, :] = carry[0:1, :] + jnp.sum(onehot, axis=0, keepdims=True)
    cnt_o[...] = carry[...]


def _merge(x, o_nsa4, o_gla, m_a, m_b, mw):
    n = x.shape[0]
    tm = TM_PROJ
    row = lambda w: pl.BlockSpec((tm, w), lambda i: (i, 0))
    full = lambda a: pl.BlockSpec(a.shape, lambda i: (0,) * a.ndim)
    ws = (mw["wn"], mw["wg"], mw["wo"], mw["gf"], mw["rw"], mw["rb"])
    return pl.pallas_call(
        _merge_kernel,
        grid=(n // tm,),
        in_specs=[row(D_MODEL), pl.BlockSpec((NSA_GROUP, tm, LANES), lambda i: (0, i, 0)), row(GLA_V_WIDTH),
                  row(D_MODEL), row(D_MODEL)] + [full(w) for w in ws],
        out_specs=[row(D_MODEL), pl.BlockSpec((tm * SUBLANES, LANES), lambda i: (i, 0)),
                   row(LANES), row(LANES), row(LANES), pl.BlockSpec((SUBLANES, LANES), lambda i: (0, 0))],
        out_shape=[jax.ShapeDtypeStruct((n, D_MODEL), F32), jax.ShapeDtypeStruct((n * SUBLANES, LANES), F32),
                   jax.ShapeDtypeStruct((n, LANES), I32), jax.ShapeDtypeStruct((n, LANES), F32),
                   jax.ShapeDtypeStruct((n, LANES), I32), jax.ShapeDtypeStruct((SUBLANES, LANES), F32)],
        scratch_shapes=[pltpu.VMEM((SUBLANES, LANES), F32)],
        compiler_params=_params(("arbitrary",)),
        name="merge_router",
    )(x, o_nsa4, o_gla, m_a, m_b, *ws)


def _token_tile(ref, row):
    return ref.at[pl.ds(row * SUBLANES, SUBLANES)]


def _dispatch_kernel(meta_ref, dest_ref, h_hbm, xs_hbm, dsm, ztile, sem_idx, sem_row, sem_pad, *, td):
    i = pl.program_id(0)
    n_steps = pl.num_programs(0)
    idx_copy = pltpu.make_async_copy(dest_ref, dsm, sem_idx)
    idx_copy.start()
    idx_copy.wait()

    def body(tk, _):
        for k in range(TOP_K):
            j = tk * TOP_K + k
            d = dsm[j // LANES, j % LANES]
            pltpu.make_async_copy(_token_tile(h_hbm, i * td + tk), _token_tile(xs_hbm, d), sem_row).start()
        return 0
    lax.fori_loop(0, td, body, 0)

    @pl.when(i == n_steps - 1)
    def _():
        ztile[...] = jnp.zeros(ztile.shape, F32)
        blk_rows = ztile.shape[0]
        n_blocks = xs_hbm.shape[0] // blk_rows

        def pads(start_or_wait):
            def per_expert(e, _):
                first = meta_ref[N_EXPERTS + e] + meta_ref[e]
                last = meta_ref[N_EXPERTS + e] + meta_ref[2 * N_EXPERTS + e]

                def per_row(rw, _):
                    cp = pltpu.make_async_copy(ztile.at[pl.ds(0, SUBLANES)], _token_tile(xs_hbm, rw), sem_pad)
                    cp.start() if start_or_wait else cp.wait()
                    return 0
                lax.fori_loop(first, last, per_row, 0)
                return 0
            lax.fori_loop(0, N_EXPERTS, per_expert, 0)

            def per_block(bk, _):
                cp = pltpu.make_async_copy(ztile, xs_hbm.at[pl.ds(bk * blk_rows, blk_rows)], sem_pad)
                cp.start() if start_or_wait else cp.wait()
                return 0
            lax.fori_loop(meta_ref[3 * N_EXPERTS], n_blocks, per_block, 0)
        pads(True)
        pads(False)

    n_rows = td * TOP_K * SUBLANES
    pltpu.make_async_copy(h_hbm.at[pl.ds(0, n_rows)], xs_hbm.at[pl.ds(0, n_rows)], sem_row).wait()


def _dispatch(meta, dest2d, h2t, n_rows_total, td):
    n = h2t.shape[0] // SUBLANES
    hbm = pl.BlockSpec(memory_space=pl.ANY)
    drows = td * TOP_K // LANES
    return pl.pallas_call(
        functools.partial(_dispatch_kernel, td=td),
        grid_spec=pltpu.PrefetchScalarGridSpec(
            num_scalar_prefetch=1,
            grid=(n // td,),
            in_specs=[pl.BlockSpec((drows, LANES), lambda i, m: (i, 0)), hbm],
            out_specs=hbm,
            scratch_shapes=[pltpu.SMEM((drows, LANES), I32), pltpu.VMEM((MOE_ROWS * SUBLANES, LANES), F32),
                            pltpu.SemaphoreType.DMA, pltpu.SemaphoreType.DMA, pltpu.SemaphoreType.DMA],
        ),
        out_shape=jax.ShapeDtypeStruct((n_rows_total * SUBLANES, LANES), F32),
        compiler_params=_params(("arbitrary",)),
        name="moe_dispatch",
    )(meta, dest2d, h2t)


def _moe_kernel(be_ref, nu_ref, xs_ref, wg_ref, bg_ref, wu_ref, bu_ref, wd_ref, bd_ref, y_ref):
    i = pl.program_id(0)
    rows = xs_ref.shape[0] // SUBLANES
    n_s = D_MODEL // LANES

    @pl.when(i < nu_ref[0])
    def _():
        xb = jnp.concatenate([xs_ref[pl.ds(s, rows, stride=SUBLANES), :] for s in range(n_s)], axis=1).astype(BF16)
        g = _dot(xb, wg_ref[0]) + bg_ref[0]
        u = _dot(xb, wu_ref[0]) + bu_ref[0]
        g = jnp.minimum(g, SWIGLU_LIMIT)
        u = jnp.clip(u, -SWIGLU_LIMIT, SWIGLU_LIMIT)
        hh = (u + 1.0) * (g * jax.nn.sigmoid(SWIGLU_ALPHA * g))
        y = _dot(hh.astype(BF16), wd_ref[0]) + bd_ref[0]
        for s in range(n_s):
            y_ref[pl.ds(s, rows, stride=SUBLANES), :] = y[:, s * LANES:(s + 1) * LANES]

    @pl.when(i >= nu_ref[0])
    def _():
        y_ref[...] = jnp.zeros(y_ref.shape, F32)


def _moe_experts(block_e, n_used, xs, ew, n_blocks):
    blk = lambda i, be, nu: jnp.minimum(i, nu[0] - 1)
    rows = pl.BlockSpec((MOE_ROWS * SUBLANES, LANES), lambda i, be, nu: (i, 0))
    wspec = lambda a: pl.BlockSpec((1,) + a.shape[1:], lambda i, be, nu: (be[blk(i, be, nu)], 0, 0))
    ws = (ew["wg"], ew["bg"], ew["wu"], ew["bu"], ew["wd"], ew["bd"])
    return pl.pallas_call(
        _moe_kernel,
        grid_spec=pltpu.PrefetchScalarGridSpec(
            num_scalar_prefetch=2,
            grid=(n_blocks,),
            in_specs=[rows] + [wspec(w) for w in ws],
            out_specs=rows,
        ),
        out_shape=jax.ShapeDtypeStruct(xs.shape, F32),
        compiler_params=_params(("arbitrary",)),
        name="moe_experts",
    )(block_e, n_used, xs, *ws)


def _combine_kernel(dest_ref, gw_ref, x1_ref, gfin_ref, y_hbm, out_ref, dsm, buf, sem_idx, sem_row):
    tc = x1_ref.shape[0]
    idx_copy = pltpu.make_async_copy(dest_ref, dsm, sem_idx)
    idx_copy.start()
    idx_copy.wait()

    def body(tk, _):
        for k in range(TOP_K):
            j = tk * TOP_K + k
            d = dsm[j // LANES, j % LANES]
            pltpu.make_async_copy(_token_tile(y_hbm, d), _token_tile(buf.at[k], tk), sem_row).start()
        return 0
    lax.fori_loop(0, tc, body, 0)
    for k in range(TOP_K):
        pltpu.make_async_copy(y_hbm.at[pl.ds(0, tc * SUBLANES)], buf.at[k], sem_row).wait()

    gw = gw_ref[...]
    parts = []
    for s in range(D_MODEL // LANES):
        acc = None
        for k in range(TOP_K):
            term = buf[k, pl.ds(s, tc, stride=SUBLANES), :] * gw[:, k:k + 1]
            acc = term if acc is None else acc + term
        parts.append(acc)
    x2 = x1_ref[...] + jnp.concatenate(parts, axis=1)
    out_ref[...] = x2 * lax.rsqrt(jnp.mean(x2 * x2, axis=-1, keepdims=True) + RMS_EPS) * gfin_ref[...]


def _combine(dest2d, gw, x1, g_final, y_rows):
    n = x1.shape[0]
    tc = TC_COMBINE
    drows = tc * TOP_K // LANES
    row = lambda w: pl.BlockSpec((tc, w), lambda i: (i, 0))
    return pl.pallas_call(
        _combine_kernel,
        grid=(n // tc,),
        in_specs=[pl.BlockSpec((drows, LANES), lambda i: (i, 0)), row(LANES), row(D_MODEL),
                  pl.BlockSpec(g_final.shape, lambda i: (0, 0)), pl.BlockSpec(memory_space=pl.ANY)],
        out_specs=row(D_MODEL),
        out_shape=jax.ShapeDtypeStruct((n, D_MODEL), F32),
        scratch_shapes=[pltpu.SMEM((drows, LANES), I32), pltpu.VMEM((TOP_K, tc * SUBLANES, LANES), F32),
                        pltpu.SemaphoreType.DMA, pltpu.SemaphoreType.DMA],
        compiler_params=_params(("arbitrary",)),
        name="moe_combine",
    )(dest2d, gw, x1, g_final, y_rows)


def _bucket_table(max_dist):
    n = np.arange(max_dist, dtype=np.int64)
    scaled = np.log(np.maximum(n, 1).astype(np.float64) / REL_EXACT) / math.log(REL_MAX_DIST / REL_EXACT)
    large = REL_EXACT + (scaled * (REL_BUCKETS - REL_EXACT)).astype(np.int64)
    return np.where(n < REL_EXACT, n, np.minimum(large, REL_BUCKETS - 1)).astype(np.int32)


def _bias_lookup(rel_bias, dist):
    d = np.maximum(dist, 0)
    buckets = _bucket_table(int(d.max()) + 1)[d]
    return jnp.take(rel_bias.astype(F32).T, jnp.asarray(buckets), axis=1)


def _importance_matrix(n_rows, n_cmp, n_sel, n_cols):
    rc = CMP_BLOCK // CMP_STRIDE
    rs = SEL_BLOCK // CMP_STRIDE
    m = np.zeros((n_rows, n_cols), np.float32)
    for j in range(n_sel):
        for o in range(rs + rc - 1):
            w = min(o - (rc - 1) + rc, rs) - max(o - (rc - 1), 0)
            c = rs * j + o - (rc - 1)
            if 0 <= c < n_cmp:
                m[c, j] += w
    return jnp.asarray(m)


def _block_expand(n_blocks, n_rows=LANES):
    e = np.zeros((n_rows, n_blocks * SEL_BLOCK), np.float32)
    for j in range(n_blocks):
        e[j, j * SEL_BLOCK:(j + 1) * SEL_BLOCK] = 1.0
    return jnp.asarray(e, dtype=BF16)


def _gate_expand():
    e = np.zeros((LG_PAD, 3 * NSA_WIDTH), np.float32)
    for g in range(NSA_KV_HEADS):
        for r in range(NSA_GROUP):
            for c in range(3):
                lo = c * NSA_WIDTH + r * LANES + g * HEAD_DIM
                e[(g * NSA_GROUP + r) * 3 + c, lo:lo + HEAD_DIM] = 1.0
    return jnp.asarray(e)


def _prompt_tables(rel_bias, t):
    n_qb = t // Q_BLOCK
    n_ch = t // CMP_STRIDE
    tk = min(SEL_CHUNK, t)
    voff = max(t - Q_BLOCK, WINDOW)
    nv = voff // LANES + tk // LANES
    i = np.arange(Q_BLOCK)
    v = np.arange(nv * LANES).reshape(nv, 1, LANES)
    gr = _bias_lookup(rel_bias, voff - v + i[None, :, None]).transpose(1, 0, 2, 3)
    cend = np.arange(n_ch) * CMP_STRIDE + (CMP_BLOCK - 1)
    qpos = (np.arange(n_qb)[:, None] * Q_BLOCK + i[None, :])
    bc = _bias_lookup(rel_bias, qpos[:, :, None] - cend[None, None, :]).transpose(1, 0, 2, 3)
    return dict(gr=gr, bc=bc, voff=voff,
                mimp=_importance_matrix(n_ch, n_ch - 1, t // SEL_BLOCK, LANES),
                esel=_block_expand(t // SEL_BLOCK), eg=_gate_expand())


def _sample_tables(rel_bias, past, s_new, w_buf):
    n_ch = past // CMP_STRIDE
    n_sel = (past + s_new + SEL_BLOCK - 1) // SEL_BLOCK
    rows = np.arange(NSA_GROUP * s_new) % s_new
    per_group = lambda a: a.reshape(NSA_KV_HEADS, NSA_GROUP, s_new, -1).reshape(NSA_KV_HEADS, NSA_GROUP * s_new, -1)
    qi = np.arange(s_new)

    def table(dist):
        return per_group(_bias_lookup(rel_bias, dist))
    del rows
    gs = table(past + qi[:, None] - np.arange(past)[None, :])
    jn = np.arange(LANES)
    gn = table(np.where(jn[None, :] < s_new, qi[:, None] - jn[None, :], 0))
    cend = np.arange(n_ch) * CMP_STRIDE + (CMP_BLOCK - 1)
    bcs = table(past + qi[:, None] - cend[None, :])
    bws = table(w_buf + qi[:, None] - np.arange(w_buf)[None, :])
    sel_lanes = -(-n_sel // LANES) * LANES
    return dict(gs=gs, gn=gn, bcs=bcs, bws=bws,
                mimp_s=_importance_matrix(n_ch, n_ch - 1, n_sel, sel_lanes),
                esel_s=_block_expand(past // SEL_BLOCK, past // SEL_BLOCK), eg=_gate_expand())


def _compress_weights(pe, w1, w2):
    rc = CMP_BLOCK // CMP_STRIDE
    w1r = w1.reshape(rc, CMP_STRIDE, HEAD_DIM, CMP_HIDDEN)
    eye = jnp.eye(NSA_KV_HEADS, dtype=w1.dtype)
    wcat = jnp.einsum("rldh,ge->lgdreh", w1r, eye).reshape(CMP_STRIDE, KV_WIDTH, rc * NSA_KV_HEADS * CMP_HIDDEN)
    w2bd = jnp.einsum("hd,ge->ghed", w2, eye).reshape(NSA_KV_HEADS * CMP_HIDDEN, KV_WIDTH)
    pe_rows = jnp.concatenate([pe.reshape(1, CMP_BLOCK * HEAD_DIM),
                               jnp.zeros((SUBLANES - 1, CMP_BLOCK * HEAD_DIM), pe.dtype)], axis=0)
    return wcat.astype(BF16), pe_rows, w1.reshape(CMP_BLOCK * HEAD_DIM, CMP_HIDDEN), w2bd.astype(BF16)


def _layer_weights(w_in, gla_w_alpha, gla_b_alpha, w_branch_nsa, norm_ffn, router_w, router_b):
    offs = np.cumsum((0,) + IN_SPLITS)
    col = lambda j: w_in[:, offs[j]:offs[j + 1]]
    q_perm = col(0).reshape(D_MODEL, NSA_KV_HEADS, NSA_GROUP, HEAD_DIM).transpose(0, 2, 1, 3).reshape(D_MODEL, NSA_WIDTH)
    pad = jnp.zeros((D_MODEL, LG_PAD - 3 * NSA_HEADS - GLA_RANK), w_in.dtype)
    w_p = jnp.concatenate([q_perm, col(1), col(3), col(4), col(5), col(7), col(8), col(9), col(2), col(6), pad],
                          axis=1).astype(BF16)
    w_al = jnp.zeros((LG_PAD, GLA_K_WIDTH), F32).at[3 * NSA_HEADS:3 * NSA_HEADS + GLA_RANK].set(gla_w_alpha)
    wn = w_branch_nsa.reshape(NSA_KV_HEADS, NSA_GROUP, HEAD_DIM, D_MODEL).transpose(1, 0, 2, 3).reshape(NSA_WIDTH, D_MODEL)
    rw = jnp.concatenate([router_w, jnp.zeros((D_MODEL, LANES - N_EXPERTS), F32)], axis=1)
    rb = jnp.concatenate([router_b, jnp.full((LANES - N_EXPERTS,), NEG_INF, F32)]).reshape(1, LANES)
    return w_p, w_al, gla_b_alpha.reshape(1, GLA_K_WIDTH), wn.astype(BF16), norm_ffn.reshape(1, D_MODEL), rw, rb


def _moe(h2t, ei, gw, rk, counts, x1, g_final, ew):
    n = x1.shape[0]
    nk = n * TOP_K
    counts = counts[0, :N_EXPERTS].astype(I32)
    padded = (counts + MOE_ROWS - 1) // MOE_ROWS * MOE_ROWS
    pends = jnp.cumsum(padded)
    pstarts = pends - padded
    n_blocks = (nk + N_EXPERTS * (MOE_ROWS - 1) + MOE_ROWS - 1) // MOE_ROWS
    block_e = jnp.minimum(jnp.searchsorted(pends, jnp.arange(n_blocks, dtype=I32) * MOE_ROWS, side="right"),
                          N_EXPERTS - 1).astype(I32)
    n_used = (pends[-1] // MOE_ROWS).astype(I32).reshape(1)
    e_sel = ei[:, :TOP_K, None] == jnp.arange(N_EXPERTS, dtype=I32)
    dest = jnp.sum(jnp.where(e_sel, pstarts.astype(I32), 0), axis=-1) + rk[:, :TOP_K]
    dest2d = dest.reshape(nk // LANES, LANES)
    meta = jnp.concatenate([counts, pstarts, padded, n_used]).astype(I32)
    td = min(TD_DISPATCH, n)
    xs = _dispatch(meta, dest2d, h2t, n_blocks * MOE_ROWS, td)
    y_rows = _moe_experts(block_e, n_used, xs, ew, n_blocks)
    return _combine(dest2d, gw, x1, g_final, y_rows)


def kernel(x_prompt, x_sample, cache_cmp_k, cache_cmp_v, cache_sel_k, cache_sel_v, state_win_k, state_win_v, state_gla, page_table, rel_bias, norm_mix, w_in, nsa_pe_k, nsa_pe_v, nsa_w1_k, nsa_w1_v, nsa_w2_k, nsa_w2_v, gla_w_alpha, gla_b_alpha, gla_norm, w_branch_nsa, w_branch_gla, w_out, norm_ffn, router_w, router_b, exp_w_gate, exp_b_gate, exp_w_up, exp_b_up, exp_w_down, exp_b_down, norm_final):
    depth = w_in.shape[0]
    assert depth == 1, "single-layer trunk"
    bsz, t, d = x_prompt.shape
    n_seq, s_new, _ = x_sample.shape
    n_pages = page_table.shape[1]
    past = n_pages * PAGE_SIZE
    w_buf = state_win_k.shape[2]
    assert d == D_MODEL and t % Q_BLOCK == 0 and t % GLA_CHUNK == 0

    w_p, w_al, b_al, wn, gf, rw, rb = _layer_weights(w_in[0], gla_w_alpha[0], gla_b_alpha[0], w_branch_nsa[0],
                                                     norm_ffn[0], router_w[0], router_b[0])
    g_mix = norm_mix[0].reshape(1, D_MODEL)
    g_fin = norm_final.reshape(1, D_MODEL)
    g_gla = gla_norm[0].reshape(1, GLA_DV)
    mw = dict(wn=wn, wg=w_branch_gla[0].astype(BF16), wo=w_out[0].astype(BF16), gf=gf, rw=rw, rb=rb)
    ew = dict(wg=exp_w_gate[0].astype(BF16), bg=exp_b_gate[0].reshape(N_EXPERTS, 1, D_FF),
              wu=exp_w_up[0].astype(BF16), bu=exp_b_up[0].reshape(N_EXPERTS, 1, D_FF),
              wd=exp_w_down[0].astype(BF16), bd=exp_b_down[0].reshape(N_EXPERTS, 1, D_MODEL))
    cw = {}
    for nm, pe, w1, w2 in (("k", nsa_pe_k[0], nsa_w1_k[0], nsa_w2_k[0]), ("v", nsa_pe_v[0], nsa_w1_v[0], nsa_w2_v[0])):
        cw["wcat_" + nm], cw["pe_" + nm], cw["w1f_" + nm], cw["w2_" + nm] = _compress_weights(pe, w1, w2)

    xp = x_prompt.reshape(bsz * t, d)
    (q4, kck, kcv, ksk, ksv, kwk, kwv, gates, q_l, k_l, v_l, lg, r_l, m_a, m_b) = _in_projection(xp, g_mix, w_p, w_al, b_al)
    kc, vc = _compress_prompt(kck, kcv, cw, bsz, t)
    o_nsa = _nsa_prompt(q4, gates, kc, vc, ksk, ksv, kwk, kwv, _prompt_tables(rel_bias, t), bsz, t)
    s_zero = jnp.zeros((bsz * GLA_HEADS * GLA_DK, GLA_DV), F32)
    o_gla, p_gla = _gla(q_l, k_l, v_l, lg, r_l, g_gla, s_zero, bsz, t // GLA_CHUNK, GLA_CHUNK)
    x1, h2t, ei, gw, rk, counts = _merge(xp, o_nsa, o_gla, m_a, m_b, mw)
    y_prompt = _moe(h2t, ei, gw, rk, counts, x1, g_fin, ew).reshape(bsz, t, d)

    kv_shape = (1, bsz, t, NSA_KV_HEADS, HEAD_DIM)
    w_len = min(WINDOW, t)
    win = lambda a: a.reshape(kv_shape)[:, :, t - w_len:]
    p_states = (kck.reshape(kv_shape), kcv.reshape(kv_shape), ksk.reshape(kv_shape), ksv.reshape(kv_shape),
                win(kwk), win(kwv), p_gla.reshape(1, bsz, GLA_HEADS, GLA_DK, GLA_DV))

    xs = x_sample.reshape(n_seq * s_new, d)
    (q4, kck, kcv, ksk, ksv, kwk, kwv, gates, q_l, k_l, v_l, lg, r_l, m_a, m_b) = _in_projection(xs, g_mix, w_p, w_al, b_al)
    pt_flat = page_table.reshape(n_seq * n_pages).astype(I32)
    pool = lambda c: c[0].reshape(-1, KV_WIDTH)
    kc, vc = _compress_sample(pt_flat, pool(cache_cmp_k), pool(cache_cmp_v), cw, n_seq, n_pages)
    buf_kw = state_win_k[0].reshape(n_seq * w_buf, KV_WIDTH)
    buf_vw = state_win_v[0].reshape(n_seq * w_buf, KV_WIDTH)
    o_nsa = _nsa_sample(pt_flat, q4, gates, kc, vc, ksk, ksv, kwk, kwv, buf_kw, buf_vw,
                        pool(cache_sel_k), pool(cache_sel_v), _sample_tables(rel_bias, past, s_new, w_buf),
                        n_seq, s_new, n_pages)
    cl = 16
    padc = lambda a: jnp.pad(a.reshape(n_seq, s_new, -1), ((0, 0), (0, cl - s_new), (0, 0))).reshape(n_seq * cl, -1)
    s_in = state_gla[0].reshape(n_seq * GLA_HEADS * GLA_DK, GLA_DV)
    o_gla, s_gla = _gla(padc(q_l), padc(k_l), padc(v_l), padc(lg), padc(r_l), g_gla, s_in, n_seq, 1, cl)
    o_gla = o_gla.reshape(n_seq, cl, GLA_V_WIDTH)[:, :s_new].reshape(n_seq * s_new, GLA_V_WIDTH)
    x1, h2t, ei, gw, rk, counts = _merge(xs, o_nsa, o_gla, m_a, m_b, mw)
    y_sample = _moe(h2t, ei, gw, rk, counts, x1, g_fin, ew).reshape(n_seq, s_new, d)

    kvs = (1, n_seq, s_new, NSA_KV_HEADS, HEAD_DIM)
    new_win = lambda buf, new: jnp.concatenate([buf, new.reshape(kvs).astype(buf.dtype)], axis=2)[:, :, s_new:]
    s_states = (kck.reshape(kvs), kcv.reshape(kvs), ksk.reshape(kvs), ksv.reshape(kvs),
                new_win(state_win_k, kwk), new_win(state_win_v, kwv),
                s_gla.reshape(1, n_seq, GLA_HEADS, GLA_DK, GLA_DV))
    return (y_prompt, y_sample) + p_states + s_states
```

```python
import functools
import math

import numpy as np
import jax
import jax.numpy as jnp
from jax import lax
from jax.experimental import pallas as pl
from jax.experimental.pallas import tpu as pltpu

F32 = jnp.float32
BF16 = jnp.bfloat16
I32 = jnp.int32
HI = lax.Precision.HIGHEST

D_MODEL = 1024
PAGE_SIZE = 128
NSA_HEADS = 8
NSA_KV_HEADS = 2
NSA_GROUP = NSA_HEADS // NSA_KV_HEADS
HEAD_DIM = 64
NSA_WIDTH = NSA_HEADS * HEAD_DIM
KV_WIDTH = NSA_KV_HEADS * HEAD_DIM
CMP_BLOCK = 32
CMP_STRIDE = 16
CMP_HIDDEN = 2 * HEAD_DIM
SEL_BLOCK = 64
N_SEL = 16
WINDOW = 512
Q_BLOCK = 128
SEL_FORCE = 1e9
GLA_HEADS = 4
GLA_DK = 64
GLA_DV = 128
GLA_K_WIDTH = GLA_HEADS * GLA_DK
GLA_V_WIDTH = GLA_HEADS * GLA_DV
GLA_RANK = 16
GLA_TAU = 16.0
GLA_CHUNK = 64
N_EXPERTS = 32
TOP_K = 4
D_FF = D_MODEL
SWIGLU_ALPHA = 1.702
SWIGLU_LIMIT = 7.0
REL_BUCKETS = 32
REL_EXACT = REL_BUCKETS // 2
REL_MAX_DIST = 1024
RMS_EPS = 1e-6
NEG_INF = -1e30
IN_SPLITS = (NSA_WIDTH, 6 * KV_WIDTH, 3 * NSA_HEADS, GLA_K_WIDTH, GLA_K_WIDTH, GLA_V_WIDTH, GLA_RANK,
             GLA_V_WIDTH, D_MODEL, D_MODEL)

LANES = 128
SUBLANES = 8
VMEM_LIMIT = 56 * 1024 * 1024

TM_PROJ = 256
MOE_ROWS = 256
TD_DISPATCH = 512
TC_COMBINE = 256
SEL_CHUNK = 512
LG_PAD = LANES


def _nt(a, b, **kw):
    return lax.dot_general(a, b, (((1,), (1,)), ((), ())), preferred_element_type=F32, **kw)


def _tn(a, b, **kw):
    return lax.dot_general(a, b, (((0,), (0,)), ((), ())), preferred_element_type=F32, **kw)


def _dot(a, b, **kw):
    return jnp.dot(a, b, preferred_element_type=F32, **kw)


def _params(sem, vmem=VMEM_LIMIT):
    return pltpu.CompilerParams(dimension_semantics=sem, vmem_limit_bytes=vmem)


def _masked_softmax_parts(s, valid):
    s = jnp.where(valid, s, NEG_INF)
    m = jnp.max(s, axis=-1, keepdims=True)
    e = jnp.where(valid, jnp.exp(s - m), 0.0)
    return e, jnp.maximum(jnp.sum(e, axis=-1, keepdims=True), 1e-20)


_OFF_Q = 0
_OFF_KV = _OFF_Q + NSA_WIDTH
_OFF_QL = _OFF_KV + 6 * KV_WIDTH
_OFF_KL = _OFF_QL + GLA_K_WIDTH
_OFF_VL = _OFF_KL + GLA_K_WIDTH
_OFF_R = _OFF_VL + GLA_V_WIDTH
_OFF_MA = _OFF_R + GLA_V_WIDTH
_OFF_MB = _OFF_MA + D_MODEL
_OFF_GA = _OFF_MB + D_MODEL
_N_PROJ = _OFF_GA + LG_PAD


def _inproj_kernel(x_ref, g_ref, w_ref, wal_ref, bal_ref,
                   q_o, kck_o, kcv_o, ksk_o, ksv_o, kwk_o, kwv_o, gt_o, ql_o, kl_o, vl_o, lg_o, r_o, ma_o, mb_o):
    x = x_ref[...]
    xn = x * lax.rsqrt(jnp.mean(x * x, axis=-1, keepdims=True) + RMS_EPS)
    xn = (xn * g_ref[...]).astype(BF16)

    def mm(lo, n):
        return _dot(xn, w_ref[:, lo:lo + n])

    q = mm(_OFF_Q, NSA_WIDTH) * (HEAD_DIM ** -0.5)
    for r in range(NSA_GROUP):
        q_o[r] = q[:, r * LANES:(r + 1) * LANES]
    for j, o in enumerate((kck_o, kcv_o, ksk_o, ksv_o, kwk_o, kwv_o)):
        o[...] = mm(_OFF_KV + j * KV_WIDTH, KV_WIDTH)
    ql_o[...] = mm(_OFF_QL, GLA_K_WIDTH) * (GLA_DK ** -0.5)
    kl_o[...] = mm(_OFF_KL, GLA_K_WIDTH)
    vl_o[...] = mm(_OFF_VL, GLA_V_WIDTH)
    r_o[...] = mm(_OFF_R, GLA_V_WIDTH)
    ma_o[...] = mm(_OFF_MA, D_MODEL)
    mb_o[...] = mm(_OFF_MB, D_MODEL)
    ga = mm(_OFF_GA, LG_PAD)
    gt_o[...] = jax.nn.sigmoid(ga)
    al = _dot(ga, wal_ref[...], precision=HI) + bal_ref[...]
    lg_o[...] = (jnp.minimum(al, 0.0) - jnp.log1p(jnp.exp(-jnp.abs(al)))) * (1.0 / GLA_TAU)


def _in_projection(x, norm_g, w_p, w_al, b_al):
    n = x.shape[0]
    tm = TM_PROJ
    assert n % tm == 0
    row = lambda w: pl.BlockSpec((tm, w), lambda i: (i, 0))
    full = lambda a: pl.BlockSpec(a.shape, lambda i: (0,) * a.ndim)
    widths = (KV_WIDTH,) * 6 + (LG_PAD, GLA_K_WIDTH, GLA_K_WIDTH, GLA_V_WIDTH, GLA_K_WIDTH, GLA_V_WIDTH,
                                 D_MODEL, D_MODEL)
    out_shape = [jax.ShapeDtypeStruct((NSA_GROUP, n, LANES), F32)] + [jax.ShapeDtypeStruct((n, w), F32) for w in widths]
    out_specs = [pl.BlockSpec((NSA_GROUP, tm, LANES), lambda i: (0, i, 0))] + [row(w) for w in widths]
    return pl.pallas_call(
        _inproj_kernel,
        grid=(n // tm,),
        in_specs=[row(D_MODEL), full(norm_g), full(w_p), full(w_al), full(b_al)],
        out_specs=out_specs,
        out_shape=out_shape,
        compiler_params=_params(("arbitrary",)),
        name="in_projection",
    )(x, norm_g, w_p, w_al, b_al)


def _gelu_tanh(x):
    return 0.5 * x * (1.0 + jnp.tanh(math.sqrt(2.0 / math.pi) * (x + 0.044715 * (x * x * x))))


def _compress_rows(src, n_ch, wcat_ref, pe_ref, w1f_ref, w2_ref):
    hid2 = NSA_KV_HEADS * CMP_HIDDEN
    acc = jnp.zeros((n_ch, 2 * hid2), F32)
    for l in range(CMP_STRIDE):
        xl = src[pl.ds(l, n_ch, stride=CMP_STRIDE), :].astype(BF16)
        acc = acc + _dot(xl, wcat_ref[l])
    bias = _dot(pe_ref[...], w1f_ref[...], precision=HI)[0:1]
    bias2 = jnp.concatenate([bias] * NSA_KV_HEADS, axis=1)
    nxt = pltpu.roll(acc[:, hid2:], n_ch - 1, 0)
    h = acc[:, :hid2] + nxt + bias2
    return _dot(_gelu_tanh(h).astype(BF16), w2_ref[...])


def _compress_prompt_kernel(k_ref, v_ref, wk_ref, wv_ref, pek_ref, pev_ref, w1k_ref, w1v_ref, w2k_ref, w2v_ref,
                            kc_o, vc_o):
    n_ch = kc_o.shape[0]
    kc_o[...] = _compress_rows(k_ref, n_ch, wk_ref, pek_ref, w1k_ref, w2k_ref)
    vc_o[...] = _compress_rows(v_ref, n_ch, wv_ref, pev_ref, w1v_ref, w2v_ref)


def _compress_prompt(k_cmp, v_cmp, cw, bsz, t):
    n_ch = t // CMP_STRIDE
    full = lambda a: pl.BlockSpec(a.shape, lambda b: (0,) * a.ndim)
    seq = pl.BlockSpec((t, KV_WIDTH), lambda b: (b, 0))
    out = pl.BlockSpec((n_ch, KV_WIDTH), lambda b: (b, 0))
    ws = (cw["wcat_k"], cw["wcat_v"], cw["pe_k"], cw["pe_v"], cw["w1f_k"], cw["w1f_v"], cw["w2_k"], cw["w2_v"])
    return pl.pallas_call(
        _compress_prompt_kernel,
        grid=(bsz,),
        in_specs=[seq, seq] + [full(w) for w in ws],
        out_specs=[out, out],
        out_shape=[jax.ShapeDtypeStruct((bsz * n_ch, KV_WIDTH), F32)] * 2,
        compiler_params=_params(("arbitrary",)),
        name="nsa_compress_prompt",
    )(k_cmp, v_cmp, *ws)


def _paged_fetch(pt_ref, pools, bufs, sems, seq, slot, n_pages, pages_on_lanes):
    def body(p, _):
        pg = pt_ref[seq * n_pages + p]
        off = pl.multiple_of(p * PAGE_SIZE, PAGE_SIZE)
        for j, (pool, buf) in enumerate(zip(pools, bufs)):
            dst = buf.at[slot, :, pl.ds(off, PAGE_SIZE)] if pages_on_lanes else buf.at[slot, pl.ds(off, PAGE_SIZE)]
            pltpu.make_async_copy(pool.at[pl.ds(pg * KV_WIDTH, KV_WIDTH)], dst, sems.at[j, slot]).start()
        return 0
    lax.fori_loop(0, n_pages, body, 0)


def _paged_wait(bufs, sems, slot):
    for j, buf in enumerate(bufs):
        pltpu.make_async_copy(buf.at[slot], buf.at[slot], sems.at[j, slot]).wait()


def _compress_sample_kernel(pt_ref, pk_hbm, pv_hbm, wk_ref, wv_ref, pek_ref, pev_ref, w1k_ref, w1v_ref,
                            w2k_ref, w2v_ref, kc_o, vc_o, bufk, bufv, sems, rows_k, rows_v, *, n_pages):
    s = pl.program_id(0)
    n_seq = pl.num_programs(0)
    slot = s % 2
    pools, bufs = (pk_hbm, pv_hbm), (bufk, bufv)

    @pl.when(s == 0)
    def _():
        _paged_fetch(pt_ref, pools, bufs, sems, s, slot, n_pages, False)

    @pl.when(s + 1 < n_seq)
    def _():
        _paged_fetch(pt_ref, pools, bufs, sems, s + 1, 1 - slot, n_pages, False)

    _paged_wait(bufs, sems, slot)
    for p in range(n_pages):
        rows = slice(p * PAGE_SIZE, (p + 1) * PAGE_SIZE)
        rows_k[rows, :] = bufk[slot, rows, :].T
        rows_v[rows, :] = bufv[slot, rows, :].T
    n_ch = kc_o.shape[0]
    kc_o[...] = _compress_rows(rows_k, n_ch, wk_ref, pek_ref, w1k_ref, w2k_ref)
    vc_o[...] = _compress_rows(rows_v, n_ch, wv_ref, pev_ref, w1v_ref, w2v_ref)


def _compress_sample(pt_flat, pool_k, pool_v, cw, n_seq, n_pages):
    past = n_pages * PAGE_SIZE
    n_ch = past // CMP_STRIDE
    full = lambda a: pl.BlockSpec(a.shape, lambda s, pt: (0,) * a.ndim)
    hbm = pl.BlockSpec(memory_space=pl.ANY)
    out = pl.BlockSpec((n_ch, KV_WIDTH), lambda s, pt: (s, 0))
    ws = (cw["wcat_k"], cw["wcat_v"], cw["pe_k"], cw["pe_v"], cw["w1f_k"], cw["w1f_v"], cw["w2_k"], cw["w2_v"])
    return pl.pallas_call(
        functools.partial(_compress_sample_kernel, n_pages=n_pages),
        grid_spec=pltpu.PrefetchScalarGridSpec(
            num_scalar_prefetch=1,
            grid=(n_seq,),
            in_specs=[hbm, hbm] + [full(w) for w in ws],
            out_specs=[out, out],
            scratch_shapes=[pltpu.VMEM((2, past, KV_WIDTH), F32), pltpu.VMEM((2, past, KV_WIDTH), F32),
                            pltpu.SemaphoreType.DMA((2, 2)),
                            pltpu.VMEM((past, KV_WIDTH), F32), pltpu.VMEM((past, KV_WIDTH), F32)],
        ),
        out_shape=[jax.ShapeDtypeStruct((n_seq * n_ch, KV_WIDTH), F32)] * 2,
        compiler_params=_params(("arbitrary",)),
        name="nsa_compress_sample",
    )(pt_flat, pool_k, pool_v, *ws)


def _select_blocks(score, n_sel):
    blk = lax.broadcasted_iota(I32, score.shape, 1)
    cnt = jnp.zeros(score.shape, F32)
    for i in range(n_sel):
        col = score[:, i:i + 1]
        ahead = (col > score) | ((col == score) & (i < blk))
        cnt = cnt + jnp.where(ahead, 1.0, 0.0)
    n_top = min(N_SEL, n_sel)
    return jnp.where((cnt < n_top) & (blk < n_sel), 1.0, 0.0)


def _block_scores(imp, qpos, n_sel):
    blk = lax.broadcasted_iota(I32, imp.shape, 1)
    cur = qpos // SEL_BLOCK
    forced = (blk == 0) | (blk == cur) | (blk == cur - 1)
    causal = (blk * SEL_BLOCK) <= qpos
    score = jnp.where(causal, jnp.where(forced, SEL_FORCE, imp), -SEL_FORCE)
    return jnp.where(blk < n_sel, score, -3e38)


def _gate_mix(gexp, o_c, o_s, o_w, r):
    out = None
    for c, o in enumerate((o_c, o_s, o_w)):
        term = gexp[:, c * NSA_WIDTH + r * LANES:c * NSA_WIDTH + (r + 1) * LANES] * o
        out = term if out is None else out + term
    return out


def _nsa_prompt_kernel(q_ref, gt_ref, kc_ref, vc_ref, ks_ref, vs_ref, kw_ref, vw_ref,
                       gr_ref, bc_ref, mimp_ref, esel_ref, eg_ref, o_ref,
                       ksb, vsb, kwb, vwb, kcb, vcb, oc_s, os_s, ow_s, msk_s, *, voff_blk):
    qb = pl.program_id(1)
    t = ks_ref.shape[0]
    n_ch = kc_ref.shape[0]
    n_sel = t // SEL_BLOCK
    tk = msk_s.shape[3]
    n_kc = t // tk
    tiles = tk // LANES

    @pl.when(qb == 0)
    def _():
        ksb[...] = ks_ref[...].astype(BF16)
        vsb[...] = vs_ref[...].astype(BF16)
        kwb[0:WINDOW, :] = jnp.zeros((WINDOW, KV_WIDTH), BF16)
        vwb[0:WINDOW, :] = jnp.zeros((WINDOW, KV_WIDTH), BF16)
        kwb[WINDOW:, :] = kw_ref[...].astype(BF16)
        vwb[WINDOW:, :] = vw_ref[...].astype(BF16)
        kcb[...] = kc_ref[...].astype(BF16)
        vcb[...] = vc_ref[...].astype(BF16)

    q0 = qb * Q_BLOCK
    lane = lax.broadcasted_iota(I32, (Q_BLOCK, LANES), 1)
    qpos = q0 + lax.broadcasted_iota(I32, (Q_BLOCK, 1), 0)
    upper = lane >= HEAD_DIM

    def masked_q(g, r):
        keep = upper if g == 1 else jnp.logical_not(upper)
        return jnp.where(keep, q_ref[r], 0.0).astype(BF16)

    cend = lax.broadcasted_iota(I32, (Q_BLOCK, n_ch), 1) * CMP_STRIDE + (CMP_BLOCK - 1)
    valid_c = qpos >= cend
    for g in range(NSA_KV_HEADS):
        psum = jnp.zeros((Q_BLOCK, n_ch), F32)
        for r in range(NSA_GROUP):
            h = g * NSA_GROUP + r
            s = _nt(masked_q(g, r), kcb[...]) + bc_ref[0, h]
            e, den = _masked_softmax_parts(s, valid_c)
            p = e / den
            psum = psum + p
            oc_s[h] = _dot(p.astype(BF16), vcb[...])
        imp = _dot(psum, mimp_ref[...], precision=HI)
        sel = _select_blocks(_block_scores(imp, qpos, n_sel), n_sel).astype(BF16)
        for c in range(n_kc):
            msk_s[g, c] = _dot(sel, esel_ref[:, c * tk:(c + 1) * tk])

    n_live = (q0 + Q_BLOCK + tk - 1) // tk
    wpos = q0 - WINDOW + lax.broadcasted_iota(I32, (Q_BLOCK, WINDOW + Q_BLOCK), 1)
    dist_w = qpos - wpos
    valid_w = (dist_w >= 0) & (dist_w < WINDOW) & (wpos >= 0)
    w_blk = voff_blk - WINDOW // LANES
    for g in range(NSA_KV_HEADS):
        for r in range(NSA_GROUP):
            h = g * NSA_GROUP + r
            qm = masked_q(g, r)

            def chunk(kt, carry, qm=qm, h=h, g=g):
                m, l, acc = carry
                k0 = pl.multiple_of(kt * tk, tk)
                s = _nt(qm, ksb[pl.ds(k0, tk), :])
                base = voff_blk - qb + kt * tiles
                s = s + jnp.concatenate([gr_ref[base + j, h] for j in range(tiles)], axis=1)
                kpos = k0 + lax.broadcasted_iota(I32, (Q_BLOCK, tk), 1)
                valid = (msk_s[g, kt] > 0.5) & (kpos <= qpos)
                s = jnp.where(valid, s, NEG_INF)
                m_new = jnp.maximum(m, jnp.max(s, axis=-1, keepdims=True))
                alpha = jnp.exp(m - m_new)
                e = jnp.where(valid, jnp.exp(s - m_new), 0.0)
                l = alpha * l + jnp.sum(e, axis=-1, keepdims=True)
                acc = alpha * acc + _dot(e.astype(BF16), vsb[pl.ds(k0, tk), :])
                return m_new, l, acc

            init = (jnp.full((Q_BLOCK, 1), NEG_INF, F32), jnp.zeros((Q_BLOCK, 1), F32),
                    jnp.zeros((Q_BLOCK, KV_WIDTH), F32))
            _, l, acc = lax.fori_loop(0, n_live, chunk, init)
            os_s[h] = acc / jnp.maximum(l, 1e-20)

            s = _nt(qm, kwb[pl.ds(pl.multiple_of(q0, Q_BLOCK), WINDOW + Q_BLOCK), :])
            s = s + jnp.concatenate([gr_ref[w_blk + j, h] for j in range((WINDOW + Q_BLOCK) // LANES)], axis=1)
            e, den = _masked_softmax_parts(s, valid_w)
            ow_s[h] = _dot(e.astype(BF16), vwb[pl.ds(pl.multiple_of(q0, Q_BLOCK), WINDOW + Q_BLOCK), :]) / den

    gexp = _dot(gt_ref[...], eg_ref[...], precision=HI)
    for r in range(NSA_GROUP):
        pick = lambda ref: jnp.where(upper, ref[NSA_GROUP + r], ref[r])
        o_ref[r] = _gate_mix(gexp, pick(oc_s), pick(os_s), pick(ow_s), r)


def _nsa_prompt(q4, gates, kc, vc, ks, vs, kw, vw, tabs, bsz, t):
    n_qb = t // Q_BLOCK
    n_ch = t // CMP_STRIDE
    tk = min(SEL_CHUNK, t)
    full = lambda a: pl.BlockSpec(a.shape, lambda b, i: (0,) * a.ndim)
    seq = pl.BlockSpec((t, KV_WIDTH), lambda b, i: (b, 0))
    cseq = pl.BlockSpec((n_ch, KV_WIDTH), lambda b, i: (b, 0))
    qspec = pl.BlockSpec((NSA_GROUP, Q_BLOCK, LANES), lambda b, i: (0, b * n_qb + i, 0))
    gr, bc, mimp, esel, eg = tabs["gr"], tabs["bc"], tabs["mimp"], tabs["esel"], tabs["eg"]
    head_tile = pltpu.VMEM((NSA_HEADS, Q_BLOCK, KV_WIDTH), F32)
    return pl.pallas_call(
        functools.partial(_nsa_prompt_kernel, voff_blk=tabs["voff"] // LANES),
        grid=(bsz, n_qb),
        in_specs=[qspec, pl.BlockSpec((Q_BLOCK, LG_PAD), lambda b, i: (b * n_qb + i, 0)),
                  cseq, cseq, seq, seq, seq, seq, full(gr),
                  pl.BlockSpec((1,) + bc.shape[1:], lambda b, i: (i, 0, 0, 0)), full(mimp), full(esel), full(eg)],
        out_specs=qspec,
        out_shape=jax.ShapeDtypeStruct((NSA_GROUP, bsz * t, LANES), F32),
        scratch_shapes=[pltpu.VMEM((t, KV_WIDTH), BF16), pltpu.VMEM((t, KV_WIDTH), BF16),
                        pltpu.VMEM((t + WINDOW, KV_WIDTH), BF16), pltpu.VMEM((t + WINDOW, KV_WIDTH), BF16),
                        pltpu.VMEM((n_ch, KV_WIDTH), BF16), pltpu.VMEM((n_ch, KV_WIDTH), BF16),
                        head_tile, head_tile, head_tile,
                        pltpu.VMEM((NSA_KV_HEADS, t // tk, Q_BLOCK, tk), F32)],
        compiler_params=_params(("arbitrary", "arbitrary")),
        name="nsa_attention_prompt",
    )(q4, gates, kc, vc, ks, vs, kw, vw, gr, bc, mimp, esel, eg)


def _nsa_sample_kernel(pt_ref, q_ref, gt_ref, kc_ref, vc_ref, ksn_ref, vsn_ref, kwn_ref, vwn_ref, bkw_ref, bvw_ref,
                       pks_hbm, pvs_hbm, gs_ref, gn_ref, bcs_ref, bws_ref, mimp_ref, esel_ref, eg_ref, o_ref,
                       bufk, bufv, sems, ksb, vsb, *, n_pages):
    sq = pl.program_id(0)
    n_seq = pl.num_programs(0)
    slot = sq % 2
    pools, bufs = (pks_hbm, pvs_hbm), (bufk, bufv)
    past = n_pages * PAGE_SIZE
    s_new = ksn_ref.shape[0]
    rows = NSA_GROUP * s_new
    n_cmp_rows = kc_ref.shape[0]
    n_sel = (past + s_new + SEL_BLOCK - 1) // SEL_BLOCK
    n_past_blk = past // SEL_BLOCK
    w_buf = bkw_ref.shape[1]

    @pl.when(sq == 0)
    def _():
        _paged_fetch(pt_ref, pools, bufs, sems, sq, slot, n_pages, True)

    @pl.when(sq + 1 < n_seq)
    def _():
        _paged_fetch(pt_ref, pools, bufs, sems, sq + 1, 1 - slot, n_pages, True)

    _paged_wait(bufs, sems, slot)
    ksb[...] = bufk[slot].astype(BF16)
    vsb[...] = bufv[slot].astype(BF16)

    lane = lax.broadcasted_iota(I32, (rows, LANES), 1)
    upper = lane >= HEAD_DIM
    qi = lax.broadcasted_iota(I32, (rows, 1), 0) % s_new
    pad_new = lambda ref: jnp.concatenate([ref[...], jnp.zeros((LANES - s_new, KV_WIDTH), F32)], axis=0).astype(BF16)
    ksn, vsn, kwn, vwn = pad_new(ksn_ref), pad_new(vsn_ref), pad_new(kwn_ref), pad_new(vwn_ref)
    kcb, vcb = kc_ref[...].astype(BF16), vc_ref[...].astype(BF16)
    bkw, bvw = bkw_ref[...].astype(BF16), bvw_ref[...].astype(BF16)
    new_causal = (lane < s_new) & (lane <= qi)

    def stacked_q(g):
        keep = upper if g == 1 else jnp.logical_not(upper)
        qs = jnp.concatenate([q_ref[r] for r in range(NSA_GROUP)], axis=0)
        return jnp.where(keep, qs, 0.0).astype(BF16)

    cend = lax.broadcasted_iota(I32, (rows, n_cmp_rows), 1) * CMP_STRIDE + (CMP_BLOCK - 1)
    valid_c = (past + qi) >= cend
    o_c, psums = [], []
    for g in range(NSA_KV_HEADS):
        s = _nt(stacked_q(g), kcb) + bcs_ref[g]
        e, den = _masked_softmax_parts(s, valid_c)
        p = e / den
        o_c.append(_dot(p.astype(BF16), vcb))
        ps = p[0:s_new]
        for r in range(1, NSA_GROUP):
            ps = ps + p[r * s_new:(r + 1) * s_new]
        psums.append(ps)
    imp = _dot(jnp.concatenate(psums, axis=0), mimp_ref[...], precision=HI)
    qpos_sel = past + lax.broadcasted_iota(I32, (NSA_KV_HEADS * s_new, 1), 0) % s_new
    sel = _select_blocks(_block_scores(imp, qpos_sel, n_sel), n_sel)
    mask_past = _dot(sel[:, :n_past_blk].astype(BF16), esel_ref[...])
    sel_new = sel[:, n_past_blk:n_past_blk + 1]

    gexp = _dot(gt_ref[...], eg_ref[...], precision=HI)
    o_s, o_w = [], []
    for g in range(NSA_KV_HEADS):
        qg = stacked_q(g)
        tile_rows = lambda a: jnp.concatenate([a[g * s_new:(g + 1) * s_new]] * NSA_GROUP, axis=0)
        valid_p = tile_rows(mask_past) > 0.5
        s_p = jnp.where(valid_p, _dot(qg, ksb[...]) + gs_ref[g], NEG_INF)
        valid_n = new_causal & (tile_rows(sel_new) > 0.5)
        s_n = jnp.where(valid_n, _nt(qg, ksn) + gn_ref[g], NEG_INF)
        m = jnp.maximum(jnp.max(s_p, axis=-1, keepdims=True), jnp.max(s_n, axis=-1, keepdims=True))
        e_p = jnp.where(valid_p, jnp.exp(s_p - m), 0.0)
        e_n = jnp.where(valid_n, jnp.exp(s_n - m), 0.0)
        den = jnp.maximum(jnp.sum(e_p, axis=-1, keepdims=True) + jnp.sum(e_n, axis=-1, keepdims=True), 1e-20)
        o_s.append((_nt(e_p.astype(BF16), vsb[...]) + _dot(e_n.astype(BF16), vsn)) / den)
        jb = lax.broadcasted_iota(I32, (rows, w_buf), 1)
        dist_b = w_buf + qi - jb
        valid_b = (dist_b < WINDOW) & (past - w_buf + jb >= 0)
        s_b = jnp.where(valid_b, _dot(qg, bkw) + bws_ref[g], NEG_INF)
        s_n = jnp.where(new_causal, _nt(qg, kwn) + gn_ref[g], NEG_INF)
        m = jnp.maximum(jnp.max(s_b, axis=-1, keepdims=True), jnp.max(s_n, axis=-1, keepdims=True))
        e_b = jnp.where(valid_b, jnp.exp(s_b - m), 0.0)
        e_n = jnp.where(new_causal, jnp.exp(s_n - m), 0.0)
        den = jnp.maximum(jnp.sum(e_b, axis=-1, keepdims=True) + jnp.sum(e_n, axis=-1, keepdims=True), 1e-20)
        o_w.append((_nt(e_b.astype(BF16), bvw) + _dot(e_n.astype(BF16), vwn)) / den)

    up8 = upper[0:s_new]
    for r in range(NSA_GROUP):
        pick = lambda o: jnp.where(up8, o[1][r * s_new:(r + 1) * s_new], o[0][r * s_new:(r + 1) * s_new])
        o_ref[r] = _gate_mix(gexp, pick(o_c), pick(o_s), pick(o_w), r)


def _nsa_sample(pt_flat, q4, gates, kc, vc, ksn, vsn, kwn, vwn, buf_kw, buf_vw, pool_ks, pool_vs, tabs,
                n_seq, s_new, n_pages):
    past = n_pages * PAGE_SIZE
    n_ch = past // CMP_STRIDE
    w_buf = buf_kw.shape[1]
    full = lambda a: pl.BlockSpec(a.shape, lambda s, pt: (0,) * a.ndim)
    hbm = pl.BlockSpec(memory_space=pl.ANY)
    rows = lambda n, w: pl.BlockSpec((n, w), lambda s, pt: (s, 0))
    qspec = pl.BlockSpec((NSA_GROUP, s_new, LANES), lambda s, pt: (0, s, 0))
    consts = (tabs["gs"], tabs["gn"], tabs["bcs"], tabs["bws"], tabs["mimp_s"], tabs["esel_s"], tabs["eg"])
    return pl.pallas_call(
        functools.partial(_nsa_sample_kernel, n_pages=n_pages),
        grid_spec=pltpu.PrefetchScalarGridSpec(
            num_scalar_prefetch=1,
            grid=(n_seq,),
            in_specs=[qspec, rows(s_new, LG_PAD), rows(n_ch, KV_WIDTH), rows(n_ch, KV_WIDTH)]
                     + [rows(s_new, KV_WIDTH)] * 4 + [rows(KV_WIDTH, w_buf)] * 2 + [hbm, hbm]
                     + [full(c) for c in consts],
            out_specs=qspec,
            scratch_shapes=[pltpu.VMEM((2, KV_WIDTH, past), F32), pltpu.VMEM((2, KV_WIDTH, past), F32),
                            pltpu.SemaphoreType.DMA((2, 2)),
                            pltpu.VMEM((KV_WIDTH, past), BF16), pltpu.VMEM((KV_WIDTH, past), BF16)],
        ),
        out_shape=jax.ShapeDtypeStruct((NSA_GROUP, n_seq * s_new, LANES), F32),
        compiler_params=_params(("arbitrary",)),
        name="nsa_attention_sample",
    )(pt_flat, q4, gates, kc, vc, ksn, vsn, kwn, vwn, buf_kw, buf_vw, pool_ks, pool_vs, *consts)


def _gla_kernel(q_ref, k_ref, v_ref, lg_ref, r_ref, gn_ref, s0_ref, o_ref, sfin_ref, s_scr):
    c = pl.program_id(1)
    n_c = pl.num_programs(1)
    cl = q_ref.shape[0]

    @pl.when(c == 0)
    def _():
        s_scr[...] = s0_ref[...]

    row_t = lax.broadcasted_iota(I32, (cl, LANES), 0)
    lane_t = lax.broadcasted_iota(I32, (cl, LANES), 1)
    causal = lax.broadcasted_iota(I32, (cl, cl), 1) <= lax.broadcasted_iota(I32, (cl, cl), 0)
    row_s = lax.broadcasted_iota(I32, (LANES, GLA_DV), 0)
    pair = LANES // GLA_DK
    for p in range(GLA_HEADS // pair):
        cols = slice(p * LANES, (p + 1) * LANES)
        lg = lg_ref[:, cols]
        b = lg
        sh = 1
        while sh < cl:
            b = b + jnp.where(row_t >= sh, pltpu.roll(b, sh, 0), 0.0)
            sh *= 2
        b_last = b[cl - 1:cl, :]
        qt = q_ref[:, cols] * jnp.exp(b)
        kp = k_ref[:, cols]
        kt = (kp * jnp.exp(-b)).astype(BF16)
        khat = (kp * jnp.exp(b_last - b)).astype(BF16)
        dec = jnp.exp(jnp.broadcast_to(b_last, (LANES, LANES))).T
        s_old = s_scr[cols, :]
        s_bf = s_old.astype(BF16)
        upd = jnp.zeros((LANES, GLA_DV), F32)
        for hh in range(pair):
            h = p * pair + hh
            vcols = slice(h * GLA_DV, (h + 1) * GLA_DV)
            mine = (lane_t >= GLA_DK) if hh == 1 else (lane_t < GLA_DK)
            qm = jnp.where(mine, qt, 0.0).astype(BF16)
            att = jnp.where(causal, _nt(qm, kt), 0.0)
            vh = v_ref[:, vcols].astype(BF16)
            o = _dot(qm, s_bf) + _dot(att.astype(BF16), vh)
            o = o * lax.rsqrt(jnp.mean(o * o, axis=-1, keepdims=True) + RMS_EPS) * gn_ref[...]
            rh = r_ref[:, vcols]
            o_ref[:, vcols] = o * (rh * jax.nn.sigmoid(rh))
            u = _tn(khat, vh)
            upd = jnp.where((row_s >= GLA_DK) == (hh == 1), u, upd)
        s_scr[cols, :] = s_old * dec + upd

    @pl.when(c == n_c - 1)
    def _():
        sfin_ref[...] = s_scr[...]


def _gla(q_l, k_l, v_l, lg, r, g_norm, s0, bsz, n_c, cl):
    srows = GLA_HEADS * GLA_DK
    blk = lambda w: pl.BlockSpec((cl, w), lambda b, c: (b * n_c + c, 0))
    st = pl.BlockSpec((srows, GLA_DV), lambda b, c: (b, 0))
    return pl.pallas_call(
        _gla_kernel,
        grid=(bsz, n_c),
        in_specs=[blk(GLA_K_WIDTH), blk(GLA_K_WIDTH), blk(GLA_V_WIDTH), blk(GLA_K_WIDTH), blk(GLA_V_WIDTH),
                  pl.BlockSpec(g_norm.shape, lambda b, c: (0, 0)), st],
        out_specs=[blk(GLA_V_WIDTH), st],
        out_shape=[jax.ShapeDtypeStruct((bsz * n_c * cl, GLA_V_WIDTH), F32),
                   jax.ShapeDtypeStruct((bsz * srows, GLA_DV), F32)],
        scratch_shapes=[pltpu.VMEM((srows, GLA_DV), F32)],
        compiler_params=_params(("arbitrary", "arbitrary")),
        name="gla",
    )(q_l, k_l, v_l, lg, r, g_norm, s0)


def _merge_kernel(x_ref, on_ref, og_ref, ma_ref, mb_ref, wn_ref, wg_ref, wo_ref, gf_ref, rw_ref, rb_ref,
                  x1_o, h2t_o, ei_o, gw_o, rk_o, cnt_o, carry):
    i = pl.program_id(0)
    tm = x_ref.shape[0]

    @pl.when(i == 0)
    def _():
        carry[...] = jnp.zeros(carry.shape, F32)

    on = jnp.concatenate([on_ref[r] for r in range(NSA_GROUP)], axis=1).astype(BF16)
    ya = _dot(on, wn_ref[...])
    yb = _dot(og_ref[...].astype(BF16), wg_ref[...])
    m = jax.nn.sigmoid(ma_ref[...]) * ya + jax.nn.sigmoid(mb_ref[...]) * yb
    x1 = x_ref[...] + _dot(m.astype(BF16), wo_ref[...])
    x1_o[...] = x1
    h2 = x1 * lax.rsqrt(jnp.mean(x1 * x1, axis=-1, keepdims=True) + RMS_EPS) * gf_ref[...]
    for s in range(D_MODEL // LANES):
        h2t_o[pl.ds(s, tm, stride=SUBLANES), :] = h2[:, s * LANES:(s + 1) * LANES]

    logits = _dot(h2, rw_ref[...], precision=HI) + rb_ref[...]
    lane = lax.broadcasted_iota(I32, (tm, LANES), 1)
    lane_f = lane.astype(F32)
    work = logits
    vals, idxs = [], []
    for _ in range(TOP_K):
        mk = jnp.max(work, axis=-1, keepdims=True)
        ik = jnp.min(jnp.where(work == mk, lane_f, float(LANES)), axis=-1, keepdims=True)
        vals.append(mk)
        idxs.append(ik)
        work = jnp.where(lane_f == ik, -jnp.inf, work)
    es = [jnp.exp(v - vals[0]) for v in vals]
    den = es[0]
    for e in es[1:]:
        den = den + e
    onehot = jnp.zeros((tm, LANES), F32)
    for ik in idxs:
        onehot = onehot + jnp.where(lane_f == ik, 1.0, 0.0)
    below = lax.broadcasted_iota(I32, (tm, tm), 1) < lax.broadcasted_iota(I32, (tm, tm), 0)
    before = _dot(jnp.where(below, 1.0, 0.0).astype(BF16), onehot.astype(BF16)) + carry[0:1, :]
    ei = jnp.zeros((tm, LANES), F32)
    gw = jnp.zeros((tm, LANES), F32)
    rk = jnp.zeros((tm, LANES), F32)
    for k in range(TOP_K):
        rank_k = jnp.sum(jnp.where(lane_f == idxs[k], before, 0.0), axis=-1, keepdims=True)
        ei = jnp.where(lane == k, idxs[k], ei)
        gw = jnp.where(lane == k, es[k] / den, gw)
        rk = jnp.where(lane == k, rank_k, rk)
    ei_o[...] = ei.astype(I32)
    gw_o[...] = gw
    rk_o[...] = rk.astype(I32)
    carry[0:1, :] = carry[0:1, :] + jnp.sum(onehot, axis=0, keepdims=True)
    cnt_o[...] = carry[...]


def _merge(x, o_nsa4, o_gla, m_a, m_b, mw):
    n = x.shape[0]
    tm = TM_PROJ
    row = lambda w: pl.BlockSpec((tm, w), lambda i: (i, 0))
    full = lambda a: pl.BlockSpec(a.shape, lambda i: (0,) * a.ndim)
    ws = (mw["wn"], mw["wg"], mw["wo"], mw["gf"], mw["rw"], mw["rb"])
    return pl.pallas_call(
        _merge_kernel,
        grid=(n // tm,),
        in_specs=[row(D_MODEL), pl.BlockSpec((NSA_GROUP, tm, LANES), lambda i: (0, i, 0)), row(GLA_V_WIDTH),
                  row(D_MODEL), row(D_MODEL)] + [full(w) for w in ws],
        out_specs=[row(D_MODEL), pl.BlockSpec((tm * SUBLANES, LANES), lambda i: (i, 0)),
                   row(LANES), row(LANES), row(LANES), pl.BlockSpec((SUBLANES, LANES), lambda i: (0, 0))],
        out_shape=[jax.ShapeDtypeStruct((n, D_MODEL), F32), jax.ShapeDtypeStruct((n * SUBLANES, LANES), F32),
                   jax.ShapeDtypeStruct((n, LANES), I32), jax.ShapeDtypeStruct((n, LANES), F32),
                   jax.ShapeDtypeStruct((n, LANES), I32), jax.ShapeDtypeStruct((SUBLANES, LANES), F32)],
        scratch_shapes=[pltpu.VMEM((SUBLANES, LANES), F32)],
        compiler_params=_params(("arbitrary",)),
        name="merge_router",
    )(x, o_nsa4, o_gla, m_a, m_b, *ws)


def _token_tile(ref, row):
    return ref.at[pl.ds(row * SUBLANES, SUBLANES)]


def _dispatch_kernel(meta_ref, dest_ref, h_ref, xs_hbm, dsm, ztile, sem_idx, sem_row, sem_pad, *, td):
    i = pl.program_id(0)
    n_steps = pl.num_programs(0)
    idx_copy = pltpu.make_async_copy(dest_ref, dsm, sem_idx)
    idx_copy.start()
    idx_copy.wait()

    def body(tk, _):
        for k in range(TOP_K):
            j = tk * TOP_K + k
            d = dsm[j // LANES, j % LANES]
            pltpu.make_async_copy(_token_tile(h_ref, tk), _token_tile(xs_hbm, d), sem_row).start()
        return 0
    lax.fori_loop(0, td, body, 0)

    @pl.when(i == n_steps - 1)
    def _():
        ztile[...] = jnp.zeros(ztile.shape, F32)
        blk_rows = ztile.shape[0]
        n_blocks = xs_hbm.shape[0] // blk_rows

        def pads(start_or_wait):
            def per_expert(e, _):
                first = meta_ref[N_EXPERTS + e] + meta_ref[e]
                last = meta_ref[N_EXPERTS + e] + meta_ref[2 * N_EXPERTS + e]

                def per_row(rw, _):
                    cp = pltpu.make_async_copy(ztile.at[pl.ds(0, SUBLANES)], _token_tile(xs_hbm, rw), sem_pad)
                    cp.start() if start_or_wait else cp.wait()
                    return 0
                lax.fori_loop(first, last, per_row, 0)
                return 0
            lax.fori_loop(0, N_EXPERTS, per_expert, 0)

            def per_block(bk, _):
                cp = pltpu.make_async_copy(ztile, xs_hbm.at[pl.ds(bk * blk_rows, blk_rows)], sem_pad)
                cp.start() if start_or_wait else cp.wait()
                return 0
            lax.fori_loop(meta_ref[3 * N_EXPERTS], n_blocks, per_block, 0)
        pads(True)
        pads(False)

    for _ in range(TOP_K):
        pltpu.make_async_copy(h_ref, xs_hbm.at[pl.ds(0, td * SUBLANES)], sem_row).wait()


def _dispatch(meta, dest2d, h2t, n_rows_total, td):
    n = h2t.shape[0] // SUBLANES
    hbm = pl.BlockSpec(memory_space=pl.ANY)
    drows = td * TOP_K // LANES
    return pl.pallas_call(
        functools.partial(_dispatch_kernel, td=td),
        grid_spec=pltpu.PrefetchScalarGridSpec(
            num_scalar_prefetch=1,
            grid=(n // td,),
            in_specs=[pl.BlockSpec((drows, LANES), lambda i, m: (i, 0)),
                      pl.BlockSpec((td * SUBLANES, LANES), lambda i, m: (i, 0))],
            out_specs=hbm,
            scratch_shapes=[pltpu.SMEM((drows, LANES), I32), pltpu.VMEM((MOE_ROWS * SUBLANES, LANES), F32),
                            pltpu.SemaphoreType.DMA, pltpu.SemaphoreType.DMA, pltpu.SemaphoreType.DMA],
        ),
        out_shape=jax.ShapeDtypeStruct((n_rows_total * SUBLANES, LANES), F32),
        compiler_params=_params(("arbitrary",)),
        name="moe_dispatch",
    )(meta, dest2d, h2t)


def _moe_kernel(be_ref, nu_ref, xs_ref, wg_ref, bg_ref, wu_ref, bu_ref, wd_ref, bd_ref, y_ref):
    i = pl.program_id(0)
    rows = xs_ref.shape[0] // SUBLANES
    n_s = D_MODEL // LANES

    @pl.when(i < nu_ref[0])
    def _():
        xb = jnp.concatenate([xs_ref[pl.ds(s, rows, stride=SUBLANES), :] for s in range(n_s)], axis=1).astype(BF16)
        g = _dot(xb, wg_ref[0]) + bg_ref[0]
        u = _dot(xb, wu_ref[0]) + bu_ref[0]
        g = jnp.minimum(g, SWIGLU_LIMIT)
        u = jnp.clip(u, -SWIGLU_LIMIT, SWIGLU_LIMIT)
        hh = (u + 1.0) * (g * jax.nn.sigmoid(SWIGLU_ALPHA * g))
        y = _dot(hh.astype(BF16), wd_ref[0]) + bd_ref[0]
        for s in range(n_s):
            y_ref[pl.ds(s, rows, stride=SUBLANES), :] = y[:, s * LANES:(s + 1) * LANES]

    @pl.when(i >= nu_ref[0])
    def _():
        y_ref[...] = jnp.zeros(y_ref.shape, F32)


def _moe_experts(block_e, n_used, xs, ew, n_blocks):
    blk = lambda i, be, nu: jnp.minimum(i, nu[0] - 1)
    rows = pl.BlockSpec((MOE_ROWS * SUBLANES, LANES), lambda i, be, nu: (i, 0))
    wspec = lambda a: pl.BlockSpec((1,) + a.shape[1:], lambda i, be, nu: (be[blk(i, be, nu)], 0, 0))
    ws = (ew["wg"], ew["bg"], ew["wu"], ew["bu"], ew["wd"], ew["bd"])
    return pl.pallas_call(
        _moe_kernel,
        grid_spec=pltpu.PrefetchScalarGridSpec(
            num_scalar_prefetch=2,
            grid=(n_blocks,),
            in_specs=[rows] + [wspec(w) for w in ws],
            out_specs=rows,
        ),
        out_shape=jax.ShapeDtypeStruct(xs.shape, F32),
        compiler_params=_params(("arbitrary",)),
        name="moe_experts",
    )(block_e, n_used, xs, *ws)


def _combine_kernel(dest_ref, gw_ref, x1_ref, gfin_ref, y_hbm, out_ref, dsm, buf, sem_idx, sem_row):
    tc = x1_ref.shape[0]
    idx_copy = pltpu.make_async_copy(dest_ref, dsm, sem_idx)
    idx_copy.start()
    idx_copy.wait()

    def body(tk, _):
        for k in range(TOP_K):
            j = tk * TOP_K + k
            d = dsm[j // LANES, j % LANES]
            pltpu.make_async_copy(_token_tile(y_hbm, d), _token_tile(buf.at[k], tk), sem_row).start()
        return 0
    lax.fori_loop(0, tc, body, 0)
    for k in range(TOP_K):
        pltpu.make_async_copy(y_hbm.at[pl.ds(0, tc * SUBLANES)], buf.at[k], sem_row).wait()

    gw = gw_ref[...]
    parts = []
    for s in range(D_MODEL // LANES):
        acc = None
        for k in range(TOP_K):
            term = buf[k, pl.ds(s, tc, stride=SUBLANES), :] * gw[:, k:k + 1]
            acc = term if acc is None else acc + term
        parts.append(acc)
    x2 = x1_ref[...] + jnp.concatenate(parts, axis=1)
    out_ref[...] = x2 * lax.rsqrt(jnp.mean(x2 * x2, axis=-1, keepdims=True) + RMS_EPS) * gfin_ref[...]


def _combine(dest2d, gw, x1, g_final, y_rows):
    n = x1.shape[0]
    tc = TC_COMBINE
    drows = tc * TOP_K // LANES
    row = lambda w: pl.BlockSpec((tc, w), lambda i: (i, 0))
    return pl.pallas_call(
        _combine_kernel,
        grid=(n // tc,),
        in_specs=[pl.BlockSpec((drows, LANES), lambda i: (i, 0)), row(LANES), row(D_MODEL),
                  pl.BlockSpec(g_final.shape, lambda i: (0, 0)), pl.BlockSpec(memory_space=pl.ANY)],
        out_specs=row(D_MODEL),
        out_shape=jax.ShapeDtypeStruct((n, D_MODEL), F32),
        scratch_shapes=[pltpu.SMEM((drows, LANES), I32), pltpu.VMEM((TOP_K, tc * SUBLANES, LANES), F32),
                        pltpu.SemaphoreType.DMA, pltpu.SemaphoreType.DMA],
        compiler_params=_params(("arbitrary",)),
        name="moe_combine",
    )(dest2d, gw, x1, g_final, y_rows)


def _bucket_table(max_dist):
    n = np.arange(max_dist, dtype=np.int64)
    scaled = np.log(np.maximum(n, 1).astype(np.float64) / REL_EXACT) / math.log(REL_MAX_DIST / REL_EXACT)
    large = REL_EXACT + (scaled * (REL_BUCKETS - REL_EXACT)).astype(np.int64)
    return np.where(n < REL_EXACT, n, np.minimum(large, REL_BUCKETS - 1)).astype(np.int32)


def _bias_lookup(rel_bias, dist):
    d = np.maximum(dist, 0)
    buckets = _bucket_table(int(d.max()) + 1)[d]
    return jnp.take(rel_bias.astype(F32).T, jnp.asarray(buckets), axis=1)


def _skew(w, n_rows, step, width):
    h, l = w.shape
    flat = jnp.tile(w, (1, n_rows))[:, :n_rows * (l - step)]
    return flat.reshape(h, n_rows, l - step)[:, :, :width]


def _importance_matrix(n_rows, n_cmp, n_sel, n_cols):
    rc = CMP_BLOCK // CMP_STRIDE
    rs = SEL_BLOCK // CMP_STRIDE
    m = np.zeros((n_rows, n_cols), np.float32)
    for j in range(n_sel):
        for o in range(rs + rc - 1):
            w = min(o - (rc - 1) + rc, rs) - max(o - (rc - 1), 0)
            c = rs * j + o - (rc - 1)
            if 0 <= c < n_cmp:
                m[c, j] += w
    return jnp.asarray(m)


def _block_expand(n_blocks, n_rows=LANES):
    e = np.zeros((n_rows, n_blocks * SEL_BLOCK), np.float32)
    for j in range(n_blocks):
        e[j, j * SEL_BLOCK:(j + 1) * SEL_BLOCK] = 1.0
    return jnp.asarray(e, dtype=BF16)


def _gate_expand():
    e = np.zeros((LG_PAD, 3 * NSA_WIDTH), np.float32)
    for g in range(NSA_KV_HEADS):
        for r in range(NSA_GROUP):
            for c in range(3):
                lo = c * NSA_WIDTH + r * LANES + g * HEAD_DIM
                e[(g * NSA_GROUP + r) * 3 + c, lo:lo + HEAD_DIM] = 1.0
    return jnp.asarray(e)


def _prompt_tables(rel_bias, t):
    n_qb = t // Q_BLOCK
    n_ch = t // CMP_STRIDE
    tk = min(SEL_CHUNK, t)
    voff = max(t - Q_BLOCK, WINDOW)
    nv = voff // LANES + tk // LANES
    wd = nv * LANES
    wv = _bias_lookup(rel_bias, np.concatenate([voff - np.arange(wd), voff + np.arange(Q_BLOCK, 0, -1)]))
    gr = _skew(wv, Q_BLOCK, 1, wd).reshape(NSA_HEADS, Q_BLOCK, nv, LANES).transpose(2, 0, 1, 3)
    wc = _bias_lookup(rel_bias, np.concatenate([np.arange(t) - (CMP_BLOCK - 1), np.zeros(CMP_STRIDE * n_ch, np.int64)]))
    bc = _skew(wc, n_ch, CMP_STRIDE, t).transpose(0, 2, 1).reshape(NSA_HEADS, n_qb, Q_BLOCK, n_ch).transpose(1, 0, 2, 3)
    return dict(gr=gr, bc=bc, voff=voff,
                mimp=_importance_matrix(n_ch, n_ch - 1, t // SEL_BLOCK, LANES),
                esel=_block_expand(t // SEL_BLOCK), eg=_gate_expand())


def _sample_tables(rel_bias, past, s_new, w_buf):
    n_ch = past // CMP_STRIDE
    n_sel = (past + s_new + SEL_BLOCK - 1) // SEL_BLOCK
    per_group = lambda a: a.reshape(NSA_KV_HEADS, NSA_GROUP, s_new, -1).reshape(NSA_KV_HEADS, NSA_GROUP * s_new, -1)
    qi = np.arange(s_new)

    def table(dist):
        return per_group(_bias_lookup(rel_bias, dist))
    ws = _bias_lookup(rel_bias, np.concatenate([past - np.arange(past), past + np.arange(s_new, 0, -1)]))
    gs = per_group(_skew(ws, s_new, 1, past))
    jn = np.arange(LANES)
    gn = table(np.where(jn[None, :] < s_new, qi[:, None] - jn[None, :], 0))
    cend = np.arange(n_ch) * CMP_STRIDE + (CMP_BLOCK - 1)
    bcs = table(past + qi[:, None] - cend[None, :])
    bws = table(w_buf + qi[:, None] - np.arange(w_buf)[None, :])
    sel_lanes = -(-n_sel // LANES) * LANES
    return dict(gs=gs, gn=gn, bcs=bcs, bws=bws,
                mimp_s=_importance_matrix(n_ch, n_ch - 1, n_sel, sel_lanes),
                esel_s=_block_expand(past // SEL_BLOCK, past // SEL_BLOCK), eg=_gate_expand())


def _compress_weights(pe, w1, w2):
    rc = CMP_BLOCK // CMP_STRIDE
    w1r = w1.reshape(rc, CMP_STRIDE, HEAD_DIM, CMP_HIDDEN)
    eye = jnp.eye(NSA_KV_HEADS, dtype=w1.dtype)
    wcat = jnp.einsum("rldh,ge->lgdreh", w1r, eye).reshape(CMP_STRIDE, KV_WIDTH, rc * NSA_KV_HEADS * CMP_HIDDEN)
    w2bd = jnp.einsum("hd,ge->ghed", w2, eye).reshape(NSA_KV_HEADS * CMP_HIDDEN, KV_WIDTH)
    pe_rows = jnp.concatenate([pe.reshape(1, CMP_BLOCK * HEAD_DIM),
                               jnp.zeros((SUBLANES - 1, CMP_BLOCK * HEAD_DIM), pe.dtype)], axis=0)
    return wcat.astype(BF16), pe_rows, w1.reshape(CMP_BLOCK * HEAD_DIM, CMP_HIDDEN), w2bd.astype(BF16)


def _layer_weights(w_in, gla_w_alpha, gla_b_alpha, w_branch_nsa, norm_ffn, router_w, router_b):
    offs = np.cumsum((0,) + IN_SPLITS)
    col = lambda j: w_in[:, offs[j]:offs[j + 1]]
    q_perm = col(0).reshape(D_MODEL, NSA_KV_HEADS, NSA_GROUP, HEAD_DIM).transpose(0, 2, 1, 3).reshape(D_MODEL, NSA_WIDTH)
    pad = jnp.zeros((D_MODEL, LG_PAD - 3 * NSA_HEADS - GLA_RANK), w_in.dtype)
    w_p = jnp.concatenate([q_perm, col(1), col(3), col(4), col(5), col(7), col(8), col(9), col(2), col(6), pad],
                          axis=1).astype(BF16)
    w_al = jnp.zeros((LG_PAD, GLA_K_WIDTH), F32).at[3 * NSA_HEADS:3 * NSA_HEADS + GLA_RANK].set(gla_w_alpha)
    wn = w_branch_nsa.reshape(NSA_KV_HEADS, NSA_GROUP, HEAD_DIM, D_MODEL).transpose(1, 0, 2, 3).reshape(NSA_WIDTH, D_MODEL)
    rw = jnp.concatenate([router_w, jnp.zeros((D_MODEL, LANES - N_EXPERTS), F32)], axis=1)
    rb = jnp.concatenate([router_b, jnp.full((LANES - N_EXPERTS,), NEG_INF, F32)]).reshape(1, LANES)
    return w_p, w_al, gla_b_alpha.reshape(1, GLA_K_WIDTH), wn.astype(BF16), norm_ffn.reshape(1, D_MODEL), rw, rb


def _moe(h2t, ei, gw, rk, counts, x1, g_final, ew):
    n = x1.shape[0]
    nk = n * TOP_K
    counts = counts[0, :N_EXPERTS].astype(I32)
    padded = (counts + MOE_ROWS - 1) // MOE_ROWS * MOE_ROWS
    pends = jnp.cumsum(padded)
    pstarts = pends - padded
    n_blocks = (nk + N_EXPERTS * (MOE_ROWS - 1) + MOE_ROWS - 1) // MOE_ROWS
    blk_start = jnp.arange(n_blocks, dtype=I32) * MOE_ROWS
    block_e = jnp.minimum(jnp.sum((pends[None, :] <= blk_start[:, None]).astype(I32), axis=1), N_EXPERTS - 1)
    n_used = (pends[-1] // MOE_ROWS).astype(I32).reshape(1)
    e_sel = ei[:, :TOP_K, None] == jnp.arange(N_EXPERTS, dtype=I32)
    dest = jnp.sum(jnp.where(e_sel, pstarts.astype(I32), 0), axis=-1) + rk[:, :TOP_K]
    dest2d = dest.reshape(nk // LANES, LANES)
    meta = jnp.concatenate([counts, pstarts, padded, n_used]).astype(I32)
    td = min(TD_DISPATCH, n)
    xs = _dispatch(meta, dest2d, h2t, n_blocks * MOE_ROWS, td)
    y_rows = _moe_experts(block_e, n_used, xs, ew, n_blocks)
    return _combine(dest2d, gw, x1, g_final, y_rows)


def kernel(x_prompt, x_sample, cache_cmp_k, cache_cmp_v, cache_sel_k, cache_sel_v, state_win_k, state_win_v, state_gla, page_table, rel_bias, norm_mix, w_in, nsa_pe_k, nsa_pe_v, nsa_w1_k, nsa_w1_v, nsa_w2_k, nsa_w2_v, gla_w_alpha, gla_b_alpha, gla_norm, w_branch_nsa, w_branch_gla, w_out, norm_ffn, router_w, router_b, exp_w_gate, exp_b_gate, exp_w_up, exp_b_up, exp_w_down, exp_b_down, norm_final):
    depth = w_in.shape[0]
    assert depth == 1, "single-layer trunk"
    bsz, t, d = x_prompt.shape
    n_seq, s_new, _ = x_sample.shape
    n_pages = page_table.shape[1]
    past = n_pages * PAGE_SIZE
    w_buf = state_win_k.shape[2]
    assert d == D_MODEL and t % Q_BLOCK == 0 and t % GLA_CHUNK == 0

    w_p, w_al, b_al, wn, gf, rw, rb = _layer_weights(w_in[0], gla_w_alpha[0], gla_b_alpha[0], w_branch_nsa[0],
                                                     norm_ffn[0], router_w[0], router_b[0])
    g_mix = norm_mix[0].reshape(1, D_MODEL)
    g_fin = norm_final.reshape(1, D_MODEL)
    g_gla = gla_norm[0].reshape(1, GLA_DV)
    mw = dict(wn=wn, wg=w_branch_gla[0].astype(BF16), wo=w_out[0].astype(BF16), gf=gf, rw=rw, rb=rb)
    ew = dict(wg=exp_w_gate[0].astype(BF16), bg=exp_b_gate[0].reshape(N_EXPERTS, 1, D_FF),
              wu=exp_w_up[0].astype(BF16), bu=exp_b_up[0].reshape(N_EXPERTS, 1, D_FF),
              wd=exp_w_down[0].astype(BF16), bd=exp_b_down[0].reshape(N_EXPERTS, 1, D_MODEL))
    cw = {}
    for nm, pe, w1, w2 in (("k", nsa_pe_k[0], nsa_w1_k[0], nsa_w2_k[0]), ("v", nsa_pe_v[0], nsa_w1_v[0], nsa_w2_v[0])):
        cw["wcat_" + nm], cw["pe_" + nm], cw["w1f_" + nm], cw["w2_" + nm] = _compress_weights(pe, w1, w2)

    xp = x_prompt.reshape(bsz * t, d)
    (q4, kck, kcv, ksk, ksv, kwk, kwv, gates, q_l, k_l, v_l, lg, r_l, m_a, m_b) = _in_projection(xp, g_mix, w_p, w_al, b_al)
    kc, vc = _compress_prompt(kck, kcv, cw, bsz, t)
    o_nsa = _nsa_prompt(q4, gates, kc, vc, ksk, ksv, kwk, kwv, _prompt_tables(rel_bias, t), bsz, t)
    s_zero = jnp.zeros((bsz * GLA_HEADS * GLA_DK, GLA_DV), F32)
    o_gla, p_gla = _gla(q_l, k_l, v_l, lg, r_l, g_gla, s_zero, bsz, t // GLA_CHUNK, GLA_CHUNK)
    x1, h2t, ei, gw, rk, counts = _merge(xp, o_nsa, o_gla, m_a, m_b, mw)
    y_prompt = _moe(h2t, ei, gw, rk, counts, x1, g_fin, ew).reshape(bsz, t, d)

    kv_shape = (1, bsz, t, NSA_KV_HEADS, HEAD_DIM)
    w_len = min(WINDOW, t)
    win = lambda a: a.reshape(kv_shape)[:, :, t - w_len:]
    p_states = (kck.reshape(kv_shape), kcv.reshape(kv_shape), ksk.reshape(kv_shape), ksv.reshape(kv_shape),
                win(kwk), win(kwv), p_gla.reshape(1, bsz, GLA_HEADS, GLA_DK, GLA_DV))

    xs = x_sample.reshape(n_seq * s_new, d)
    (q4, kck, kcv, ksk, ksv, kwk, kwv, gates, q_l, k_l, v_l, lg, r_l, m_a, m_b) = _in_projection(xs, g_mix, w_p, w_al, b_al)
    pt_flat = page_table.reshape(n_seq * n_pages).astype(I32)
    pool = lambda c: c[0].transpose(0, 2, 3, 1).reshape(-1, PAGE_SIZE)
    kc, vc = _compress_sample(pt_flat, pool(cache_cmp_k), pool(cache_cmp_v), cw, n_seq, n_pages)
    buf_kw = state_win_k[0].transpose(0, 2, 3, 1).reshape(n_seq * KV_WIDTH, w_buf)
    buf_vw = state_win_v[0].transpose(0, 2, 3, 1).reshape(n_seq * KV_WIDTH, w_buf)
    o_nsa = _nsa_sample(pt_flat, q4, gates, kc, vc, ksk, ksv, kwk, kwv, buf_kw, buf_vw,
                        pool(cache_sel_k), pool(cache_sel_v), _sample_tables(rel_bias, past, s_new, w_buf),
                        n_seq, s_new, n_pages)
    cl = 16
    padc = lambda a: jnp.pad(a.reshape(n_seq, s_new, -1), ((0, 0), (0, cl - s_new), (0, 0))).reshape(n_seq * cl, -1)
    s_in = state_gla[0].reshape(n_seq * GLA_HEADS * GLA_DK, GLA_DV)
    o_gla, s_gla = _gla(padc(q_l), padc(k_l), padc(v_l), padc(lg), padc(r_l), g_gla, s_in, n_seq, 1, cl)
    o_gla = o_gla.reshape(n_seq, cl, GLA_V_WIDTH)[:, :s_new].reshape(n_seq * s_new, GLA_V_WIDTH)
    x1, h2t, ei, gw, rk, counts = _merge(xs, o_nsa, o_gla, m_a, m_b, mw)
    y_sample = _moe(h2t, ei, gw, rk, counts, x1, g_fin, ew).reshape(n_seq, s_new, d)

    kvs = (1, n_seq, s_new, NSA_KV_HEADS, HEAD_DIM)
    new_win = lambda buf, new: jnp.concatenate([buf, new.reshape(kvs).astype(buf.dtype)], axis=2)[:, :, s_new:]
    s_states = (kck.reshape(kvs), kcv.reshape(kvs), ksk.reshape(kvs), ksv.reshape(kvs),
                new_win(state_win_k, kwk), new_win(state_win_v, kwv),
                s_gla.reshape(1, n_seq, GLA_HEADS, GLA_DK, GLA_DV))
    return (y_prompt, y_sample) + p_states + s_states
```

```python
import functools
import math

import numpy as np
import jax
import jax.numpy as jnp
from jax import lax
from jax.experimental import pallas as pl
from jax.experimental.pallas import tpu as pltpu

F32 = jnp.float32
BF16 = jnp.bfloat16
I32 = jnp.int32
HI = lax.Precision.HIGHEST

D_MODEL = 1024
PAGE_SIZE = 128
NSA_HEADS = 8
NSA_KV_HEADS = 2
NSA_GROUP = NSA_HEADS // NSA_KV_HEADS
HEAD_DIM = 64
NSA_WIDTH = NSA_HEADS * HEAD_DIM
KV_WIDTH = NSA_KV_HEADS * HEAD_DIM
CMP_BLOCK = 32
CMP_STRIDE = 16
CMP_HIDDEN = 2 * HEAD_DIM
SEL_BLOCK = 64
N_SEL = 16
WINDOW = 512
Q_BLOCK = 128
SEL_FORCE = 1e9
GLA_HEADS = 4
GLA_DK = 64
GLA_DV = 128
GLA_K_WIDTH = GLA_HEADS * GLA_DK
GLA_V_WIDTH = GLA_HEADS * GLA_DV
GLA_RANK = 16
GLA_TAU = 16.0
GLA_CHUNK = 64
N_EXPERTS = 32
TOP_K = 4
D_FF = D_MODEL
SWIGLU_ALPHA = 1.702
SWIGLU_LIMIT = 7.0
REL_BUCKETS = 32
REL_EXACT = REL_BUCKETS // 2
REL_MAX_DIST = 1024
RMS_EPS = 1e-6
NEG_INF = -1e30
IN_SPLITS = (NSA_WIDTH, 6 * KV_WIDTH, 3 * NSA_HEADS, GLA_K_WIDTH, GLA_K_WIDTH, GLA_V_WIDTH, GLA_RANK,
             GLA_V_WIDTH, D_MODEL, D_MODEL)

LANES = 128
SUBLANES = 8
VMEM_LIMIT = 56 * 1024 * 1024

TM_PROJ = 256
MOE_ROWS = 256
TD_DISPATCH = 512
TC_COMBINE = 256
SEL_CHUNK = 512
PIPE_HEADS = 2
LG_PAD = LANES


def _nt(a, b, **kw):
    return lax.dot_general(a, b, (((1,), (1,)), ((), ())), preferred_element_type=F32, **kw)


def _tn(a, b, **kw):
    return lax.dot_general(a, b, (((0,), (0,)), ((), ())), preferred_element_type=F32, **kw)


def _dot(a, b, **kw):
    return jnp.dot(a, b, preferred_element_type=F32, **kw)


def _params(sem, vmem=VMEM_LIMIT):
    return pltpu.CompilerParams(dimension_semantics=sem, vmem_limit_bytes=vmem)


def _masked_softmax_parts(s, valid):
    s = jnp.where(valid, s, NEG_INF)
    m = jnp.max(s, axis=-1, keepdims=True)
    e = jnp.where(valid, jnp.exp(s - m), 0.0)
    return e, jnp.maximum(jnp.sum(e, axis=-1, keepdims=True), 1e-20)


_OFF_Q = 0
_OFF_KV = _OFF_Q + NSA_WIDTH
_OFF_QL = _OFF_KV + 6 * KV_WIDTH
_OFF_KL = _OFF_QL + GLA_K_WIDTH
_OFF_VL = _OFF_KL + GLA_K_WIDTH
_OFF_R = _OFF_VL + GLA_V_WIDTH
_OFF_MA = _OFF_R + GLA_V_WIDTH
_OFF_MB = _OFF_MA + D_MODEL
_OFF_GA = _OFF_MB + D_MODEL
_N_PROJ = _OFF_GA + LG_PAD


def _inproj_kernel(x_ref, g_ref, w_ref, wal_ref, bal_ref,
                   q_o, kck_o, kcv_o, ksk_o, ksv_o, kwk_o, kwv_o, gt_o, ql_o, kl_o, vl_o, lg_o, r_o, ma_o, mb_o):
    x = x_ref[...]
    xn = x * lax.rsqrt(jnp.mean(x * x, axis=-1, keepdims=True) + RMS_EPS)
    xn = (xn * g_ref[...]).astype(BF16)

    def mm(lo, n):
        return _dot(xn, w_ref[:, lo:lo + n])

    q = mm(_OFF_Q, NSA_WIDTH) * (HEAD_DIM ** -0.5)
    for r in range(NSA_GROUP):
        q_o[r] = q[:, r * LANES:(r + 1) * LANES]
    for j, o in enumerate((kck_o, kcv_o, ksk_o, ksv_o, kwk_o, kwv_o)):
        o[...] = mm(_OFF_KV + j * KV_WIDTH, KV_WIDTH)
    ql_o[...] = mm(_OFF_QL, GLA_K_WIDTH) * (GLA_DK ** -0.5)
    kl_o[...] = mm(_OFF_KL, GLA_K_WIDTH)
    vl_o[...] = mm(_OFF_VL, GLA_V_WIDTH)
    r_o[...] = mm(_OFF_R, GLA_V_WIDTH)
    ma_o[...] = mm(_OFF_MA, D_MODEL)
    mb_o[...] = mm(_OFF_MB, D_MODEL)
    ga = mm(_OFF_GA, LG_PAD)
    gt_o[...] = jax.nn.sigmoid(ga)
    al = _dot(ga, wal_ref[...], precision=HI) + bal_ref[...]
    lg_o[...] = (jnp.minimum(al, 0.0) - jnp.log1p(jnp.exp(-jnp.abs(al)))) * (1.0 / GLA_TAU)


def _in_projection(x, norm_g, w_p, w_al, b_al):
    n = x.shape[0]
    tm = TM_PROJ
    assert n % tm == 0
    row = lambda w: pl.BlockSpec((tm, w), lambda i: (i, 0))
    full = lambda a: pl.BlockSpec(a.shape, lambda i: (0,) * a.ndim)
    widths = (KV_WIDTH,) * 6 + (LG_PAD, GLA_K_WIDTH, GLA_K_WIDTH, GLA_V_WIDTH, GLA_K_WIDTH, GLA_V_WIDTH,
                                 D_MODEL, D_MODEL)
    out_shape = [jax.ShapeDtypeStruct((NSA_GROUP, n, LANES), F32)] + [jax.ShapeDtypeStruct((n, w), F32) for w in widths]
    out_specs = [pl.BlockSpec((NSA_GROUP, tm, LANES), lambda i: (0, i, 0))] + [row(w) for w in widths]
    return pl.pallas_call(
        _inproj_kernel,
        grid=(n // tm,),
        in_specs=[row(D_MODEL), full(norm_g), full(w_p), full(w_al), full(b_al)],
        out_specs=out_specs,
        out_shape=out_shape,
        compiler_params=_params(("arbitrary",)),
        name="in_projection",
    )(x, norm_g, w_p, w_al, b_al)


def _gelu_tanh(x):
    return 0.5 * x * (1.0 + jnp.tanh(math.sqrt(2.0 / math.pi) * (x + 0.044715 * (x * x * x))))


def _compress_rows(src, n_ch, wcat_ref, pe_ref, w1f_ref, w2_ref):
    hid2 = NSA_KV_HEADS * CMP_HIDDEN
    acc = jnp.zeros((n_ch, 2 * hid2), F32)
    for l in range(CMP_STRIDE):
        xl = src[pl.ds(l, n_ch, stride=CMP_STRIDE), :].astype(BF16)
        acc = acc + _dot(xl, wcat_ref[l])
    bias = _dot(pe_ref[...], w1f_ref[...], precision=HI)[0:1]
    bias2 = jnp.concatenate([bias] * NSA_KV_HEADS, axis=1)
    nxt = pltpu.roll(acc[:, hid2:], n_ch - 1, 0)
    h = acc[:, :hid2] + nxt + bias2
    return _dot(_gelu_tanh(h).astype(BF16), w2_ref[...])


def _compress_prompt_kernel(k_ref, v_ref, wk_ref, wv_ref, pek_ref, pev_ref, w1k_ref, w1v_ref, w2k_ref, w2v_ref,
                            kc_o, vc_o):
    n_ch = kc_o.shape[0]
    kc_o[...] = _compress_rows(k_ref, n_ch, wk_ref, pek_ref, w1k_ref, w2k_ref)
    vc_o[...] = _compress_rows(v_ref, n_ch, wv_ref, pev_ref, w1v_ref, w2v_ref)


def _compress_prompt(k_cmp, v_cmp, cw, bsz, t):
    n_ch = t // CMP_STRIDE
    full = lambda a: pl.BlockSpec(a.shape, lambda b: (0,) * a.ndim)
    seq = pl.BlockSpec((t, KV_WIDTH), lambda b: (b, 0))
    out = pl.BlockSpec((n_ch, KV_WIDTH), lambda b: (b, 0))
    ws = (cw["wcat_k"], cw["wcat_v"], cw["pe_k"], cw["pe_v"], cw["w1f_k"], cw["w1f_v"], cw["w2_k"], cw["w2_v"])
    return pl.pallas_call(
        _compress_prompt_kernel,
        grid=(bsz,),
        in_specs=[seq, seq] + [full(w) for w in ws],
        out_specs=[out, out],
        out_shape=[jax.ShapeDtypeStruct((bsz * n_ch, KV_WIDTH), F32)] * 2,
        compiler_params=_params(("arbitrary",)),
        name="nsa_compress_prompt",
    )(k_cmp, v_cmp, *ws)


def _paged_fetch(pt_ref, pools, bufs, sems, seq, slot, n_pages, pages_on_lanes):
    def body(p, _):
        pg = pt_ref[seq * n_pages + p]
        off = pl.multiple_of(p * PAGE_SIZE, PAGE_SIZE)
        for j, (pool, buf) in enumerate(zip(pools, bufs)):
            dst = buf.at[slot, :, pl.ds(off, PAGE_SIZE)] if pages_on_lanes else buf.at[slot, pl.ds(off, PAGE_SIZE)]
            pltpu.make_async_copy(pool.at[pl.ds(pg * KV_WIDTH, KV_WIDTH)], dst, sems.at[j, slot]).start()
        return 0
    lax.fori_loop(0, n_pages, body, 0)


def _paged_wait(bufs, sems, slot):
    for j, buf in enumerate(bufs):
        pltpu.make_async_copy(buf.at[slot], buf.at[slot], sems.at[j, slot]).wait()


def _compress_sample_kernel(pt_ref, pk_hbm, pv_hbm, wk_ref, wv_ref, pek_ref, pev_ref, w1k_ref, w1v_ref,
                            w2k_ref, w2v_ref, kc_o, vc_o, bufk, bufv, sems, rows_k, rows_v, *, n_pages):
    s = pl.program_id(0)
    n_seq = pl.num_programs(0)
    slot = s % 2
    pools, bufs = (pk_hbm, pv_hbm), (bufk, bufv)

    @pl.when(s == 0)
    def _():
        _paged_fetch(pt_ref, pools, bufs, sems, s, slot, n_pages, False)

    @pl.when(s + 1 < n_seq)
    def _():
        _paged_fetch(pt_ref, pools, bufs, sems, s + 1, 1 - slot, n_pages, False)

    _paged_wait(bufs, sems, slot)
    for p in range(n_pages):
        rows = slice(p * PAGE_SIZE, (p + 1) * PAGE_SIZE)
        rows_k[rows, :] = bufk[slot, rows, :].T
        rows_v[rows, :] = bufv[slot, rows, :].T
    n_ch = kc_o.shape[0]
    kc_o[...] = _compress_rows(rows_k, n_ch, wk_ref, pek_ref, w1k_ref, w2k_ref)
    vc_o[...] = _compress_rows(rows_v, n_ch, wv_ref, pev_ref, w1v_ref, w2v_ref)


def _compress_sample(pt_flat, pool_k, pool_v, cw, n_seq, n_pages):
    past = n_pages * PAGE_SIZE
    n_ch = past // CMP_STRIDE
    full = lambda a: pl.BlockSpec(a.shape, lambda s, pt: (0,) * a.ndim)
    hbm = pl.BlockSpec(memory_space=pl.ANY)
    out = pl.BlockSpec((n_ch, KV_WIDTH), lambda s, pt: (s, 0))
    ws = (cw["wcat_k"], cw["wcat_v"], cw["pe_k"], cw["pe_v"], cw["w1f_k"], cw["w1f_v"], cw["w2_k"], cw["w2_v"])
    return pl.pallas_call(
        functools.partial(_compress_sample_kernel, n_pages=n_pages),
        grid_spec=pltpu.PrefetchScalarGridSpec(
            num_scalar_prefetch=1,
            grid=(n_seq,),
            in_specs=[hbm, hbm] + [full(w) for w in ws],
            out_specs=[out, out],
            scratch_shapes=[pltpu.VMEM((2, past, KV_WIDTH), F32), pltpu.VMEM((2, past, KV_WIDTH), F32),
                            pltpu.SemaphoreType.DMA((2, 2)),
                            pltpu.VMEM((past, KV_WIDTH), F32), pltpu.VMEM((past, KV_WIDTH), F32)],
        ),
        out_shape=[jax.ShapeDtypeStruct((n_seq * n_ch, KV_WIDTH), F32)] * 2,
        compiler_params=_params(("arbitrary",)),
        name="nsa_compress_sample",
    )(pt_flat, pool_k, pool_v, *ws)


def _select_blocks(score, n_sel):
    blk = lax.broadcasted_iota(I32, score.shape, 1)
    cnt = jnp.zeros(score.shape, F32)
    for i in range(n_sel):
        col = score[:, i:i + 1]
        ahead = (col > score) | ((col == score) & (i < blk))
        cnt = cnt + jnp.where(ahead, 1.0, 0.0)
    n_top = min(N_SEL, n_sel)
    return jnp.where((cnt < n_top) & (blk < n_sel), 1.0, 0.0)


def _select_blocks_t(score_t, n_sel):
    blk = lax.broadcasted_iota(I32, score_t.shape, 0)
    cnt = jnp.zeros(score_t.shape, F32)
    for i in range(n_sel):
        row = jnp.broadcast_to(score_t[i:i + 1, :], score_t.shape)
        ahead = (row > score_t) | ((row == score_t) & (i < blk))
        cnt = cnt + jnp.where(ahead, 1.0, 0.0)
    return jnp.where(cnt < min(N_SEL, n_sel), 1.0, 0.0)


def _block_scores(imp, qpos, n_sel):
    blk = lax.broadcasted_iota(I32, imp.shape, 1)
    cur = qpos // SEL_BLOCK
    forced = (blk == 0) | (blk == cur) | (blk == cur - 1)
    causal = (blk * SEL_BLOCK) <= qpos
    score = jnp.where(causal, jnp.where(forced, SEL_FORCE, imp), -SEL_FORCE)
    return jnp.where(blk < n_sel, score, -3e38)


def _gate_mix(gexp, o_c, o_s, o_w, r):
    out = None
    for c, o in enumerate((o_c, o_s, o_w)):
        term = gexp[:, c * NSA_WIDTH + r * LANES:c * NSA_WIDTH + (r + 1) * LANES] * o
        out = term if out is None else out + term
    return out


def _nsa_prompt_kernel(q_ref, gt_ref, kc_ref, vc_ref, ks_ref, vs_ref, kw_ref, vw_ref,
                       gr_ref, bc_ref, wt_ref, mimp_ref, esel_ref, eg_ref, o_ref,
                       ksb, vsb, kwb, vwb, kcb, vcb, oc_s, msk_s, qs, *state, voff_blk):
    qb = pl.program_id(1)
    t = ks_ref.shape[0]
    n_ch = kc_ref.shape[0]
    n_sel = t // SEL_BLOCK
    tk = msk_s.shape[3]
    n_kc = t // tk
    tiles = tk // LANES

    @pl.when(qb == 0)
    def _():
        ksb[...] = ks_ref[...].astype(BF16)
        vsb[...] = vs_ref[...].astype(BF16)
        kwb[0:WINDOW, :] = jnp.zeros((WINDOW, KV_WIDTH), BF16)
        vwb[0:WINDOW, :] = jnp.zeros((WINDOW, KV_WIDTH), BF16)
        kwb[WINDOW:, :] = kw_ref[...].astype(BF16)
        vwb[WINDOW:, :] = vw_ref[...].astype(BF16)
        kcb[...] = kc_ref[...].astype(BF16)
        vcb[...] = vc_ref[...].astype(BF16)

    q0 = qb * Q_BLOCK
    lane = lax.broadcasted_iota(I32, (Q_BLOCK, LANES), 1)
    qpos = q0 + lax.broadcasted_iota(I32, (Q_BLOCK, 1), 0)
    upper = lane >= HEAD_DIM

    nh = NSA_HEADS
    rows = nh * Q_BLOCK
    for h in range(nh):
        keep = upper if h >= NSA_GROUP else jnp.logical_not(upper)
        qs[h * Q_BLOCK:(h + 1) * Q_BLOCK, :] = jnp.where(keep, q_ref[h % NSA_GROUP], 0.0).astype(BF16)
    q_all = qs[...]

    def per_group(x3, add2):
        return jnp.concatenate([x3[g * NSA_GROUP:(g + 1) * NSA_GROUP] + add2[g][None]
                                for g in range(NSA_KV_HEADS)], axis=0)

    any_c = (qpos >= (CMP_BLOCK - 1))[None]
    s = _nt(q_all, kcb[...]).reshape(nh, Q_BLOCK, n_ch) + bc_ref[0]
    e = jnp.exp(s - jnp.max(s, axis=-1, keepdims=True))
    den = jnp.maximum(jnp.sum(e, axis=-1, keepdims=True), 1e-20)
    p = e * jnp.where(any_c, 1.0 / den, 0.0)
    oc_s[...] = _dot(p.reshape(rows, n_ch).astype(BF16), vcb[...])
    for g in range(NSA_KV_HEADS):
        psum = p[g * NSA_GROUP]
        for r in range(1, NSA_GROUP):
            psum = psum + p[g * NSA_GROUP + r]
        imp = _dot(psum, mimp_ref[...], precision=HI)
        sel_t = _select_blocks_t(_block_scores(imp, qpos, n_sel).T[0:n_sel], n_sel).astype(BF16)
        for c in range(n_kc):
            kpos = c * tk + lax.broadcasted_iota(I32, (Q_BLOCK, tk), 1)
            picked = _tn(sel_t, esel_ref[0:n_sel, c * tk:(c + 1) * tk]) > 0.5
            msk_s[g, c] = jnp.where(picked & (kpos <= qpos), 0.0, NEG_INF)

    ph = PIPE_HEADS
    nb = nh // ph
    brow = ph * Q_BLOCK
    m_r, l_r, a_r = state[0:nb], state[nb:2 * nb], state[2 * nb:3 * nb]
    for b in range(nb):
        m_r[b][...] = jnp.full((brow, 1), NEG_INF, F32)
        l_r[b][...] = jnp.zeros((brow, 1), F32)
        a_r[b][...] = jnp.zeros((brow, KV_WIDTH), F32)
    q_blk = lambda b: qs[b * brow:(b + 1) * brow, :]

    def chunk(kt, _):
        k0 = pl.multiple_of(kt * tk, tk)
        kk = ksb[pl.ds(k0, tk), :]
        vv = vsb[pl.ds(k0, tk), :]
        base = voff_blk - qb + kt * tiles

        def scores(b):
            bias = jnp.concatenate([gr_ref[base + j, b * ph:(b + 1) * ph] for j in range(tiles)], axis=-1)
            s = _nt(q_blk(b), kk).reshape(ph, Q_BLOCK, tk) + bias + msk_s[(b * ph) // NSA_GROUP, kt][None]
            return s.reshape(brow, tk)

        s_next = scores(0)
        for b in range(nb):
            s = s_next
            if b + 1 < nb:
                s_next = scores(b + 1)
            m_old = m_r[b][...]
            m_new = jnp.maximum(m_old, jnp.max(s, axis=-1, keepdims=True))
            alpha = jnp.exp(m_old - m_new)
            e = jnp.exp(s - m_new)
            l_r[b][...] = alpha * l_r[b][...] + jnp.sum(e, axis=-1, keepdims=True)
            m_r[b][...] = m_new
            a_r[b][...] = alpha * a_r[b][...] + _dot(e.astype(BF16), vv)
        return 0
    lax.fori_loop(0, (q0 + Q_BLOCK + tk - 1) // tk, chunk, 0)

    n_w = WINDOW + Q_BLOCK
    wpos = q0 - WINDOW + lax.broadcasted_iota(I32, (Q_BLOCK, n_w), 1)
    before_start = jnp.where(wpos >= 0, 0.0, NEG_INF)[None]
    w0 = pl.multiple_of(q0, Q_BLOCK)
    kw = kwb[pl.ds(w0, n_w), :]
    vw = vwb[pl.ds(w0, n_w), :]

    def w_scores(b):
        s = _nt(q_blk(b), kw).reshape(ph, Q_BLOCK, n_w) + wt_ref[b * ph:(b + 1) * ph] + before_start
        return s.reshape(brow, n_w)

    o_w = []
    s_next = w_scores(0)
    for b in range(nb):
        s = s_next
        if b + 1 < nb:
            s_next = w_scores(b + 1)
        e = jnp.exp(s - jnp.max(s, axis=-1, keepdims=True))
        den = jnp.maximum(jnp.sum(e, axis=-1, keepdims=True), 1e-20)
        o_w.append(_dot(e.astype(BF16), vw) / den)
    o_w = jnp.concatenate(o_w, axis=0)
    o_s = jnp.concatenate([a_r[b][...] / jnp.maximum(l_r[b][...], 1e-20) for b in range(nb)], axis=0)
    o_c = oc_s[...]

    gexp = _dot(gt_ref[...], eg_ref[...], precision=HI)
    head = lambda x, h: x[h * Q_BLOCK:(h + 1) * Q_BLOCK]
    for r in range(NSA_GROUP):
        pick = lambda x: jnp.where(upper, head(x, NSA_GROUP + r), head(x, r))
        o_ref[r] = _gate_mix(gexp, pick(o_c), pick(o_s), pick(o_w), r)


def _nsa_prompt(q4, gates, kc, vc, ks, vs, kw, vw, tabs, bsz, t):
    n_qb = t // Q_BLOCK
    n_ch = t // CMP_STRIDE
    tk = min(SEL_CHUNK, t)
    full = lambda a: pl.BlockSpec(a.shape, lambda b, i: (0,) * a.ndim)
    seq = pl.BlockSpec((t, KV_WIDTH), lambda b, i: (b, 0))
    cseq = pl.BlockSpec((n_ch, KV_WIDTH), lambda b, i: (b, 0))
    qspec = pl.BlockSpec((NSA_GROUP, Q_BLOCK, LANES), lambda b, i: (0, b * n_qb + i, 0))
    gr, bc, wt, mimp, esel, eg = tabs["gr"], tabs["bc"], tabs["wt"], tabs["mimp"], tabs["esel"], tabs["eg"]
    head_tile = pltpu.VMEM((NSA_HEADS * Q_BLOCK, KV_WIDTH), F32)
    nb, brow = NSA_HEADS // PIPE_HEADS, PIPE_HEADS * Q_BLOCK
    per_block = [pltpu.VMEM((brow, 1), F32)] * (2 * nb) + [pltpu.VMEM((brow, KV_WIDTH), F32)] * nb
    return pl.pallas_call(
        functools.partial(_nsa_prompt_kernel, voff_blk=tabs["voff"] // LANES),
        grid=(bsz, n_qb),
        in_specs=[qspec, pl.BlockSpec((Q_BLOCK, LG_PAD), lambda b, i: (b * n_qb + i, 0)),
                  cseq, cseq, seq, seq, seq, seq, full(gr),
                  pl.BlockSpec((1,) + bc.shape[1:], lambda b, i: (i, 0, 0, 0)), full(wt),
                  full(mimp), full(esel), full(eg)],
        out_specs=qspec,
        out_shape=jax.ShapeDtypeStruct((NSA_GROUP, bsz * t, LANES), F32),
        scratch_shapes=[pltpu.VMEM((t, KV_WIDTH), BF16), pltpu.VMEM((t, KV_WIDTH), BF16),
                        pltpu.VMEM((t + WINDOW, KV_WIDTH), BF16), pltpu.VMEM((t + WINDOW, KV_WIDTH), BF16),
                        pltpu.VMEM((n_ch, KV_WIDTH), BF16), pltpu.VMEM((n_ch, KV_WIDTH), BF16),
                        head_tile,
                        pltpu.VMEM((NSA_KV_HEADS, t // tk, Q_BLOCK, tk), F32),
                        pltpu.VMEM((NSA_HEADS * Q_BLOCK, KV_WIDTH), BF16)] + per_block,
        compiler_params=_params(("arbitrary", "arbitrary")),
        name="nsa_attention_prompt",
    )(q4, gates, kc, vc, ks, vs, kw, vw, gr, bc, wt, mimp, esel, eg)


def _nsa_sample_kernel(pt_ref, q_ref, gt_ref, kc_ref, vc_ref, ksn_ref, vsn_ref, kwn_ref, vwn_ref, bkw_ref, bvw_ref,
                       pks_hbm, pvs_hbm, gs_ref, gn_ref, bcs_ref, bws_ref, mimp_ref, esel_ref, eg_ref, o_ref,
                       bufk, bufv, sems, ksb, vsb, *, n_pages):
    sq = pl.program_id(0)
    n_seq = pl.num_programs(0)
    slot = sq % 2
    pools, bufs = (pks_hbm, pvs_hbm), (bufk, bufv)
    past = n_pages * PAGE_SIZE
    s_new = ksn_ref.shape[0]
    rows = NSA_GROUP * s_new
    n_cmp_rows = kc_ref.shape[0]
    n_sel = (past + s_new + SEL_BLOCK - 1) // SEL_BLOCK
    n_past_blk = past // SEL_BLOCK
    w_buf = bkw_ref.shape[1]

    @pl.when(sq == 0)
    def _():
        _paged_fetch(pt_ref, pools, bufs, sems, sq, slot, n_pages, True)

    @pl.when(sq + 1 < n_seq)
    def _():
        _paged_fetch(pt_ref, pools, bufs, sems, sq + 1, 1 - slot, n_pages, True)

    _paged_wait(bufs, sems, slot)
    ksb[...] = bufk[slot].astype(BF16)
    vsb[...] = bufv[slot].astype(BF16)

    lane = lax.broadcasted_iota(I32, (rows, LANES), 1)
    upper = lane >= HEAD_DIM
    qi = lax.broadcasted_iota(I32, (rows, 1), 0) % s_new
    pad_new = lambda ref: jnp.concatenate([ref[...], jnp.zeros((LANES - s_new, KV_WIDTH), F32)], axis=0).astype(BF16)
    ksn, vsn, kwn, vwn = pad_new(ksn_ref), pad_new(vsn_ref), pad_new(kwn_ref), pad_new(vwn_ref)
    kcb, vcb = kc_ref[...].astype(BF16), vc_ref[...].astype(BF16)
    bkw, bvw = bkw_ref[...].astype(BF16), bvw_ref[...].astype(BF16)
    new_causal = (lane < s_new) & (lane <= qi)

    def stacked_q(g):
        keep = upper if g == 1 else jnp.logical_not(upper)
        qs = jnp.concatenate([q_ref[r] for r in range(NSA_GROUP)], axis=0)
        return jnp.where(keep, qs, 0.0).astype(BF16)

    cend = lax.broadcasted_iota(I32, (rows, n_cmp_rows), 1) * CMP_STRIDE + (CMP_BLOCK - 1)
    valid_c = (past + qi) >= cend
    o_c, psums = [], []
    for g in range(NSA_KV_HEADS):
        s = _nt(stacked_q(g), kcb) + bcs_ref[g]
        e, den = _masked_softmax_parts(s, valid_c)
        p = e / den
        o_c.append(_dot(p.astype(BF16), vcb))
        ps = p[0:s_new]
        for r in range(1, NSA_GROUP):
            ps = ps + p[r * s_new:(r + 1) * s_new]
        psums.append(ps)
    imp = _dot(jnp.concatenate(psums, axis=0), mimp_ref[...], precision=HI)
    qpos_sel = past + lax.broadcasted_iota(I32, (NSA_KV_HEADS * s_new, 1), 0) % s_new
    sel = _select_blocks(_block_scores(imp, qpos_sel, n_sel), n_sel)
    mask_past = _dot(sel[:, :n_past_blk].astype(BF16), esel_ref[...])
    sel_new = sel[:, n_past_blk:n_past_blk + 1]

    gexp = _dot(gt_ref[...], eg_ref[...], precision=HI)
    o_s, o_w = [], []
    for g in range(NSA_KV_HEADS):
        qg = stacked_q(g)
        tile_rows = lambda a: jnp.concatenate([a[g * s_new:(g + 1) * s_new]] * NSA_GROUP, axis=0)
        valid_p = tile_rows(mask_past) > 0.5
        s_p = jnp.where(valid_p, _dot(qg, ksb[...]) + gs_ref[g], NEG_INF)
        valid_n = new_causal & (tile_rows(sel_new) > 0.5)
        s_n = jnp.where(valid_n, _nt(qg, ksn) + gn_ref[g], NEG_INF)
        m = jnp.maximum(jnp.max(s_p, axis=-1, keepdims=True), jnp.max(s_n, axis=-1, keepdims=True))
        e_p = jnp.where(valid_p, jnp.exp(s_p - m), 0.0)
        e_n = jnp.where(valid_n, jnp.exp(s_n - m), 0.0)
        den = jnp.maximum(jnp.sum(e_p, axis=-1, keepdims=True) + jnp.sum(e_n, axis=-1, keepdims=True), 1e-20)
        o_s.append((_nt(e_p.astype(BF16), vsb[...]) + _dot(e_n.astype(BF16), vsn)) / den)
        jb = lax.broadcasted_iota(I32, (rows, w_buf), 1)
        dist_b = w_buf + qi - jb
        valid_b = (dist_b < WINDOW) & (past - w_buf + jb >= 0)
        s_b = jnp.where(valid_b, _dot(qg, bkw) + bws_ref[g], NEG_INF)
        s_n = jnp.where(new_causal, _nt(qg, kwn) + gn_ref[g], NEG_INF)
        m = jnp.maximum(jnp.max(s_b, axis=-1, keepdims=True), jnp.max(s_n, axis=-1, keepdims=True))
        e_b = jnp.where(valid_b, jnp.exp(s_b - m), 0.0)
        e_n = jnp.where(new_causal, jnp.exp(s_n - m), 0.0)
        den = jnp.maximum(jnp.sum(e_b, axis=-1, keepdims=True) + jnp.sum(e_n, axis=-1, keepdims=True), 1e-20)
        o_w.append((_nt(e_b.astype(BF16), bvw) + _dot(e_n.astype(BF16), vwn)) / den)

    up8 = upper[0:s_new]
    for r in range(NSA_GROUP):
        pick = lambda o: jnp.where(up8, o[1][r * s_new:(r + 1) * s_new], o[0][r * s_new:(r + 1) * s_new])
        o_ref[r] = _gate_mix(gexp, pick(o_c), pick(o_s), pick(o_w), r)


def _nsa_sample(pt_flat, q4, gates, kc, vc, ksn, vsn, kwn, vwn, buf_kw, buf_vw, pool_ks, pool_vs, tabs,
                n_seq, s_new, n_pages):
    past = n_pages * PAGE_SIZE
    n_ch = past // CMP_STRIDE
    w_buf = buf_kw.shape[1]
    full = lambda a: pl.BlockSpec(a.shape, lambda s, pt: (0,) * a.ndim)
    hbm = pl.BlockSpec(memory_space=pl.ANY)
    rows = lambda n, w: pl.BlockSpec((n, w), lambda s, pt: (s, 0))
    qspec = pl.BlockSpec((NSA_GROUP, s_new, LANES), lambda s, pt: (0, s, 0))
    consts = (tabs["gs"], tabs["gn"], tabs["bcs"], tabs["bws"], tabs["mimp_s"], tabs["esel_s"], tabs["eg"])
    return pl.pallas_call(
        functools.partial(_nsa_sample_kernel, n_pages=n_pages),
        grid_spec=pltpu.PrefetchScalarGridSpec(
            num_scalar_prefetch=1,
            grid=(n_seq,),
            in_specs=[qspec, rows(s_new, LG_PAD), rows(n_ch, KV_WIDTH), rows(n_ch, KV_WIDTH)]
                     + [rows(s_new, KV_WIDTH)] * 4 + [rows(KV_WIDTH, w_buf)] * 2 + [hbm, hbm]
                     + [full(c) for c in consts],
            out_specs=qspec,
            scratch_shapes=[pltpu.VMEM((2, KV_WIDTH, past), F32), pltpu.VMEM((2, KV_WIDTH, past), F32),
                            pltpu.SemaphoreType.DMA((2, 2)),
                            pltpu.VMEM((KV_WIDTH, past), BF16), pltpu.VMEM((KV_WIDTH, past), BF16)],
        ),
        out_shape=jax.ShapeDtypeStruct((NSA_GROUP, n_seq * s_new, LANES), F32),
        compiler_params=_params(("arbitrary",)),
        name="nsa_attention_sample",
    )(pt_flat, q4, gates, kc, vc, ksn, vsn, kwn, vwn, buf_kw, buf_vw, pool_ks, pool_vs, *consts)


def _gla_kernel(q_ref, k_ref, v_ref, lg_ref, r_ref, gn_ref, s0_ref, o_ref, sfin_ref, s_scr):
    c = pl.program_id(1)
    n_c = pl.num_programs(1)
    cl = q_ref.shape[0]

    @pl.when(c == 0)
    def _():
        s_scr[...] = s0_ref[...]

    row_t = lax.broadcasted_iota(I32, (cl, LANES), 0)
    lane_t = lax.broadcasted_iota(I32, (cl, LANES), 1)
    causal = lax.broadcasted_iota(I32, (cl, cl), 1) <= lax.broadcasted_iota(I32, (cl, cl), 0)
    row_s = lax.broadcasted_iota(I32, (LANES, GLA_DV), 0)
    pair = LANES // GLA_DK
    for p in range(GLA_HEADS // pair):
        cols = slice(p * LANES, (p + 1) * LANES)
        lg = lg_ref[:, cols]
        b = lg
        sh = 1
        while sh < cl:
            b = b + jnp.where(row_t >= sh, pltpu.roll(b, sh, 0), 0.0)
            sh *= 2
        b_last = b[cl - 1:cl, :]
        qt = q_ref[:, cols] * jnp.exp(b)
        kp = k_ref[:, cols]
        kt = (kp * jnp.exp(-b)).astype(BF16)
        khat = (kp * jnp.exp(b_last - b)).astype(BF16)
        dec = jnp.exp(jnp.broadcast_to(b_last, (LANES, LANES))).T
        s_old = s_scr[cols, :]
        s_bf = s_old.astype(BF16)
        upd = jnp.zeros((LANES, GLA_DV), F32)
        for hh in range(pair):
            h = p * pair + hh
            vcols = slice(h * GLA_DV, (h + 1) * GLA_DV)
            mine = (lane_t >= GLA_DK) if hh == 1 else (lane_t < GLA_DK)
            qm = jnp.where(mine, qt, 0.0).astype(BF16)
            att = jnp.where(causal, _nt(qm, kt), 0.0)
            vh = v_ref[:, vcols].astype(BF16)
            o = _dot(qm, s_bf) + _dot(att.astype(BF16), vh)
            o = o * lax.rsqrt(jnp.mean(o * o, axis=-1, keepdims=True) + RMS_EPS) * gn_ref[...]
            rh = r_ref[:, vcols]
            o_ref[:, vcols] = o * (rh * jax.nn.sigmoid(rh))
            u = _tn(khat, vh)
            upd = jnp.where((row_s >= GLA_DK) == (hh == 1), u, upd)
        s_scr[cols, :] = s_old * dec + upd

    @pl.when(c == n_c - 1)
    def _():
        sfin_ref[...] = s_scr[...]


def _gla(q_l, k_l, v_l, lg, r, g_norm, s0, bsz, n_c, cl):
    srows = GLA_HEADS * GLA_DK
    blk = lambda w: pl.BlockSpec((cl, w), lambda b, c: (b * n_c + c, 0))
    st = pl.BlockSpec((srows, GLA_DV), lambda b, c: (b, 0))
    return pl.pallas_call(
        _gla_kernel,
        grid=(bsz, n_c),
        in_specs=[blk(GLA_K_WIDTH), blk(GLA_K_WIDTH), blk(GLA_V_WIDTH), blk(GLA_K_WIDTH), blk(GLA_V_WIDTH),
                  pl.BlockSpec(g_norm.shape, lambda b, c: (0, 0)), st],
        out_specs=[blk(GLA_V_WIDTH), st],
        out_shape=[jax.ShapeDtypeStruct((bsz * n_c * cl, GLA_V_WIDTH), F32),
                   jax.ShapeDtypeStruct((bsz * srows, GLA_DV), F32)],
        scratch_shapes=[pltpu.VMEM((srows, GLA_DV), F32)],
        compiler_params=_params(("arbitrary", "arbitrary")),
        name="gla",
    )(q_l, k_l, v_l, lg, r, g_norm, s0)


def _merge_kernel(x_ref, on_ref, og_ref, ma_ref, mb_ref, wn_ref, wg_ref, wo_ref, gf_ref, rw_ref, rb_ref,
                  x1_o, h2t_o, ei_o, gw_o, rk_o, cnt_o, carry):
    i = pl.program_id(0)
    tm = x_ref.shape[0]

    @pl.when(i == 0)
    def _():
        carry[...] = jnp.zeros(carry.shape, F32)

    on = jnp.concatenate([on_ref[r] for r in range(NSA_GROUP)], axis=1).astype(BF16)
    ya = _dot(on, wn_ref[...])
    yb = _dot(og_ref[...].astype(BF16), wg_ref[...])
    m = jax.nn.sigmoid(ma_ref[...]) * ya + jax.nn.sigmoid(mb_ref[...]) * yb
    x1 = x_ref[...] + _dot(m.astype(BF16), wo_ref[...])
    x1_o[...] = x1
    h2 = x1 * lax.rsqrt(jnp.mean(x1 * x1, axis=-1, keepdims=True) + RMS_EPS) * gf_ref[...]
    for s in range(D_MODEL // LANES):
        h2t_o[pl.ds(s, tm, stride=SUBLANES), :] = h2[:, s * LANES:(s + 1) * LANES]

    logits = _dot(h2, rw_ref[...], precision=HI) + rb_ref[...]
    lane = lax.broadcasted_iota(I32, (tm, LANES), 1)
    lane_f = lane.astype(F32)
    work = logits
    vals, idxs = [], []
    for _ in range(TOP_K):
        mk = jnp.max(work, axis=-1, keepdims=True)
        ik = jnp.min(jnp.where(work == mk, lane_f, float(LANES)), axis=-1, keepdims=True)
        vals.append(mk)
        idxs.append(ik)
        work = jnp.where(lane_f == ik, -jnp.inf, work)
    es = [jnp.exp(v - vals[0]) for v in vals]
    den = es[0]
    for e in es[1:]:
        den = den + e
    onehot = jnp.zeros((tm, LANES), F32)
    for ik in idxs:
        onehot = onehot + jnp.where(lane_f == ik, 1.0, 0.0)
    below = lax.broadcasted_iota(I32, (tm, tm), 1) < lax.broadcasted_iota(I32, (tm, tm), 0)
    before = _dot(jnp.where(below, 1.0, 0.0).astype(BF16), onehot.astype(BF16)) + carry[0:1, :]
    ei = jnp.zeros((tm, LANES), F32)
    gw = jnp.zeros((tm, LANES), F32)
    rk = jnp.zeros((tm, LANES), F32)
    for k in range(TOP_K):
        rank_k = jnp.sum(jnp.where(lane_f == idxs[k], before, 0.0), axis=-1, keepdims=True)
        ei = jnp.where(lane == k, idxs[k], ei)
        gw = jnp.where(lane == k, es[k] / den, gw)
        rk = jnp.where(lane == k, rank_k, rk)
    ei_o[...] = ei.astype(I32)
    gw_o[...] = gw
    rk_o[...] = rk.astype(I32)
    carry[0:1, :] = carry[0:1, :] + jnp.sum(onehot, axis=0, keepdims=True)
    cnt_o[...] = carry[...]


def _merge(x, o_nsa4, o_gla, m_a, m_b, mw):
    n = x.shape[0]
    tm = TM_PROJ
    row = lambda w: pl.BlockSpec((tm, w), lambda i: (i, 0))
    full = lambda a: pl.BlockSpec(a.shape, lambda i: (0,) * a.ndim)
    ws = (mw["wn"], mw["wg"], mw["wo"], mw["gf"], mw["rw"], mw["rb"])
    return pl.pallas_call(
        _merge_kernel,
        grid=(n // tm,),
        in_specs=[row(D_MODEL), pl.BlockSpec((NSA_GROUP, tm, LANES), lambda i: (0, i, 0)), row(GLA_V_WIDTH),
                  row(D_MODEL), row(D_MODEL)] + [full(w) for w in ws],
        out_specs=[row(D_MODEL), pl.BlockSpec((tm * SUBLANES, LANES), lambda i: (i, 0)),
                   row(LANES), row(LANES), row(LANES), pl.BlockSpec((SUBLANES, LANES), lambda i: (0, 0))],
        out_shape=[jax.ShapeDtypeStruct((n, D_MODEL), F32), jax.ShapeDtypeStruct((n * SUBLANES, LANES), F32),
                   jax.ShapeDtypeStruct((n, LANES), I32), jax.ShapeDtypeStruct((n, LANES), F32),
                   jax.ShapeDtypeStruct((n, LANES), I32), jax.ShapeDtypeStruct((SUBLANES, LANES), F32)],
        scratch_shapes=[pltpu.VMEM((SUBLANES, LANES), F32)],
        compiler_params=_params(("arbitrary",)),
        name="merge_router",
    )(x, o_nsa4, o_gla, m_a, m_b, *ws)


def _token_tile(ref, row):
    return ref.at[pl.ds(row * SUBLANES, SUBLANES)]


def _dispatch_kernel(meta_ref, dest_ref, h_ref, xs_hbm, dsm, ztile, sem_idx, sem_row, sem_pad, *, td):
    i = pl.program_id(0)
    n_steps = pl.num_programs(0)
    idx_copy = pltpu.make_async_copy(dest_ref, dsm, sem_idx)
    idx_copy.start()
    idx_copy.wait()

    per_row = LANES // TOP_K

    def body(rr, _):
        for c in range(LANES):
            tk = rr * per_row + c // TOP_K
            pltpu.make_async_copy(_token_tile(h_ref, tk), _token_tile(xs_hbm, dsm[rr, c]), sem_row).start()
        return 0
    lax.fori_loop(0, td // per_row, body, 0)

    @pl.when(i == n_steps - 1)
    def _():
        ztile[...] = jnp.zeros(ztile.shape, F32)
        blk_rows = ztile.shape[0]
        n_blocks = xs_hbm.shape[0] // blk_rows

        def pads(start_or_wait):
            def per_expert(e, _):
                first = meta_ref[N_EXPERTS + e] + meta_ref[e]
                last = meta_ref[N_EXPERTS + e] + meta_ref[2 * N_EXPERTS + e]

                def per_row(rw, _):
                    cp = pltpu.make_async_copy(ztile.at[pl.ds(0, SUBLANES)], _token_tile(xs_hbm, rw), sem_pad)
                    cp.start() if start_or_wait else cp.wait()
                    return 0
                lax.fori_loop(first, last, per_row, 0)
                return 0
            lax.fori_loop(0, N_EXPERTS, per_expert, 0)

            def per_block(bk, _):
                cp = pltpu.make_async_copy(ztile, xs_hbm.at[pl.ds(bk * blk_rows, blk_rows)], sem_pad)
                cp.start() if start_or_wait else cp.wait()
                return 0
            lax.fori_loop(meta_ref[3 * N_EXPERTS], n_blocks, per_block, 0)
        pads(True)
        pads(False)

    for _ in range(TOP_K):
        pltpu.make_async_copy(h_ref, xs_hbm.at[pl.ds(0, td * SUBLANES)], sem_row).wait()


def _dispatch(meta, dest2d, h2t, n_rows_total, td):
    n = h2t.shape[0] // SUBLANES
    hbm = pl.BlockSpec(memory_space=pl.ANY)
    drows = td * TOP_K // LANES
    return pl.pallas_call(
        functools.partial(_dispatch_kernel, td=td),
        grid_spec=pltpu.PrefetchScalarGridSpec(
            num_scalar_prefetch=1,
            grid=(n // td,),
            in_specs=[pl.BlockSpec((drows, LANES), lambda i, m: (i, 0)),
                      pl.BlockSpec((td * SUBLANES, LANES), lambda i, m: (i, 0))],
            out_specs=hbm,
            scratch_shapes=[pltpu.SMEM((drows, LANES), I32), pltpu.VMEM((MOE_ROWS * SUBLANES, LANES), F32),
                            pltpu.SemaphoreType.DMA, pltpu.SemaphoreType.DMA, pltpu.SemaphoreType.DMA],
        ),
        out_shape=jax.ShapeDtypeStruct((n_rows_total * SUBLANES, LANES), F32),
        compiler_params=_params(("arbitrary",)),
        name="moe_dispatch",
    )(meta, dest2d, h2t)


def _moe_kernel(be_ref, nu_ref, xs_ref, wg_ref, bg_ref, wu_ref, bu_ref, wd_ref, bd_ref, y_ref):
    i = pl.program_id(0)
    rows = xs_ref.shape[0] // SUBLANES
    n_s = D_MODEL // LANES

    @pl.when(i < nu_ref[0])
    def _():
        xb = jnp.concatenate([xs_ref[pl.ds(s, rows, stride=SUBLANES), :] for s in range(n_s)], axis=1).astype(BF16)
        g = _dot(xb, wg_ref[0]) + bg_ref[0]
        u = _dot(xb, wu_ref[0]) + bu_ref[0]
        g = jnp.minimum(g, SWIGLU_LIMIT)
        u = jnp.clip(u, -SWIGLU_LIMIT, SWIGLU_LIMIT)
        hh = (u + 1.0) * (g * jax.nn.sigmoid(SWIGLU_ALPHA * g))
        y = _dot(hh.astype(BF16), wd_ref[0]) + bd_ref[0]
        for s in range(n_s):
            y_ref[pl.ds(s, rows, stride=SUBLANES), :] = y[:, s * LANES:(s + 1) * LANES]

    @pl.when(i >= nu_ref[0])
    def _():
        y_ref[...] = jnp.zeros(y_ref.shape, F32)


def _moe_experts(block_e, n_used, xs, ew, n_blocks):
    blk = lambda i, be, nu: jnp.minimum(i, nu[0] - 1)
    rows = pl.BlockSpec((MOE_ROWS * SUBLANES, LANES), lambda i, be, nu: (i, 0))
    wspec = lambda a: pl.BlockSpec((1,) + a.shape[1:], lambda i, be, nu: (be[blk(i, be, nu)], 0, 0))
    ws = (ew["wg"], ew["bg"], ew["wu"], ew["bu"], ew["wd"], ew["bd"])
    return pl.pallas_call(
        _moe_kernel,
        grid_spec=pltpu.PrefetchScalarGridSpec(
            num_scalar_prefetch=2,
            grid=(n_blocks,),
            in_specs=[rows] + [wspec(w) for w in ws],
            out_specs=rows,
        ),
        out_shape=jax.ShapeDtypeStruct(xs.shape, F32),
        compiler_params=_params(("arbitrary",)),
        name="moe_experts",
    )(block_e, n_used, xs, *ws)


def _combine_kernel(dest_ref, gw_ref, x1_ref, gfin_ref, y_hbm, out_ref, dsm, buf, sem_idx, sem_row):
    tc = x1_ref.shape[0]
    idx_copy = pltpu.make_async_copy(dest_ref, dsm, sem_idx)
    idx_copy.start()
    idx_copy.wait()

    per_row = LANES // TOP_K

    def body(rr, _):
        for c in range(LANES):
            tk = rr * per_row + c // TOP_K
            pltpu.make_async_copy(_token_tile(y_hbm, dsm[rr, c]), _token_tile(buf.at[c % TOP_K], tk), sem_row).start()
        return 0
    lax.fori_loop(0, tc // per_row, body, 0)
    for k in range(TOP_K):
        pltpu.make_async_copy(y_hbm.at[pl.ds(0, tc * SUBLANES)], buf.at[k], sem_row).wait()

    gw = gw_ref[...]
    parts = []
    for s in range(D_MODEL // LANES):
        acc = None
        for k in range(TOP_K):
            term = buf[k, pl.ds(s, tc, stride=SUBLANES), :] * gw[:, k:k + 1]
            acc = term if acc is None else acc + term
        parts.append(acc)
    x2 = x1_ref[...] + jnp.concatenate(parts, axis=1)
    out_ref[...] = x2 * lax.rsqrt(jnp.mean(x2 * x2, axis=-1, keepdims=True) + RMS_EPS) * gfin_ref[...]


def _combine(dest2d, gw, x1, g_final, y_rows):
    n = x1.shape[0]
    tc = TC_COMBINE
    drows = tc * TOP_K // LANES
    row = lambda w: pl.BlockSpec((tc, w), lambda i: (i, 0))
    return pl.pallas_call(
        _combine_kernel,
        grid=(n // tc,),
        in_specs=[pl.BlockSpec((drows, LANES), lambda i: (i, 0)), row(LANES), row(D_MODEL),
                  pl.BlockSpec(g_final.shape, lambda i: (0, 0)), pl.BlockSpec(memory_space=pl.ANY)],
        out_specs=row(D_MODEL),
        out_shape=jax.ShapeDtypeStruct((n, D_MODEL), F32),
        scratch_shapes=[pltpu.SMEM((drows, LANES), I32), pltpu.VMEM((TOP_K, tc * SUBLANES, LANES), F32),
                        pltpu.SemaphoreType.DMA, pltpu.SemaphoreType.DMA],
        compiler_params=_params(("arbitrary",)),
        name="moe_combine",
    )(dest2d, gw, x1, g_final, y_rows)


def _bucket_table(max_dist):
    n = np.arange(max_dist, dtype=np.int64)
    scaled = np.log(np.maximum(n, 1).astype(np.float64) / REL_EXACT) / math.log(REL_MAX_DIST / REL_EXACT)
    large = REL_EXACT + (scaled * (REL_BUCKETS - REL_EXACT)).astype(np.int64)
    return np.where(n < REL_EXACT, n, np.minimum(large, REL_BUCKETS - 1)).astype(np.int32)


def _bias_lookup(rel_bias, dist):
    d = np.maximum(dist, 0)
    buckets = _bucket_table(int(d.max()) + 1)[d]
    return jnp.take(rel_bias.astype(F32).T, jnp.asarray(buckets), axis=1)


def _skew(w, n_rows, step, width):
    h, l = w.shape
    flat = jnp.tile(w, (1, n_rows))[:, :n_rows * (l - step)]
    return flat.reshape(h, n_rows, l - step)[:, :, :width]


def _importance_matrix(n_rows, n_cmp, n_sel, n_cols):
    rc = CMP_BLOCK // CMP_STRIDE
    rs = SEL_BLOCK // CMP_STRIDE
    m = np.zeros((n_rows, n_cols), np.float32)
    for j in range(n_sel):
        for o in range(rs + rc - 1):
            w = min(o - (rc - 1) + rc, rs) - max(o - (rc - 1), 0)
            c = rs * j + o - (rc - 1)
            if 0 <= c < n_cmp:
                m[c, j] += w
    return jnp.asarray(m)


def _block_expand(n_blocks, n_rows=LANES):
    e = np.zeros((n_rows, n_blocks * SEL_BLOCK), np.float32)
    for j in range(n_blocks):
        e[j, j * SEL_BLOCK:(j + 1) * SEL_BLOCK] = 1.0
    return jnp.asarray(e, dtype=BF16)


def _gate_expand():
    e = np.zeros((LG_PAD, 3 * NSA_WIDTH), np.float32)
    for g in range(NSA_KV_HEADS):
        for r in range(NSA_GROUP):
            for c in range(3):
                lo = c * NSA_WIDTH + r * LANES + g * HEAD_DIM
                e[(g * NSA_GROUP + r) * 3 + c, lo:lo + HEAD_DIM] = 1.0
    return jnp.asarray(e)


def _prompt_tables(rel_bias, t):
    n_qb = t // Q_BLOCK
    n_ch = t // CMP_STRIDE
    tk = min(SEL_CHUNK, t)
    voff = max(t - Q_BLOCK, WINDOW)
    nv = voff // LANES + tk // LANES
    wd = nv * LANES
    wv = _bias_lookup(rel_bias, np.concatenate([voff - np.arange(wd), voff + np.arange(Q_BLOCK, 0, -1)]))
    gr = _skew(wv, Q_BLOCK, 1, wd).reshape(NSA_HEADS, Q_BLOCK, nv, LANES).transpose(2, 0, 1, 3)
    wc = _bias_lookup(rel_bias, np.concatenate([np.arange(t) - (CMP_BLOCK - 1), np.zeros(CMP_STRIDE * n_ch, np.int64)]))
    cend = np.arange(n_ch) * CMP_STRIDE + (CMP_BLOCK - 1)
    cmask = np.where(np.arange(t)[:, None] >= cend[None, :], 0.0, NEG_INF).astype(np.float32)
    bc = _skew(wc, n_ch, CMP_STRIDE, t).transpose(0, 2, 1) + cmask[None]
    bc = bc.reshape(NSA_HEADS, n_qb, Q_BLOCK, n_ch).transpose(1, 0, 2, 3)
    n_w = WINDOW + Q_BLOCK
    ww = _bias_lookup(rel_bias, np.concatenate([WINDOW - np.arange(n_w), WINDOW + np.arange(Q_BLOCK, 0, -1)]))
    dist_w = WINDOW + np.arange(Q_BLOCK)[:, None] - np.arange(n_w)[None, :]
    wmask = np.where((dist_w >= 0) & (dist_w < WINDOW), 0.0, NEG_INF).astype(np.float32)
    wt = _skew(ww, Q_BLOCK, 1, n_w) + wmask[None]
    return dict(gr=gr, bc=bc, wt=wt, voff=voff,
                mimp=_importance_matrix(n_ch, n_ch - 1, t // SEL_BLOCK, LANES),
                esel=_block_expand(t // SEL_BLOCK), eg=_gate_expand())


def _sample_tables(rel_bias, past, s_new, w_buf):
    n_ch = past // CMP_STRIDE
    n_sel = (past + s_new + SEL_BLOCK - 1) // SEL_BLOCK
    per_group = lambda a: a.reshape(NSA_KV_HEADS, NSA_GROUP, s_new, -1).reshape(NSA_KV_HEADS, NSA_GROUP * s_new, -1)
    qi = np.arange(s_new)

    def table(dist):
        return per_group(_bias_lookup(rel_bias, dist))
    ws = _bias_lookup(rel_bias, np.concatenate([past - np.arange(past), past + np.arange(s_new, 0, -1)]))
    gs = per_group(_skew(ws, s_new, 1, past))
    jn = np.arange(LANES)
    gn = table(np.where(jn[None, :] < s_new, qi[:, None] - jn[None, :], 0))
    cend = np.arange(n_ch) * CMP_STRIDE + (CMP_BLOCK - 1)
    bcs = table(past + qi[:, None] - cend[None, :])
    bws = table(w_buf + qi[:, None] - np.arange(w_buf)[None, :])
    sel_lanes = -(-n_sel // LANES) * LANES
    return dict(gs=gs, gn=gn, bcs=bcs, bws=bws,
                mimp_s=_importance_matrix(n_ch, n_ch - 1, n_sel, sel_lanes),
                esel_s=_block_expand(past // SEL_BLOCK, past // SEL_BLOCK), eg=_gate_expand())


def _compress_weights(pe, w1, w2):
    rc = CMP_BLOCK // CMP_STRIDE
    w1r = w1.reshape(rc, CMP_STRIDE, HEAD_DIM, CMP_HIDDEN)
    eye = jnp.eye(NSA_KV_HEADS, dtype=w1.dtype)
    wcat = jnp.einsum("rldh,ge->lgdreh", w1r, eye).reshape(CMP_STRIDE, KV_WIDTH, rc * NSA_KV_HEADS * CMP_HIDDEN)
    w2bd = jnp.einsum("hd,ge->ghed", w2, eye).reshape(NSA_KV_HEADS * CMP_HIDDEN, KV_WIDTH)
    pe_rows = jnp.concatenate([pe.reshape(1, CMP_BLOCK * HEAD_DIM),
                               jnp.zeros((SUBLANES - 1, CMP_BLOCK * HEAD_DIM), pe.dtype)], axis=0)
    return wcat.astype(BF16), pe_rows, w1.reshape(CMP_BLOCK * HEAD_DIM, CMP_HIDDEN), w2bd.astype(BF16)


def _layer_weights(w_in, gla_w_alpha, gla_b_alpha, w_branch_nsa, norm_ffn, router_w, router_b):
    offs = np.cumsum((0,) + IN_SPLITS)
    col = lambda j: w_in[:, offs[j]:offs[j + 1]]
    q_perm = col(0).reshape(D_MODEL, NSA_KV_HEADS, NSA_GROUP, HEAD_DIM).transpose(0, 2, 1, 3).reshape(D_MODEL, NSA_WIDTH)
    pad = jnp.zeros((D_MODEL, LG_PAD - 3 * NSA_HEADS - GLA_RANK), w_in.dtype)
    w_p = jnp.concatenate([q_perm, col(1), col(3), col(4), col(5), col(7), col(8), col(9), col(2), col(6), pad],
                          axis=1).astype(BF16)
    w_al = jnp.zeros((LG_PAD, GLA_K_WIDTH), F32).at[3 * NSA_HEADS:3 * NSA_HEADS + GLA_RANK].set(gla_w_alpha)
    wn = w_branch_nsa.reshape(NSA_KV_HEADS, NSA_GROUP, HEAD_DIM, D_MODEL).transpose(1, 0, 2, 3).reshape(NSA_WIDTH, D_MODEL)
    rw = jnp.concatenate([router_w, jnp.zeros((D_MODEL, LANES - N_EXPERTS), F32)], axis=1)
    rb = jnp.concatenate([router_b, jnp.full((LANES - N_EXPERTS,), NEG_INF, F32)]).reshape(1, LANES)
    return w_p, w_al, gla_b_alpha.reshape(1, GLA_K_WIDTH), wn.astype(BF16), norm_ffn.reshape(1, D_MODEL), rw, rb


def _moe(h2t, ei, gw, rk, counts, x1, g_final, ew):
    n = x1.shape[0]
    nk = n * TOP_K
    counts = counts[0, :N_EXPERTS].astype(I32)
    padded = (counts + MOE_ROWS - 1) // MOE_ROWS * MOE_ROWS
    pends = jnp.cumsum(padded)
    pstarts = pends - padded
    n_blocks = (nk + N_EXPERTS * (MOE_ROWS - 1) + MOE_ROWS - 1) // MOE_ROWS
    blk_start = jnp.arange(n_blocks, dtype=I32) * MOE_ROWS
    block_e = jnp.minimum(jnp.sum((pends[None, :] <= blk_start[:, None]).astype(I32), axis=1), N_EXPERTS - 1)
    n_used = (pends[-1] // MOE_ROWS).astype(I32).reshape(1)
    e_sel = ei[:, :TOP_K, None] == jnp.arange(N_EXPERTS, dtype=I32)
    dest = jnp.sum(jnp.where(e_sel, pstarts.astype(I32), 0), axis=-1) + rk[:, :TOP_K]
    dest2d = dest.reshape(nk // LANES, LANES)
    meta = jnp.concatenate([counts, pstarts, padded, n_used]).astype(I32)
    td = min(TD_DISPATCH, n)
    xs = _dispatch(meta, dest2d, h2t, n_blocks * MOE_ROWS, td)
    y_rows = _moe_experts(block_e, n_used, xs, ew, n_blocks)
    return _combine(dest2d, gw, x1, g_final, y_rows)


def kernel(x_prompt, x_sample, cache_cmp_k, cache_cmp_v, cache_sel_k, cache_sel_v, state_win_k, state_win_v, state_gla, page_table, rel_bias, norm_mix, w_in, nsa_pe_k, nsa_pe_v, nsa_w1_k, nsa_w1_v, nsa_w2_k, nsa_w2_v, gla_w_alpha, gla_b_alpha, gla_norm, w_branch_nsa, w_branch_gla, w_out, norm_ffn, router_w, router_b, exp_w_gate, exp_b_gate, exp_w_up, exp_b_up, exp_w_down, exp_b_down, norm_final):
    depth = w_in.shape[0]
    assert depth == 1, "single-layer trunk"
    bsz, t, d = x_prompt.shape
    n_seq, s_new, _ = x_sample.shape
    n_pages = page_table.shape[1]
    past = n_pages * PAGE_SIZE
    w_buf = state_win_k.shape[2]
    assert d == D_MODEL and t % Q_BLOCK == 0 and t % GLA_CHUNK == 0

    w_p, w_al, b_al, wn, gf, rw, rb = _layer_weights(w_in[0], gla_w_alpha[0], gla_b_alpha[0], w_branch_nsa[0],
                                                     norm_ffn[0], router_w[0], router_b[0])
    g_mix = norm_mix[0].reshape(1, D_MODEL)
    g_fin = norm_final.reshape(1, D_MODEL)
    g_gla = gla_norm[0].reshape(1, GLA_DV)
    mw = dict(wn=wn, wg=w_branch_gla[0].astype(BF16), wo=w_out[0].astype(BF16), gf=gf, rw=rw, rb=rb)
    ew = dict(wg=exp_w_gate[0].astype(BF16), bg=exp_b_gate[0].reshape(N_EXPERTS, 1, D_FF),
              wu=exp_w_up[0].astype(BF16), bu=exp_b_up[0].reshape(N_EXPERTS, 1, D_FF),
              wd=exp_w_down[0].astype(BF16), bd=exp_b_down[0].reshape(N_EXPERTS, 1, D_MODEL))
    cw = {}
    for nm, pe, w1, w2 in (("k", nsa_pe_k[0], nsa_w1_k[0], nsa_w2_k[0]), ("v", nsa_pe_v[0], nsa_w1_v[0], nsa_w2_v[0])):
        cw["wcat_" + nm], cw["pe_" + nm], cw["w1f_" + nm], cw["w2_" + nm] = _compress_weights(pe, w1, w2)

    xp = x_prompt.reshape(bsz * t, d)
    (q4, kck, kcv, ksk, ksv, kwk, kwv, gates, q_l, k_l, v_l, lg, r_l, m_a, m_b) = _in_projection(xp, g_mix, w_p, w_al, b_al)
    kc, vc = _compress_prompt(kck, kcv, cw, bsz, t)
    o_nsa = _nsa_prompt(q4, gates, kc, vc, ksk, ksv, kwk, kwv, _prompt_tables(rel_bias, t), bsz, t)
    s_zero = jnp.zeros((bsz * GLA_HEADS * GLA_DK, GLA_DV), F32)
    o_gla, p_gla = _gla(q_l, k_l, v_l, lg, r_l, g_gla, s_zero, bsz, t // GLA_CHUNK, GLA_CHUNK)
    x1, h2t, ei, gw, rk, counts = _merge(xp, o_nsa, o_gla, m_a, m_b, mw)
    y_prompt = _moe(h2t, ei, gw, rk, counts, x1, g_fin, ew).reshape(bsz, t, d)

    kv_shape = (1, bsz, t, NSA_KV_HEADS, HEAD_DIM)
    w_len = min(WINDOW, t)
    win = lambda a: a.reshape(kv_shape)[:, :, t - w_len:]
    p_states = (kck.reshape(kv_shape), kcv.reshape(kv_shape), ksk.reshape(kv_shape), ksv.reshape(kv_shape),
                win(kwk), win(kwv), p_gla.reshape(1, bsz, GLA_HEADS, GLA_DK, GLA_DV))

    xs = x_sample.reshape(n_seq * s_new, d)
    (q4, kck, kcv, ksk, ksv, kwk, kwv, gates, q_l, k_l, v_l, lg, r_l, m_a, m_b) = _in_projection(xs, g_mix, w_p, w_al, b_al)
    pt_flat = page_table.reshape(n_seq * n_pages).astype(I32)
    pool = lambda c: c[0].transpose(0, 2, 3, 1).reshape(-1, PAGE_SIZE)
    kc, vc = _compress_sample(pt_flat, pool(cache_cmp_k), pool(cache_cmp_v), cw, n_seq, n_pages)
    buf_kw = state_win_k[0].transpose(0, 2, 3, 1).reshape(n_seq * KV_WIDTH, w_buf)
    buf_vw = state_win_v[0].transpose(0, 2, 3, 1).reshape(n_seq * KV_WIDTH, w_buf)
    o_nsa = _nsa_sample(pt_flat, q4, gates, kc, vc, ksk, ksv, kwk, kwv, buf_kw, buf_vw,
                        pool(cache_sel_k), pool(cache_sel_v), _sample_tables(rel_bias, past, s_new, w_buf),
                        n_seq, s_new, n_pages)
    cl = 16
    padc = lambda a: jnp.pad(a.reshape(n_seq, s_new, -1), ((0, 0), (0, cl - s_new), (0, 0))).reshape(n_seq * cl, -1)
    s_in = state_gla[0].reshape(n_seq * GLA_HEADS * GLA_DK, GLA_DV)
    o_gla, s_gla = _gla(padc(q_l), padc(k_l), padc(v_l), padc(lg), padc(r_l), g_gla, s_in, n_seq, 1, cl)
    o_gla = o_gla.reshape(n_seq, cl, GLA_V_WIDTH)[:, :s_new].reshape(n_seq * s_new, GLA_V_WIDTH)
    x1, h2t, ei, gw, rk, counts = _merge(xs, o_nsa, o_gla, m_a, m_b, mw)
    y_sample = _moe(h2t, ei, gw, rk, counts, x1, g_fin, ew).reshape(n_seq, s_new, d)

    kvs = (1, n_seq, s_new, NSA_KV_HEADS, HEAD_DIM)
    new_win = lambda buf, new: jnp.concatenate([buf, new.reshape(kvs).astype(buf.dtype)], axis=2)[:, :, s_new:]
    s_states = (kck.reshape(kvs), kcv.reshape(kvs), ksk.reshape(kvs), ksv.reshape(kvs),
                new_win(state_win_k, kwk), new_win(state_win_v, kwv),
                s_gla.reshape(1, n_seq, GLA_HEADS, GLA_DK, GLA_DV))
    return (y_prompt, y_sample) + p_states + s_states
```

```python
import functools
import math

import numpy as np
import jax
import jax.numpy as jnp
from jax import lax
from jax.experimental import pallas as pl
from jax.experimental.pallas import tpu as pltpu

F32 = jnp.float32
BF16 = jnp.bfloat16
I32 = jnp.int32
HI = lax.Precision.HIGHEST

D_MODEL = 1024
PAGE_SIZE = 128
NSA_HEADS = 8
NSA_KV_HEADS = 2
NSA_GROUP = NSA_HEADS // NSA_KV_HEADS
HEAD_DIM = 64
NSA_WIDTH = NSA_HEADS * HEAD_DIM
KV_WIDTH = NSA_KV_HEADS * HEAD_DIM
CMP_BLOCK = 32
CMP_STRIDE = 16
CMP_HIDDEN = 2 * HEAD_DIM
SEL_BLOCK = 64
N_SEL = 16
WINDOW = 512
Q_BLOCK = 128
SEL_FORCE = 1e9
GLA_HEADS = 4
GLA_DK = 64
GLA_DV = 128
GLA_K_WIDTH = GLA_HEADS * GLA_DK
GLA_V_WIDTH = GLA_HEADS * GLA_DV
GLA_RANK = 16
GLA_TAU = 16.0
GLA_CHUNK = 64
N_EXPERTS = 32
TOP_K = 4
D_FF = D_MODEL
SWIGLU_ALPHA = 1.702
SWIGLU_LIMIT = 7.0
REL_BUCKETS = 32
REL_EXACT = REL_BUCKETS // 2
REL_MAX_DIST = 1024
RMS_EPS = 1e-6
NEG_INF = -1e30
IN_SPLITS = (NSA_WIDTH, 6 * KV_WIDTH, 3 * NSA_HEADS, GLA_K_WIDTH, GLA_K_WIDTH, GLA_V_WIDTH, GLA_RANK,
             GLA_V_WIDTH, D_MODEL, D_MODEL)

LANES = 128
SUBLANES = 8
VMEM_LIMIT = 56 * 1024 * 1024

TM_PROJ = 256
MOE_ROWS = 256
TD_DISPATCH = 512
TC_COMBINE = 256
SEL_CHUNK = 512
PIPE_HEADS = 2
LG_PAD = LANES


def _nt(a, b, **kw):
    return lax.dot_general(a, b, (((1,), (1,)), ((), ())), preferred_element_type=F32, **kw)


def _tn(a, b, **kw):
    return lax.dot_general(a, b, (((0,), (0,)), ((), ())), preferred_element_type=F32, **kw)


def _dot(a, b, **kw):
    return jnp.dot(a, b, preferred_element_type=F32, **kw)


def _params(sem, vmem=VMEM_LIMIT):
    return pltpu.CompilerParams(dimension_semantics=sem, vmem_limit_bytes=vmem)


def _masked_softmax_parts(s, valid):
    s = jnp.where(valid, s, NEG_INF)
    m = jnp.max(s, axis=-1, keepdims=True)
    e = jnp.where(valid, jnp.exp(s - m), 0.0)
    return e, jnp.maximum(jnp.sum(e, axis=-1, keepdims=True), 1e-20)


_OFF_Q = 0
_OFF_KV = _OFF_Q + NSA_WIDTH
_OFF_QL = _OFF_KV + 6 * KV_WIDTH
_OFF_KL = _OFF_QL + GLA_K_WIDTH
_OFF_VL = _OFF_KL + GLA_K_WIDTH
_OFF_R = _OFF_VL + GLA_V_WIDTH
_OFF_MA = _OFF_R + GLA_V_WIDTH
_OFF_MB = _OFF_MA + D_MODEL
_OFF_GA = _OFF_MB + D_MODEL
_N_PROJ = _OFF_GA + LG_PAD


def _inproj_kernel(x_ref, g_ref, w_ref, wal_ref, bal_ref,
                   q_o, kck_o, kcv_o, ksk_o, ksv_o, kwk_o, kwv_o, gt_o, ql_o, kl_o, vl_o, lg_o, r_o, ma_o, mb_o,
                   *kv_t_o):
    x = x_ref[...]
    xn = x * lax.rsqrt(jnp.mean(x * x, axis=-1, keepdims=True) + RMS_EPS)
    xn = (xn * g_ref[...]).astype(BF16)

    def mm(lo, n):
        return _dot(xn, w_ref[:, lo:lo + n])

    q = mm(_OFF_Q, NSA_WIDTH) * (HEAD_DIM ** -0.5)
    for r in range(NSA_GROUP):
        q_o[r] = q[:, r * LANES:(r + 1) * LANES]
    kv = mm(_OFF_KV, 6 * KV_WIDTH)
    for j, o in enumerate((kck_o, kcv_o, ksk_o, ksv_o, kwk_o, kwv_o)):
        o[...] = kv[:, j * KV_WIDTH:(j + 1) * KV_WIDTH]
    for j, o in enumerate(kv_t_o):
        o[...] = kv[:, j * KV_WIDTH:(j + 1) * KV_WIDTH].T
    ql_o[...] = mm(_OFF_QL, GLA_K_WIDTH) * (GLA_DK ** -0.5)
    kl_o[...] = mm(_OFF_KL, GLA_K_WIDTH)
    vl_o[...] = mm(_OFF_VL, GLA_V_WIDTH)
    r_o[...] = mm(_OFF_R, GLA_V_WIDTH)
    ma_o[...] = mm(_OFF_MA, D_MODEL)
    mb_o[...] = mm(_OFF_MB, D_MODEL)
    ga = mm(_OFF_GA, LG_PAD)
    gt_o[...] = jax.nn.sigmoid(ga)
    al = _dot(ga, wal_ref[...], precision=HI) + bal_ref[...]
    lg_o[...] = (jnp.minimum(al, 0.0) - jnp.log1p(jnp.exp(-jnp.abs(al)))) * (1.0 / GLA_TAU)


def _in_projection(x, norm_g, w_p, w_al, b_al, seq_len=None):
    n = x.shape[0]
    tm = TM_PROJ
    assert n % tm == 0
    row = lambda w: pl.BlockSpec((tm, w), lambda i: (i, 0))
    full = lambda a: pl.BlockSpec(a.shape, lambda i: (0,) * a.ndim)
    widths = (KV_WIDTH,) * 6 + (LG_PAD, GLA_K_WIDTH, GLA_K_WIDTH, GLA_V_WIDTH, GLA_K_WIDTH, GLA_V_WIDTH,
                                 D_MODEL, D_MODEL)
    out_shape = [jax.ShapeDtypeStruct((NSA_GROUP, n, LANES), F32)] + [jax.ShapeDtypeStruct((n, w), F32) for w in widths]
    out_specs = [pl.BlockSpec((NSA_GROUP, tm, LANES), lambda i: (0, i, 0))] + [row(w) for w in widths]
    if seq_len is not None:
        assert seq_len % tm == 0
        per_seq = seq_len // tm
        out_shape += [jax.ShapeDtypeStruct((n // seq_len * KV_WIDTH, seq_len), F32)] * 6
        out_specs += [pl.BlockSpec((KV_WIDTH, tm), lambda i: (i // per_seq, i % per_seq))] * 6
    return pl.pallas_call(
        _inproj_kernel,
        grid=(n // tm,),
        in_specs=[row(D_MODEL), full(norm_g), full(w_p), full(w_al), full(b_al)],
        out_specs=out_specs,
        out_shape=out_shape,
        compiler_params=_params(("arbitrary",)),
        name="in_projection",
    )(x, norm_g, w_p, w_al, b_al)


def _gelu_tanh(x):
    return 0.5 * x * (1.0 + jnp.tanh(math.sqrt(2.0 / math.pi) * (x + 0.044715 * (x * x * x))))


def _compress_rows(src, n_ch, wcat_ref, pe_ref, w1f_ref, w2_ref):
    hid2 = NSA_KV_HEADS * CMP_HIDDEN
    acc = jnp.zeros((n_ch, 2 * hid2), F32)
    for l in range(0, CMP_STRIDE, 2):
        xl = jnp.concatenate([src[pl.ds(l, n_ch, stride=CMP_STRIDE), :],
                              src[pl.ds(l + 1, n_ch, stride=CMP_STRIDE), :]], axis=1).astype(BF16)
        acc = acc + _dot(xl, wcat_ref[l // 2])
    bias = _dot(pe_ref[...], w1f_ref[...], precision=HI)[0:1]
    bias2 = jnp.concatenate([bias] * NSA_KV_HEADS, axis=1)
    nxt = pltpu.roll(acc[:, hid2:], n_ch - 1, 0)
    h = acc[:, :hid2] + nxt + bias2
    return _dot(_gelu_tanh(h).astype(BF16), w2_ref[...])


def _compress_prompt_kernel(k_ref, v_ref, wk_ref, wv_ref, pek_ref, pev_ref, w1k_ref, w1v_ref, w2k_ref, w2v_ref,
                            kc_o, vc_o):
    n_ch = kc_o.shape[0]
    kc_o[...] = _compress_rows(k_ref, n_ch, wk_ref, pek_ref, w1k_ref, w2k_ref)
    vc_o[...] = _compress_rows(v_ref, n_ch, wv_ref, pev_ref, w1v_ref, w2v_ref)


def _compress_prompt(k_cmp, v_cmp, cw, bsz, t):
    n_ch = t // CMP_STRIDE
    full = lambda a: pl.BlockSpec(a.shape, lambda b: (0,) * a.ndim)
    seq = pl.BlockSpec((t, KV_WIDTH), lambda b: (b, 0))
    out = pl.BlockSpec((n_ch, KV_WIDTH), lambda b: (b, 0))
    ws = (cw["wcat_k"], cw["wcat_v"], cw["pe_k"], cw["pe_v"], cw["w1f_k"], cw["w1f_v"], cw["w2_k"], cw["w2_v"])
    return pl.pallas_call(
        _compress_prompt_kernel,
        grid=(bsz,),
        in_specs=[seq, seq] + [full(w) for w in ws],
        out_specs=[out, out],
        out_shape=[jax.ShapeDtypeStruct((bsz * n_ch, KV_WIDTH), F32)] * 2,
        compiler_params=_params(("arbitrary",)),
        name="nsa_compress_prompt",
    )(k_cmp, v_cmp, *ws)


def _paged_fetch(pt_ref, pools, bufs, sems, seq, slot, n_pages, pages_on_lanes):
    def body(p, _):
        pg = pt_ref[seq * n_pages + p]
        off = pl.multiple_of(p * PAGE_SIZE, PAGE_SIZE)
        for j, (pool, buf) in enumerate(zip(pools, bufs)):
            dst = buf.at[slot, :, pl.ds(off, PAGE_SIZE)] if pages_on_lanes else buf.at[slot, pl.ds(off, PAGE_SIZE)]
            pltpu.make_async_copy(pool.at[pl.ds(pg * KV_WIDTH, KV_WIDTH)], dst, sems.at[j, slot]).start()
        return 0
    lax.fori_loop(0, n_pages, body, 0)


def _paged_wait(bufs, sems, slot):
    for j, buf in enumerate(bufs):
        pltpu.make_async_copy(buf.at[slot], buf.at[slot], sems.at[j, slot]).wait()


def _compress_sample_kernel(pt_ref, pk_hbm, pv_hbm, wk_ref, wv_ref, pek_ref, pev_ref, w1k_ref, w1v_ref,
                            w2k_ref, w2v_ref, kc_o, vc_o, bufk, bufv, sems, rows_k, rows_v, *, n_pages):
    s = pl.program_id(0)
    n_seq = pl.num_programs(0)
    slot = s % 2
    pools, bufs = (pk_hbm, pv_hbm), (bufk, bufv)

    @pl.when(s == 0)
    def _():
        _paged_fetch(pt_ref, pools, bufs, sems, s, slot, n_pages, False)

    @pl.when(s + 1 < n_seq)
    def _():
        _paged_fetch(pt_ref, pools, bufs, sems, s + 1, 1 - slot, n_pages, False)

    _paged_wait(bufs, sems, slot)
    for p in range(n_pages):
        rows = slice(p * PAGE_SIZE, (p + 1) * PAGE_SIZE)
        rows_k[rows, :] = bufk[slot, rows, :].T
        rows_v[rows, :] = bufv[slot, rows, :].T
    n_ch = kc_o.shape[0]
    kc_o[...] = _compress_rows(rows_k, n_ch, wk_ref, pek_ref, w1k_ref, w2k_ref)
    vc_o[...] = _compress_rows(rows_v, n_ch, wv_ref, pev_ref, w1v_ref, w2v_ref)


def _compress_sample(pt_flat, pool_k, pool_v, cw, n_seq, n_pages):
    past = n_pages * PAGE_SIZE
    n_ch = past // CMP_STRIDE
    full = lambda a: pl.BlockSpec(a.shape, lambda s, pt: (0,) * a.ndim)
    hbm = pl.BlockSpec(memory_space=pl.ANY)
    out = pl.BlockSpec((n_ch, KV_WIDTH), lambda s, pt: (s, 0))
    ws = (cw["wcat_k"], cw["wcat_v"], cw["pe_k"], cw["pe_v"], cw["w1f_k"], cw["w1f_v"], cw["w2_k"], cw["w2_v"])
    return pl.pallas_call(
        functools.partial(_compress_sample_kernel, n_pages=n_pages),
        grid_spec=pltpu.PrefetchScalarGridSpec(
            num_scalar_prefetch=1,
            grid=(n_seq,),
            in_specs=[hbm, hbm] + [full(w) for w in ws],
            out_specs=[out, out],
            scratch_shapes=[pltpu.VMEM((2, past, KV_WIDTH), F32), pltpu.VMEM((2, past, KV_WIDTH), F32),
                            pltpu.SemaphoreType.DMA((2, 2)),
                            pltpu.VMEM((past, KV_WIDTH), F32), pltpu.VMEM((past, KV_WIDTH), F32)],
        ),
        out_shape=[jax.ShapeDtypeStruct((n_seq * n_ch, KV_WIDTH), F32)] * 2,
        compiler_params=_params(("arbitrary",)),
        name="nsa_compress_sample",
    )(pt_flat, pool_k, pool_v, *ws)


def _select_blocks(score, n_sel):
    blk = lax.broadcasted_iota(I32, score.shape, 1)
    cnt = jnp.zeros(score.shape, F32)
    for i in range(n_sel):
        col = score[:, i:i + 1]
        ahead = (col > score) | ((col == score) & (i < blk))
        cnt = cnt + jnp.where(ahead, 1.0, 0.0)
    n_top = min(N_SEL, n_sel)
    return jnp.where((cnt < n_top) & (blk < n_sel), 1.0, 0.0)


def _select_blocks_t(score_t, n_sel):
    blk = lax.broadcasted_iota(I32, score_t.shape, 0)
    cnt = jnp.zeros(score_t.shape, F32)
    for i in range(n_sel):
        row = jnp.broadcast_to(score_t[i:i + 1, :], score_t.shape)
        ahead = (row > score_t) | ((row == score_t) & (i < blk))
        cnt = cnt + jnp.where(ahead, 1.0, 0.0)
    return jnp.where(cnt < min(N_SEL, n_sel), 1.0, 0.0)


def _block_scores(imp, qpos, n_sel):
    blk = lax.broadcasted_iota(I32, imp.shape, 1)
    cur = qpos // SEL_BLOCK
    forced = (blk == 0) | (blk == cur) | (blk == cur - 1)
    causal = (blk * SEL_BLOCK) <= qpos
    score = jnp.where(causal, jnp.where(forced, SEL_FORCE, imp), -SEL_FORCE)
    return jnp.where(blk < n_sel, score, -3e38)


def _gate_mix(gexp, o_c, o_s, o_w, r):
    out = None
    for c, o in enumerate((o_c, o_s, o_w)):
        term = gexp[:, c * NSA_WIDTH + r * LANES:c * NSA_WIDTH + (r + 1) * LANES] * o
        out = term if out is None else out + term
    return out


def _nsa_prompt_kernel(q_ref, gt_ref, kc_ref, vc_ref, ks_ref, vs_ref, kw_ref, vw_ref,
                       gr_ref, bc_ref, wt_ref, mimp_ref, esel_ref, eg_ref, o_ref,
                       ksb, vsb, kwb, vwb, kcb, vcb, oc_s, msk_s, qs, *state, voff_blk):
    qb = pl.program_id(1)
    t = ks_ref.shape[0]
    n_ch = kc_ref.shape[0]
    n_sel = t // SEL_BLOCK
    tk = msk_s.shape[3]
    n_kc = t // tk
    tiles = tk // LANES

    @pl.when(qb == 0)
    def _():
        ksb[...] = ks_ref[...].astype(BF16)
        vsb[...] = vs_ref[...].astype(BF16)
        kwb[0:WINDOW, :] = jnp.zeros((WINDOW, KV_WIDTH), BF16)
        vwb[0:WINDOW, :] = jnp.zeros((WINDOW, KV_WIDTH), BF16)
        kwb[WINDOW:, :] = kw_ref[...].astype(BF16)
        vwb[WINDOW:, :] = vw_ref[...].astype(BF16)
        kcb[...] = kc_ref[...].astype(BF16)
        vcb[...] = vc_ref[...].astype(BF16)

    q0 = qb * Q_BLOCK
    lane = lax.broadcasted_iota(I32, (Q_BLOCK, LANES), 1)
    qpos = q0 + lax.broadcasted_iota(I32, (Q_BLOCK, 1), 0)
    upper = lane >= HEAD_DIM

    nh = NSA_HEADS
    rows = nh * Q_BLOCK
    for h in range(nh):
        keep = upper if h >= NSA_GROUP else jnp.logical_not(upper)
        qs[h * Q_BLOCK:(h + 1) * Q_BLOCK, :] = jnp.where(keep, q_ref[h % NSA_GROUP], 0.0).astype(BF16)
    q_all = qs[...]

    def per_group(x3, add2):
        return jnp.concatenate([x3[g * NSA_GROUP:(g + 1) * NSA_GROUP] + add2[g][None]
                                for g in range(NSA_KV_HEADS)], axis=0)

    any_c = (qpos >= (CMP_BLOCK - 1))[None]
    s = _nt(q_all, kcb[...]).reshape(nh, Q_BLOCK, n_ch) + bc_ref[0]
    e = jnp.exp(s - jnp.max(s, axis=-1, keepdims=True))
    den = jnp.maximum(jnp.sum(e, axis=-1, keepdims=True), 1e-20)
    p = e * jnp.where(any_c, 1.0 / den, 0.0)
    oc_s[...] = _dot(p.reshape(rows, n_ch).astype(BF16), vcb[...])
    for g in range(NSA_KV_HEADS):
        psum = p[g * NSA_GROUP]
        for r in range(1, NSA_GROUP):
            psum = psum + p[g * NSA_GROUP + r]
        imp = _dot(psum, mimp_ref[...], precision=HI)
        sel_t = _select_blocks_t(_block_scores(imp, qpos, n_sel).T[0:n_sel], n_sel).astype(BF16)
        for c in range(n_kc):
            kpos = c * tk + lax.broadcasted_iota(I32, (Q_BLOCK, tk), 1)
            picked = _tn(sel_t, esel_ref[0:n_sel, c * tk:(c + 1) * tk]) > 0.5
            msk_s[g, c] = jnp.where(picked & (kpos <= qpos), 0.0, NEG_INF)

    ph = PIPE_HEADS
    nb = nh // ph
    brow = ph * Q_BLOCK
    m_r, l_r, a_r = state[0:nb], state[nb:2 * nb], state[2 * nb:3 * nb]
    for b in range(nb):
        m_r[b][...] = jnp.full((brow, 1), NEG_INF, F32)
        l_r[b][...] = jnp.zeros((brow, 1), F32)
        a_r[b][...] = jnp.zeros((brow, KV_WIDTH), F32)
    q_blk = lambda b: qs[b * brow:(b + 1) * brow, :]

    def chunk(kt, _):
        k0 = pl.multiple_of(kt * tk, tk)
        kk = ksb[pl.ds(k0, tk), :]
        vv = vsb[pl.ds(k0, tk), :]
        base = voff_blk - qb + kt * tiles

        def scores(b):
            bias = jnp.concatenate([gr_ref[base + j, b * ph:(b + 1) * ph] for j in range(tiles)], axis=-1)
            s = _nt(q_blk(b), kk).reshape(ph, Q_BLOCK, tk) + bias + msk_s[(b * ph) // NSA_GROUP, kt][None]
            return s.reshape(brow, tk)

        s_next = scores(0)
        for b in range(nb):
            s = s_next
            if b + 1 < nb:
                s_next = scores(b + 1)
            m_old = m_r[b][...]
            m_new = jnp.maximum(m_old, jnp.max(s, axis=-1, keepdims=True))
            alpha = jnp.exp(m_old - m_new)
            e = jnp.exp(s - m_new)
            l_r[b][...] = alpha * l_r[b][...] + jnp.sum(e, axis=-1, keepdims=True)
            m_r[b][...] = m_new
            a_r[b][...] = alpha * a_r[b][...] + _dot(e.astype(BF16), vv)
        return 0
    lax.fori_loop(0, (q0 + Q_BLOCK + tk - 1) // tk, chunk, 0)

    n_w = WINDOW + Q_BLOCK
    wpos = q0 - WINDOW + lax.broadcasted_iota(I32, (Q_BLOCK, n_w), 1)
    before_start = jnp.where(wpos >= 0, 0.0, NEG_INF)[None]
    w0 = pl.multiple_of(q0, Q_BLOCK)
    kw = kwb[pl.ds(w0, n_w), :]
    vw = vwb[pl.ds(w0, n_w), :]

    def w_scores(b):
        s = _nt(q_blk(b), kw).reshape(ph, Q_BLOCK, n_w) + wt_ref[b * ph:(b + 1) * ph] + before_start
        return s.reshape(brow, n_w)

    o_w = []
    s_next = w_scores(0)
    for b in range(nb):
        s = s_next
        if b + 1 < nb:
            s_next = w_scores(b + 1)
        e = jnp.exp(s - jnp.max(s, axis=-1, keepdims=True))
        den = jnp.maximum(jnp.sum(e, axis=-1, keepdims=True), 1e-20)
        o_w.append(_dot(e.astype(BF16), vw) / den)
    o_w = jnp.concatenate(o_w, axis=0)
    o_s = jnp.concatenate([a_r[b][...] / jnp.maximum(l_r[b][...], 1e-20) for b in range(nb)], axis=0)
    o_c = oc_s[...]

    gexp = _dot(gt_ref[...], eg_ref[...], precision=HI)
    head = lambda x, h: x[h * Q_BLOCK:(h + 1) * Q_BLOCK]
    for r in range(NSA_GROUP):
        pick = lambda x: jnp.where(upper, head(x, NSA_GROUP + r), head(x, r))
        o_ref[r] = _gate_mix(gexp, pick(o_c), pick(o_s), pick(o_w), r)


def _nsa_prompt(q4, gates, kc, vc, ks, vs, kw, vw, tabs, bsz, t):
    n_qb = t // Q_BLOCK
    n_ch = t // CMP_STRIDE
    tk = min(SEL_CHUNK, t)
    full = lambda a: pl.BlockSpec(a.shape, lambda b, i: (0,) * a.ndim)
    seq = pl.BlockSpec((t, KV_WIDTH), lambda b, i: (b, 0))
    cseq = pl.BlockSpec((n_ch, KV_WIDTH), lambda b, i: (b, 0))
    qspec = pl.BlockSpec((NSA_GROUP, Q_BLOCK, LANES), lambda b, i: (0, b * n_qb + i, 0))
    gr, bc, wt, mimp, esel, eg = tabs["gr"], tabs["bc"], tabs["wt"], tabs["mimp"], tabs["esel"], tabs["eg"]
    head_tile = pltpu.VMEM((NSA_HEADS * Q_BLOCK, KV_WIDTH), F32)
    nb, brow = NSA_HEADS // PIPE_HEADS, PIPE_HEADS * Q_BLOCK
    per_block = [pltpu.VMEM((brow, 1), F32)] * (2 * nb) + [pltpu.VMEM((brow, KV_WIDTH), F32)] * nb
    return pl.pallas_call(
        functools.partial(_nsa_prompt_kernel, voff_blk=tabs["voff"] // LANES),
        grid=(bsz, n_qb),
        in_specs=[qspec, pl.BlockSpec((Q_BLOCK, LG_PAD), lambda b, i: (b * n_qb + i, 0)),
                  cseq, cseq, seq, seq, seq, seq, full(gr),
                  pl.BlockSpec((1,) + bc.shape[1:], lambda b, i: (i, 0, 0, 0)), full(wt),
                  full(mimp), full(esel), full(eg)],
        out_specs=qspec,
        out_shape=jax.ShapeDtypeStruct((NSA_GROUP, bsz * t, LANES), F32),
        scratch_shapes=[pltpu.VMEM((t, KV_WIDTH), BF16), pltpu.VMEM((t, KV_WIDTH), BF16),
                        pltpu.VMEM((t + WINDOW, KV_WIDTH), BF16), pltpu.VMEM((t + WINDOW, KV_WIDTH), BF16),
                        pltpu.VMEM((n_ch, KV_WIDTH), BF16), pltpu.VMEM((n_ch, KV_WIDTH), BF16),
                        head_tile,
                        pltpu.VMEM((NSA_KV_HEADS, t // tk, Q_BLOCK, tk), F32),
                        pltpu.VMEM((NSA_HEADS * Q_BLOCK, KV_WIDTH), BF16)] + per_block,
        compiler_params=_params(("arbitrary", "arbitrary")),
        name="nsa_attention_prompt",
    )(q4, gates, kc, vc, ks, vs, kw, vw, gr, bc, wt, mimp, esel, eg)


def _nsa_sample_kernel(pt_ref, q_ref, gt_ref, kc_ref, vc_ref, ksn_ref, vsn_ref, kwn_ref, vwn_ref, bkw_ref, bvw_ref,
                       pks_hbm, pvs_hbm, gs_ref, gn_ref, bcs_ref, bws_ref, mimp_ref, esel_ref, eg_ref, o_ref,
                       bufk, bufv, sems, ksb, vsb, *, n_pages):
    sq = pl.program_id(0)
    n_seq = pl.num_programs(0)
    slot = sq % 2
    pools, bufs = (pks_hbm, pvs_hbm), (bufk, bufv)
    past = n_pages * PAGE_SIZE
    s_new = ksn_ref.shape[0]
    rows = NSA_GROUP * s_new
    n_cmp_rows = kc_ref.shape[0]
    n_sel = (past + s_new + SEL_BLOCK - 1) // SEL_BLOCK
    n_past_blk = past // SEL_BLOCK
    w_buf = bkw_ref.shape[1]

    @pl.when(sq == 0)
    def _():
        _paged_fetch(pt_ref, pools, bufs, sems, sq, slot, n_pages, True)

    @pl.when(sq + 1 < n_seq)
    def _():
        _paged_fetch(pt_ref, pools, bufs, sems, sq + 1, 1 - slot, n_pages, True)

    _paged_wait(bufs, sems, slot)
    ksb[...] = bufk[slot].astype(BF16)
    vsb[...] = bufv[slot].astype(BF16)

    lane = lax.broadcasted_iota(I32, (rows, LANES), 1)
    upper = lane >= HEAD_DIM
    qi = lax.broadcasted_iota(I32, (rows, 1), 0) % s_new
    pad_new = lambda ref: jnp.concatenate([ref[...], jnp.zeros((LANES - s_new, KV_WIDTH), F32)], axis=0).astype(BF16)
    ksn, vsn, kwn, vwn = pad_new(ksn_ref), pad_new(vsn_ref), pad_new(kwn_ref), pad_new(vwn_ref)
    kcb, vcb = kc_ref[...].astype(BF16), vc_ref[...].astype(BF16)
    bkw, bvw = bkw_ref[...].astype(BF16), bvw_ref[...].astype(BF16)
    new_causal = (lane < s_new) & (lane <= qi)

    def stacked_q(g):
        keep = upper if g == 1 else jnp.logical_not(upper)
        qs = jnp.concatenate([q_ref[r] for r in range(NSA_GROUP)], axis=0)
        return jnp.where(keep, qs, 0.0).astype(BF16)

    cend = lax.broadcasted_iota(I32, (rows, n_cmp_rows), 1) * CMP_STRIDE + (CMP_BLOCK - 1)
    valid_c = (past + qi) >= cend
    o_c, psums = [], []
    for g in range(NSA_KV_HEADS):
        s = _nt(stacked_q(g), kcb) + bcs_ref[g]
        e, den = _masked_softmax_parts(s, valid_c)
        p = e / den
        o_c.append(_dot(p.astype(BF16), vcb))
        ps = p[0:s_new]
        for r in range(1, NSA_GROUP):
            ps = ps + p[r * s_new:(r + 1) * s_new]
        psums.append(ps)
    imp = _dot(jnp.concatenate(psums, axis=0), mimp_ref[...], precision=HI)
    qpos_sel = past + lax.broadcasted_iota(I32, (NSA_KV_HEADS * s_new, 1), 0) % s_new
    sel = _select_blocks(_block_scores(imp, qpos_sel, n_sel), n_sel)
    mask_past = _dot(sel[:, :n_past_blk].astype(BF16), esel_ref[...])
    sel_new = sel[:, n_past_blk:n_past_blk + 1]

    gexp = _dot(gt_ref[...], eg_ref[...], precision=HI)
    o_s, o_w = [], []
    for g in range(NSA_KV_HEADS):
        qg = stacked_q(g)
        tile_rows = lambda a: jnp.concatenate([a[g * s_new:(g + 1) * s_new]] * NSA_GROUP, axis=0)
        valid_p = tile_rows(mask_past) > 0.5
        s_p = jnp.where(valid_p, _dot(qg, ksb[...]) + gs_ref[g], NEG_INF)
        valid_n = new_causal & (tile_rows(sel_new) > 0.5)
        s_n = jnp.where(valid_n, _nt(qg, ksn) + gn_ref[g], NEG_INF)
        m = jnp.maximum(jnp.max(s_p, axis=-1, keepdims=True), jnp.max(s_n, axis=-1, keepdims=True))
        e_p = jnp.where(valid_p, jnp.exp(s_p - m), 0.0)
        e_n = jnp.where(valid_n, jnp.exp(s_n - m), 0.0)
        den = jnp.maximum(jnp.sum(e_p, axis=-1, keepdims=True) + jnp.sum(e_n, axis=-1, keepdims=True), 1e-20)
        o_s.append((_nt(e_p.astype(BF16), vsb[...]) + _dot(e_n.astype(BF16), vsn)) / den)
        jb = lax.broadcasted_iota(I32, (rows, w_buf), 1)
        dist_b = w_buf + qi - jb
        valid_b = (dist_b < WINDOW) & (past - w_buf + jb >= 0)
        s_b = jnp.where(valid_b, _dot(qg, bkw) + bws_ref[g], NEG_INF)
        s_n = jnp.where(new_causal, _nt(qg, kwn) + gn_ref[g], NEG_INF)
        m = jnp.maximum(jnp.max(s_b, axis=-1, keepdims=True), jnp.max(s_n, axis=-1, keepdims=True))
        e_b = jnp.where(valid_b, jnp.exp(s_b - m), 0.0)
        e_n = jnp.where(new_causal, jnp.exp(s_n - m), 0.0)
        den = jnp.maximum(jnp.sum(e_b, axis=-1, keepdims=True) + jnp.sum(e_n, axis=-1, keepdims=True), 1e-20)
        o_w.append((_nt(e_b.astype(BF16), bvw) + _dot(e_n.astype(BF16), vwn)) / den)

    up8 = upper[0:s_new]
    for r in range(NSA_GROUP):
        pick = lambda o: jnp.where(up8, o[1][r * s_new:(r + 1) * s_new], o[0][r * s_new:(r + 1) * s_new])
        o_ref[r] = _gate_mix(gexp, pick(o_c), pick(o_s), pick(o_w), r)


def _nsa_sample(pt_flat, q4, gates, kc, vc, ksn, vsn, kwn, vwn, buf_kw, buf_vw, pool_ks, pool_vs, tabs,
                n_seq, s_new, n_pages):
    past = n_pages * PAGE_SIZE
    n_ch = past // CMP_STRIDE
    w_buf = buf_kw.shape[1]
    full = lambda a: pl.BlockSpec(a.shape, lambda s, pt: (0,) * a.ndim)
    hbm = pl.BlockSpec(memory_space=pl.ANY)
    rows = lambda n, w: pl.BlockSpec((n, w), lambda s, pt: (s, 0))
    qspec = pl.BlockSpec((NSA_GROUP, s_new, LANES), lambda s, pt: (0, s, 0))
    consts = (tabs["gs"], tabs["gn"], tabs["bcs"], tabs["bws"], tabs["mimp_s"], tabs["esel_s"], tabs["eg"])
    return pl.pallas_call(
        functools.partial(_nsa_sample_kernel, n_pages=n_pages),
        grid_spec=pltpu.PrefetchScalarGridSpec(
            num_scalar_prefetch=1,
            grid=(n_seq,),
            in_specs=[qspec, rows(s_new, LG_PAD), rows(n_ch, KV_WIDTH), rows(n_ch, KV_WIDTH)]
                     + [rows(s_new, KV_WIDTH)] * 4 + [rows(KV_WIDTH, w_buf)] * 2 + [hbm, hbm]
                     + [full(c) for c in consts],
            out_specs=qspec,
            scratch_shapes=[pltpu.VMEM((2, KV_WIDTH, past), F32), pltpu.VMEM((2, KV_WIDTH, past), F32),
                            pltpu.SemaphoreType.DMA((2, 2)),
                            pltpu.VMEM((KV_WIDTH, past), BF16), pltpu.VMEM((KV_WIDTH, past), BF16)],
        ),
        out_shape=jax.ShapeDtypeStruct((NSA_GROUP, n_seq * s_new, LANES), F32),
        compiler_params=_params(("arbitrary",)),
        name="nsa_attention_sample",
    )(pt_flat, q4, gates, kc, vc, ksn, vsn, kwn, vwn, buf_kw, buf_vw, pool_ks, pool_vs, *consts)


def _gla_kernel(q_ref, k_ref, v_ref, lg_ref, r_ref, gn_ref, s0_ref, o_ref, sfin_ref, s_scr):
    c = pl.program_id(1)
    n_c = pl.num_programs(1)
    cl = q_ref.shape[0]

    @pl.when(c == 0)
    def _():
        s_scr[...] = s0_ref[...]

    row_t = lax.broadcasted_iota(I32, (cl, LANES), 0)
    lane_t = lax.broadcasted_iota(I32, (cl, LANES), 1)
    causal = lax.broadcasted_iota(I32, (cl, cl), 1) <= lax.broadcasted_iota(I32, (cl, cl), 0)
    row_s = lax.broadcasted_iota(I32, (LANES, GLA_DV), 0)
    pair = LANES // GLA_DK
    for p in range(GLA_HEADS // pair):
        cols = slice(p * LANES, (p + 1) * LANES)
        lg = lg_ref[:, cols]
        b = lg
        sh = 1
        while sh < cl:
            b = b + jnp.where(row_t >= sh, pltpu.roll(b, sh, 0), 0.0)
            sh *= 2
        b_last = b[cl - 1:cl, :]
        qt = q_ref[:, cols] * jnp.exp(b)
        kp = k_ref[:, cols]
        kt = (kp * jnp.exp(-b)).astype(BF16)
        khat = (kp * jnp.exp(b_last - b)).astype(BF16)
        dec = jnp.exp(jnp.broadcast_to(b_last, (LANES, LANES))).T
        s_old = s_scr[cols, :]
        s_bf = s_old.astype(BF16)
        upd = jnp.zeros((LANES, GLA_DV), F32)
        for hh in range(pair):
            h = p * pair + hh
            vcols = slice(h * GLA_DV, (h + 1) * GLA_DV)
            mine = (lane_t >= GLA_DK) if hh == 1 else (lane_t < GLA_DK)
            qm = jnp.where(mine, qt, 0.0).astype(BF16)
            att = jnp.where(causal, _nt(qm, kt), 0.0)
            vh = v_ref[:, vcols].astype(BF16)
            o = _dot(qm, s_bf) + _dot(att.astype(BF16), vh)
            o = o * lax.rsqrt(jnp.mean(o * o, axis=-1, keepdims=True) + RMS_EPS) * gn_ref[...]
            rh = r_ref[:, vcols]
            o_ref[:, vcols] = o * (rh * jax.nn.sigmoid(rh))
            u = _tn(khat, vh)
            upd = jnp.where((row_s >= GLA_DK) == (hh == 1), u, upd)
        s_scr[cols, :] = s_old * dec + upd

    @pl.when(c == n_c - 1)
    def _():
        sfin_ref[...] = s_scr[...]


def _gla(q_l, k_l, v_l, lg, r, g_norm, s0, bsz, n_c, cl):
    srows = GLA_HEADS * GLA_DK
    blk = lambda w: pl.BlockSpec((cl, w), lambda b, c: (b * n_c + c, 0))
    st = pl.BlockSpec((srows, GLA_DV), lambda b, c: (b, 0))
    return pl.pallas_call(
        _gla_kernel,
        grid=(bsz, n_c),
        in_specs=[blk(GLA_K_WIDTH), blk(GLA_K_WIDTH), blk(GLA_V_WIDTH), blk(GLA_K_WIDTH), blk(GLA_V_WIDTH),
                  pl.BlockSpec(g_norm.shape, lambda b, c: (0, 0)), st],
        out_specs=[blk(GLA_V_WIDTH), st],
        out_shape=[jax.ShapeDtypeStruct((bsz * n_c * cl, GLA_V_WIDTH), F32),
                   jax.ShapeDtypeStruct((bsz * srows, GLA_DV), F32)],
        scratch_shapes=[pltpu.VMEM((srows, GLA_DV), F32)],
        compiler_params=_params(("arbitrary", "arbitrary")),
        name="gla",
    )(q_l, k_l, v_l, lg, r, g_norm, s0)


def _merge_kernel(x_ref, on_ref, og_ref, ma_ref, mb_ref, wn_ref, wg_ref, wo_ref, gf_ref, rw_ref, rb_ref,
                  x1_o, h2t_o, ei_o, gw_o, rk_o, cnt_o, carry):
    i = pl.program_id(0)
    tm = x_ref.shape[0]

    @pl.when(i == 0)
    def _():
        carry[...] = jnp.zeros(carry.shape, F32)

    on = jnp.concatenate([on_ref[r] for r in range(NSA_GROUP)], axis=1).astype(BF16)
    ya = _dot(on, wn_ref[...])
    yb = _dot(og_ref[...].astype(BF16), wg_ref[...])
    m = jax.nn.sigmoid(ma_ref[...]) * ya + jax.nn.sigmoid(mb_ref[...]) * yb
    x1 = x_ref[...] + _dot(m.astype(BF16), wo_ref[...])
    x1_o[...] = x1
    h2 = x1 * lax.rsqrt(jnp.mean(x1 * x1, axis=-1, keepdims=True) + RMS_EPS) * gf_ref[...]
    for s in range(D_MODEL // LANES):
        h2t_o[pl.ds(s, tm, stride=SUBLANES), :] = h2[:, s * LANES:(s + 1) * LANES]

    logits = _dot(h2, rw_ref[...], precision=HI) + rb_ref[...]
    lane = lax.broadcasted_iota(I32, (tm, LANES), 1)
    lane_f = lane.astype(F32)
    work = logits
    vals, idxs = [], []
    for _ in range(TOP_K):
        mk = jnp.max(work, axis=-1, keepdims=True)
        ik = jnp.min(jnp.where(work == mk, lane_f, float(LANES)), axis=-1, keepdims=True)
        vals.append(mk)
        idxs.append(ik)
        work = jnp.where(lane_f == ik, -jnp.inf, work)
    es = [jnp.exp(v - vals[0]) for v in vals]
    den = es[0]
    for e in es[1:]:
        den = den + e
    onehot = jnp.zeros((tm, LANES), F32)
    for ik in idxs:
        onehot = onehot + jnp.where(lane_f == ik, 1.0, 0.0)
    below = lax.broadcasted_iota(I32, (tm, tm), 1) < lax.broadcasted_iota(I32, (tm, tm), 0)
    before = _dot(jnp.where(below, 1.0, 0.0).astype(BF16), onehot.astype(BF16)) + carry[0:1, :]
    ei = jnp.zeros((tm, LANES), F32)
    gw = jnp.zeros((tm, LANES), F32)
    rk = jnp.zeros((tm, LANES), F32)
    for k in range(TOP_K):
        rank_k = jnp.sum(jnp.where(lane_f == idxs[k], before, 0.0), axis=-1, keepdims=True)
        ei = jnp.where(lane == k, idxs[k], ei)
        gw = jnp.where(lane == k, es[k] / den, gw)
        rk = jnp.where(lane == k, rank_k, rk)
    ei_o[...] = ei.astype(I32)
    gw_o[...] = gw
    rk_o[...] = rk.astype(I32)
    carry[0:1, :] = carry[0:1, :] + jnp.sum(onehot, axis=0, keepdims=True)
    cnt_o[...] = carry[...]


def _merge(x, o_nsa4, o_gla, m_a, m_b, mw):
    n = x.shape[0]
    tm = TM_PROJ
    row = lambda w: pl.BlockSpec((tm, w), lambda i: (i, 0))
    full = lambda a: pl.BlockSpec(a.shape, lambda i: (0,) * a.ndim)
    ws = (mw["wn"], mw["wg"], mw["wo"], mw["gf"], mw["rw"], mw["rb"])
    return pl.pallas_call(
        _merge_kernel,
        grid=(n // tm,),
        in_specs=[row(D_MODEL), pl.BlockSpec((NSA_GROUP, tm, LANES), lambda i: (0, i, 0)), row(GLA_V_WIDTH),
                  row(D_MODEL), row(D_MODEL)] + [full(w) for w in ws],
        out_specs=[row(D_MODEL), pl.BlockSpec((tm * SUBLANES, LANES), lambda i: (i, 0)),
                   row(LANES), row(LANES), row(LANES), pl.BlockSpec((SUBLANES, LANES), lambda i: (0, 0))],
        out_shape=[jax.ShapeDtypeStruct((n, D_MODEL), F32), jax.ShapeDtypeStruct((n * SUBLANES, LANES), F32),
                   jax.ShapeDtypeStruct((n, LANES), I32), jax.ShapeDtypeStruct((n, LANES), F32),
                   jax.ShapeDtypeStruct((n, LANES), I32), jax.ShapeDtypeStruct((SUBLANES, LANES), F32)],
        scratch_shapes=[pltpu.VMEM((SUBLANES, LANES), F32)],
        compiler_params=_params(("arbitrary",)),
        name="merge_router",
    )(x, o_nsa4, o_gla, m_a, m_b, *ws)


def _token_tile(ref, row):
    return ref.at[pl.ds(row * SUBLANES, SUBLANES)]


def _dispatch_kernel(meta_ref, dest_ref, h_ref, xs_hbm, dsm, ztile, sem_idx, sem_row, sem_pad, *, td):
    i = pl.program_id(0)
    n_steps = pl.num_programs(0)
    idx_copy = pltpu.make_async_copy(dest_ref, dsm, sem_idx)
    idx_copy.start()
    idx_copy.wait()

    per_row = LANES // TOP_K

    def body(rr, _):
        for c in range(LANES):
            tk = rr * per_row + c // TOP_K
            pltpu.make_async_copy(_token_tile(h_ref, tk), _token_tile(xs_hbm, dsm[rr, c]), sem_row).start()
        return 0
    lax.fori_loop(0, td // per_row, body, 0)

    @pl.when(i == n_steps - 1)
    def _():
        ztile[...] = jnp.zeros(ztile.shape, F32)
        blk_rows = ztile.shape[0]
        n_blocks = xs_hbm.shape[0] // blk_rows

        def pads(start_or_wait):
            def per_expert(e, _):
                first = meta_ref[N_EXPERTS + e] + meta_ref[e]
                last = meta_ref[N_EXPERTS + e] + meta_ref[2 * N_EXPERTS + e]

                def per_row(rw, _):
                    cp = pltpu.make_async_copy(ztile.at[pl.ds(0, SUBLANES)], _token_tile(xs_hbm, rw), sem_pad)
                    cp.start() if start_or_wait else cp.wait()
                    return 0
                lax.fori_loop(first, last, per_row, 0)
                return 0
            lax.fori_loop(0, N_EXPERTS, per_expert, 0)

            def per_block(bk, _):
                cp = pltpu.make_async_copy(ztile, xs_hbm.at[pl.ds(bk * blk_rows, blk_rows)], sem_pad)
                cp.start() if start_or_wait else cp.wait()
                return 0
            lax.fori_loop(meta_ref[3 * N_EXPERTS], n_blocks, per_block, 0)
        pads(True)
        pads(False)

    for _ in range(TOP_K):
        pltpu.make_async_copy(h_ref, xs_hbm.at[pl.ds(0, td * SUBLANES)], sem_row).wait()


def _dispatch(meta, dest2d, h2t, n_rows_total, td):
    n = h2t.shape[0] // SUBLANES
    hbm = pl.BlockSpec(memory_space=pl.ANY)
    drows = td * TOP_K // LANES
    return pl.pallas_call(
        functools.partial(_dispatch_kernel, td=td),
        grid_spec=pltpu.PrefetchScalarGridSpec(
            num_scalar_prefetch=1,
            grid=(n // td,),
            in_specs=[pl.BlockSpec((drows, LANES), lambda i, m: (i, 0)),
                      pl.BlockSpec((td * SUBLANES, LANES), lambda i, m: (i, 0))],
            out_specs=hbm,
            scratch_shapes=[pltpu.SMEM((drows, LANES), I32), pltpu.VMEM((MOE_ROWS * SUBLANES, LANES), F32),
                            pltpu.SemaphoreType.DMA, pltpu.SemaphoreType.DMA, pltpu.SemaphoreType.DMA],
        ),
        out_shape=jax.ShapeDtypeStruct((n_rows_total * SUBLANES, LANES), F32),
        compiler_params=_params(("arbitrary",)),
        name="moe_dispatch",
    )(meta, dest2d, h2t)


def _moe_kernel(be_ref, nu_ref, xs_ref, wg_ref, bg_ref, wu_ref, bu_ref, wd_ref, bd_ref, y_ref, wg_s, wu_s, wd_s):
    i = pl.program_id(0)
    rows = xs_ref.shape[0] // SUBLANES
    n_s = D_MODEL // LANES

    @pl.when(i < nu_ref[0])
    def _():
        @pl.when((i == 0) | (be_ref[i] != be_ref[jnp.maximum(i - 1, 0)]))
        def _():
            wg_s[...] = wg_ref[0].astype(BF16)
            wu_s[...] = wu_ref[0].astype(BF16)
            wd_s[...] = wd_ref[0].astype(BF16)

        xb = jnp.concatenate([xs_ref[pl.ds(s, rows, stride=SUBLANES), :] for s in range(n_s)], axis=1).astype(BF16)
        g = _dot(xb, wg_s[...]) + bg_ref[0]
        u = _dot(xb, wu_s[...]) + bu_ref[0]
        g = jnp.minimum(g, SWIGLU_LIMIT)
        u = jnp.clip(u, -SWIGLU_LIMIT, SWIGLU_LIMIT)
        hh = (u + 1.0) * (g * jax.nn.sigmoid(SWIGLU_ALPHA * g))
        y = _dot(hh.astype(BF16), wd_s[...]) + bd_ref[0]
        for s in range(n_s):
            y_ref[pl.ds(s, rows, stride=SUBLANES), :] = y[:, s * LANES:(s + 1) * LANES]

    @pl.when(i >= nu_ref[0])
    def _():
        y_ref[...] = jnp.zeros(y_ref.shape, F32)


def _moe_experts(block_e, n_used, xs, ew, n_blocks):
    blk = lambda i, be, nu: jnp.minimum(i, nu[0] - 1)
    rows = pl.BlockSpec((MOE_ROWS * SUBLANES, LANES), lambda i, be, nu: (i, 0))
    wspec = lambda a: pl.BlockSpec((1,) + a.shape[1:], lambda i, be, nu: (be[blk(i, be, nu)], 0, 0))
    ws = (ew["wg"], ew["bg"], ew["wu"], ew["bu"], ew["wd"], ew["bd"])
    return pl.pallas_call(
        _moe_kernel,
        grid_spec=pltpu.PrefetchScalarGridSpec(
            num_scalar_prefetch=2,
            grid=(n_blocks,),
            in_specs=[rows] + [wspec(w) for w in ws],
            out_specs=rows,
            scratch_shapes=[pltpu.VMEM((D_MODEL, D_FF), BF16), pltpu.VMEM((D_MODEL, D_FF), BF16),
                            pltpu.VMEM((D_FF, D_MODEL), BF16)],
        ),
        out_shape=jax.ShapeDtypeStruct(xs.shape, F32),
        compiler_params=_params(("arbitrary",)),
        name="moe_experts",
    )(block_e, n_used, xs, *ws)


def _combine_kernel(dest_ref, gw_ref, x1_ref, gfin_ref, y_hbm, out_ref, dsm, buf, sem_idx, sem_row):
    tc = x1_ref.shape[0]
    idx_copy = pltpu.make_async_copy(dest_ref, dsm, sem_idx)
    idx_copy.start()
    idx_copy.wait()

    per_row = LANES // TOP_K

    def body(rr, _):
        for c in range(LANES):
            tk = rr * per_row + c // TOP_K
            pltpu.make_async_copy(_token_tile(y_hbm, dsm[rr, c]), _token_tile(buf.at[c % TOP_K], tk), sem_row).start()
        return 0
    lax.fori_loop(0, tc // per_row, body, 0)
    for k in range(TOP_K):
        pltpu.make_async_copy(y_hbm.at[pl.ds(0, tc * SUBLANES)], buf.at[k], sem_row).wait()

    gw = gw_ref[...]
    parts = []
    for s in range(D_MODEL // LANES):
        acc = None
        for k in range(TOP_K):
            term = buf[k, pl.ds(s, tc, stride=SUBLANES), :] * gw[:, k:k + 1]
            acc = term if acc is None else acc + term
        parts.append(acc)
    x2 = x1_ref[...] + jnp.concatenate(parts, axis=1)
    out_ref[...] = x2 * lax.rsqrt(jnp.mean(x2 * x2, axis=-1, keepdims=True) + RMS_EPS) * gfin_ref[...]


def _combine(dest2d, gw, x1, g_final, y_rows):
    n = x1.shape[0]
    tc = TC_COMBINE
    drows = tc * TOP_K // LANES
    row = lambda w: pl.BlockSpec((tc, w), lambda i: (i, 0))
    return pl.pallas_call(
        _combine_kernel,
        grid=(n // tc,),
        in_specs=[pl.BlockSpec((drows, LANES), lambda i: (i, 0)), row(LANES), row(D_MODEL),
                  pl.BlockSpec(g_final.shape, lambda i: (0, 0)), pl.BlockSpec(memory_space=pl.ANY)],
        out_specs=row(D_MODEL),
        out_shape=jax.ShapeDtypeStruct((n, D_MODEL), F32),
        scratch_shapes=[pltpu.SMEM((drows, LANES), I32), pltpu.VMEM((TOP_K, tc * SUBLANES, LANES), F32),
                        pltpu.SemaphoreType.DMA, pltpu.SemaphoreType.DMA],
        compiler_params=_params(("arbitrary",)),
        name="moe_combine",
    )(dest2d, gw, x1, g_final, y_rows)


def _bucket_table(max_dist):
    n = np.arange(max_dist, dtype=np.int64)
    scaled = np.log(np.maximum(n, 1).astype(np.float64) / REL_EXACT) / math.log(REL_MAX_DIST / REL_EXACT)
    large = REL_EXACT + (scaled * (REL_BUCKETS - REL_EXACT)).astype(np.int64)
    return np.where(n < REL_EXACT, n, np.minimum(large, REL_BUCKETS - 1)).astype(np.int32)


def _bias_lookup(rel_bias, dist):
    d = np.maximum(dist, 0)
    buckets = _bucket_table(int(d.max()) + 1)[d]
    return jnp.take(rel_bias.astype(F32).T, jnp.asarray(buckets), axis=1)


def _skew(w, n_rows, step, width):
    h, l = w.shape
    flat = jnp.tile(w, (1, n_rows))[:, :n_rows * (l - step)]
    return flat.reshape(h, n_rows, l - step)[:, :, :width]


def _importance_matrix(n_rows, n_cmp, n_sel, n_cols):
    rc = CMP_BLOCK // CMP_STRIDE
    rs = SEL_BLOCK // CMP_STRIDE
    m = np.zeros((n_rows, n_cols), np.float32)
    for j in range(n_sel):
        for o in range(rs + rc - 1):
            w = min(o - (rc - 1) + rc, rs) - max(o - (rc - 1), 0)
            c = rs * j + o - (rc - 1)
            if 0 <= c < n_cmp:
                m[c, j] += w
    return jnp.asarray(m)


def _block_expand(n_blocks, n_rows=LANES):
    e = np.zeros((n_rows, n_blocks * SEL_BLOCK), np.float32)
    for j in range(n_blocks):
        e[j, j * SEL_BLOCK:(j + 1) * SEL_BLOCK] = 1.0
    return jnp.asarray(e, dtype=BF16)


def _gate_expand():
    e = np.zeros((LG_PAD, 3 * NSA_WIDTH), np.float32)
    for g in range(NSA_KV_HEADS):
        for r in range(NSA_GROUP):
            for c in range(3):
                lo = c * NSA_WIDTH + r * LANES + g * HEAD_DIM
                e[(g * NSA_GROUP + r) * 3 + c, lo:lo + HEAD_DIM] = 1.0
    return jnp.asarray(e)


def _prompt_tables(rel_bias, t):
    n_qb = t // Q_BLOCK
    n_ch = t // CMP_STRIDE
    tk = min(SEL_CHUNK, t)
    voff = max(t - Q_BLOCK, WINDOW)
    nv = voff // LANES + tk // LANES
    wd = nv * LANES
    wv = _bias_lookup(rel_bias, np.concatenate([voff - np.arange(wd), voff + np.arange(Q_BLOCK, 0, -1)]))
    gr = _skew(wv, Q_BLOCK, 1, wd).reshape(NSA_HEADS, Q_BLOCK, nv, LANES).transpose(2, 0, 1, 3)
    wc = _bias_lookup(rel_bias, np.concatenate([np.arange(t) - (CMP_BLOCK - 1), np.zeros(CMP_STRIDE * n_ch, np.int64)]))
    cend = np.arange(n_ch) * CMP_STRIDE + (CMP_BLOCK - 1)
    cmask = np.where(np.arange(t)[:, None] >= cend[None, :], 0.0, NEG_INF).astype(np.float32)
    bc = _skew(wc, n_ch, CMP_STRIDE, t).transpose(0, 2, 1) + cmask[None]
    bc = bc.reshape(NSA_HEADS, n_qb, Q_BLOCK, n_ch).transpose(1, 0, 2, 3)
    n_w = WINDOW + Q_BLOCK
    ww = _bias_lookup(rel_bias, np.concatenate([WINDOW - np.arange(n_w), WINDOW + np.arange(Q_BLOCK, 0, -1)]))
    dist_w = WINDOW + np.arange(Q_BLOCK)[:, None] - np.arange(n_w)[None, :]
    wmask = np.where((dist_w >= 0) & (dist_w < WINDOW), 0.0, NEG_INF).astype(np.float32)
    wt = _skew(ww, Q_BLOCK, 1, n_w) + wmask[None]
    return dict(gr=gr, bc=bc, wt=wt, voff=voff,
                mimp=_importance_matrix(n_ch, n_ch - 1, t // SEL_BLOCK, LANES),
                esel=_block_expand(t // SEL_BLOCK), eg=_gate_expand())


def _sample_tables(rel_bias, past, s_new, w_buf):
    n_ch = past // CMP_STRIDE
    n_sel = (past + s_new + SEL_BLOCK - 1) // SEL_BLOCK
    per_group = lambda a: a.reshape(NSA_KV_HEADS, NSA_GROUP, s_new, -1).reshape(NSA_KV_HEADS, NSA_GROUP * s_new, -1)
    qi = np.arange(s_new)

    def table(dist):
        return per_group(_bias_lookup(rel_bias, dist))
    ws = _bias_lookup(rel_bias, np.concatenate([past - np.arange(past), past + np.arange(s_new, 0, -1)]))
    gs = per_group(_skew(ws, s_new, 1, past))
    jn = np.arange(LANES)
    gn = table(np.where(jn[None, :] < s_new, qi[:, None] - jn[None, :], 0))
    cend = np.arange(n_ch) * CMP_STRIDE + (CMP_BLOCK - 1)
    bcs = table(past + qi[:, None] - cend[None, :])
    bws = table(w_buf + qi[:, None] - np.arange(w_buf)[None, :])
    sel_lanes = -(-n_sel // LANES) * LANES
    return dict(gs=gs, gn=gn, bcs=bcs, bws=bws,
                mimp_s=_importance_matrix(n_ch, n_ch - 1, n_sel, sel_lanes),
                esel_s=_block_expand(past // SEL_BLOCK, past // SEL_BLOCK), eg=_gate_expand())


def _compress_weights(pe, w1, w2):
    rc = CMP_BLOCK // CMP_STRIDE
    w1r = w1.reshape(rc, CMP_STRIDE, HEAD_DIM, CMP_HIDDEN)
    eye = jnp.eye(NSA_KV_HEADS, dtype=w1.dtype)
    wcat = jnp.einsum("rldh,ge->lgdreh", w1r, eye).reshape(CMP_STRIDE // 2, 2 * KV_WIDTH,
                                                           rc * NSA_KV_HEADS * CMP_HIDDEN)
    w2bd = jnp.einsum("hd,ge->ghed", w2, eye).reshape(NSA_KV_HEADS * CMP_HIDDEN, KV_WIDTH)
    pe_rows = jnp.concatenate([pe.reshape(1, CMP_BLOCK * HEAD_DIM),
                               jnp.zeros((SUBLANES - 1, CMP_BLOCK * HEAD_DIM), pe.dtype)], axis=0)
    return wcat.astype(BF16), pe_rows, w1.reshape(CMP_BLOCK * HEAD_DIM, CMP_HIDDEN), w2bd.astype(BF16)


def _layer_weights(w_in, gla_w_alpha, gla_b_alpha, w_branch_nsa, norm_ffn, router_w, router_b):
    offs = np.cumsum((0,) + IN_SPLITS)
    col = lambda j: w_in[:, offs[j]:offs[j + 1]]
    q_perm = col(0).reshape(D_MODEL, NSA_KV_HEADS, NSA_GROUP, HEAD_DIM).transpose(0, 2, 1, 3).reshape(D_MODEL, NSA_WIDTH)
    pad = jnp.zeros((D_MODEL, LG_PAD - 3 * NSA_HEADS - GLA_RANK), w_in.dtype)
    w_p = jnp.concatenate([q_perm, col(1), col(3), col(4), col(5), col(7), col(8), col(9), col(2), col(6), pad],
                          axis=1).astype(BF16)
    w_al = jnp.zeros((LG_PAD, GLA_K_WIDTH), F32).at[3 * NSA_HEADS:3 * NSA_HEADS + GLA_RANK].set(gla_w_alpha)
    wn = w_branch_nsa.reshape(NSA_KV_HEADS, NSA_GROUP, HEAD_DIM, D_MODEL).transpose(1, 0, 2, 3).reshape(NSA_WIDTH, D_MODEL)
    rw = jnp.concatenate([router_w, jnp.zeros((D_MODEL, LANES - N_EXPERTS), F32)], axis=1)
    rb = jnp.concatenate([router_b, jnp.full((LANES - N_EXPERTS,), NEG_INF, F32)]).reshape(1, LANES)
    return w_p, w_al, gla_b_alpha.reshape(1, GLA_K_WIDTH), wn.astype(BF16), norm_ffn.reshape(1, D_MODEL), rw, rb


def _moe(h2t, ei, gw, rk, counts, x1, g_final, ew):
    n = x1.shape[0]
    nk = n * TOP_K
    counts = counts[0, :N_EXPERTS].astype(I32)
    padded = (counts + MOE_ROWS - 1) // MOE_ROWS * MOE_ROWS
    pends = jnp.cumsum(padded)
    pstarts = pends - padded
    n_blocks = (nk + N_EXPERTS * (MOE_ROWS - 1) + MOE_ROWS - 1) // MOE_ROWS
    blk_start = jnp.arange(n_blocks, dtype=I32) * MOE_ROWS
    block_e = jnp.minimum(jnp.sum((pends[None, :] <= blk_start[:, None]).astype(I32), axis=1), N_EXPERTS - 1)
    n_used = (pends[-1] // MOE_ROWS).astype(I32).reshape(1)
    e_sel = ei[:, :TOP_K, None] == jnp.arange(N_EXPERTS, dtype=I32)
    dest = jnp.sum(jnp.where(e_sel, pstarts.astype(I32), 0), axis=-1) + rk[:, :TOP_K]
    dest2d = dest.reshape(nk // LANES, LANES)
    meta = jnp.concatenate([counts, pstarts, padded, n_used]).astype(I32)
    td = min(TD_DISPATCH, n)
    xs = _dispatch(meta, dest2d, h2t, n_blocks * MOE_ROWS, td)
    y_rows = _moe_experts(block_e, n_used, xs, ew, n_blocks)
    return _combine(dest2d, gw, x1, g_final, y_rows)


def kernel(x_prompt, x_sample, cache_cmp_k, cache_cmp_v, cache_sel_k, cache_sel_v, state_win_k, state_win_v, state_gla, page_table, rel_bias, norm_mix, w_in, nsa_pe_k, nsa_pe_v, nsa_w1_k, nsa_w1_v, nsa_w2_k, nsa_w2_v, gla_w_alpha, gla_b_alpha, gla_norm, w_branch_nsa, w_branch_gla, w_out, norm_ffn, router_w, router_b, exp_w_gate, exp_b_gate, exp_w_up, exp_b_up, exp_w_down, exp_b_down, norm_final):
    depth = w_in.shape[0]
    assert depth == 1, "single-layer trunk"
    bsz, t, d = x_prompt.shape
    n_seq, s_new, _ = x_sample.shape
    n_pages = page_table.shape[1]
    past = n_pages * PAGE_SIZE
    w_buf = state_win_k.shape[2]
    assert d == D_MODEL and t % Q_BLOCK == 0 and t % GLA_CHUNK == 0

    w_p, w_al, b_al, wn, gf, rw, rb = _layer_weights(w_in[0], gla_w_alpha[0], gla_b_alpha[0], w_branch_nsa[0],
                                                     norm_ffn[0], router_w[0], router_b[0])
    g_mix = norm_mix[0].reshape(1, D_MODEL)
    g_fin = norm_final.reshape(1, D_MODEL)
    g_gla = gla_norm[0].reshape(1, GLA_DV)
    mw = dict(wn=wn, wg=w_branch_gla[0].astype(BF16), wo=w_out[0].astype(BF16), gf=gf, rw=rw, rb=rb)
    ew = dict(wg=exp_w_gate[0], bg=exp_b_gate[0].reshape(N_EXPERTS, 1, D_FF),
              wu=exp_w_up[0], bu=exp_b_up[0].reshape(N_EXPERTS, 1, D_FF),
              wd=exp_w_down[0], bd=exp_b_down[0].reshape(N_EXPERTS, 1, D_MODEL))
    cw = {}
    for nm, pe, w1, w2 in (("k", nsa_pe_k[0], nsa_w1_k[0], nsa_w2_k[0]), ("v", nsa_pe_v[0], nsa_w1_v[0], nsa_w2_v[0])):
        cw["wcat_" + nm], cw["pe_" + nm], cw["w1f_" + nm], cw["w2_" + nm] = _compress_weights(pe, w1, w2)

    xp = x_prompt.reshape(bsz * t, d)
    (q4, kck, kcv, ksk, ksv, kwk, kwv, gates, q_l, k_l, v_l, lg, r_l, m_a, m_b, *kv_t) = _in_projection(
        xp, g_mix, w_p, w_al, b_al, seq_len=t)
    kc, vc = _compress_prompt(kck, kcv, cw, bsz, t)
    o_nsa = _nsa_prompt(q4, gates, kc, vc, ksk, ksv, kwk, kwv, _prompt_tables(rel_bias, t), bsz, t)
    s_zero = jnp.zeros((bsz * GLA_HEADS * GLA_DK, GLA_DV), F32)
    o_gla, p_gla = _gla(q_l, k_l, v_l, lg, r_l, g_gla, s_zero, bsz, t // GLA_CHUNK, GLA_CHUNK)
    x1, h2t, ei, gw, rk, counts = _merge(xp, o_nsa, o_gla, m_a, m_b, mw)
    y_prompt = _moe(h2t, ei, gw, rk, counts, x1, g_fin, ew).reshape(bsz, t, d)

    w_len = min(WINDOW, t)
    rows_of = lambda a: a.reshape(bsz, NSA_KV_HEADS, HEAD_DIM, t).transpose(0, 3, 1, 2)[None]
    p_states = tuple(rows_of(a) for a in kv_t[:4]) + tuple(rows_of(a)[:, :, t - w_len:] for a in kv_t[4:]) + (
        p_gla.reshape(1, bsz, GLA_HEADS, GLA_DK, GLA_DV),)

    xs = x_sample.reshape(n_seq * s_new, d)
    (q4, kck, kcv, ksk, ksv, kwk, kwv, gates, q_l, k_l, v_l, lg, r_l, m_a, m_b) = _in_projection(xs, g_mix, w_p, w_al, b_al)
    pt_flat = page_table.reshape(n_seq * n_pages).astype(I32)
    pool = lambda c: c[0].transpose(0, 2, 3, 1).reshape(-1, PAGE_SIZE)
    kc, vc = _compress_sample(pt_flat, pool(cache_cmp_k), pool(cache_cmp_v), cw, n_seq, n_pages)
    buf_kw = state_win_k[0].transpose(0, 2, 3, 1).reshape(n_seq * KV_WIDTH, w_buf)
    buf_vw = state_win_v[0].transpose(0, 2, 3, 1).reshape(n_seq * KV_WIDTH, w_buf)
    o_nsa = _nsa_sample(pt_flat, q4, gates, kc, vc, ksk, ksv, kwk, kwv, buf_kw, buf_vw,
                        pool(cache_sel_k), pool(cache_sel_v), _sample_tables(rel_bias, past, s_new, w_buf),
                        n_seq, s_new, n_pages)
    cl = 16
    padc = lambda a: jnp.pad(a.reshape(n_seq, s_new, -1), ((0, 0), (0, cl - s_new), (0, 0))).reshape(n_seq * cl, -1)
    s_in = state_gla[0].reshape(n_seq * GLA_HEADS * GLA_DK, GLA_DV)
    o_gla, s_gla = _gla(padc(q_l), padc(k_l), padc(v_l), padc(lg), padc(r_l), g_gla, s_in, n_seq, 1, cl)
    o_gla = o_gla.reshape(n_seq, cl, GLA_V_WIDTH)[:, :s_new].reshape(n_seq * s_new, GLA_V_WIDTH)
    x1, h2t, ei, gw, rk, counts = _merge(xs, o_nsa, o_gla, m_a, m_b, mw)
    y_sample = _moe(h2t, ei, gw, rk, counts, x1, g_fin, ew).reshape(n_seq, s_new, d)

    kvs = (1, n_seq, s_new, NSA_KV_HEADS, HEAD_DIM)
    new_win = lambda buf, new: jnp.concatenate([buf, new.reshape(kvs).astype(buf.dtype)], axis=2)[:, :, s_new:]
    s_states = (kck.reshape(kvs), kcv.reshape(kvs), ksk.reshape(kvs), ksv.reshape(kvs),
                new_win(state_win_k, kwk), new_win(state_win_v, kwv),
                s_gla.reshape(1, n_seq, GLA_HEADS, GLA_DK, GLA_DV))
    return (y_prompt, y_sample) + p_states + s_states
```

```python
import functools
import math

import numpy as np
import jax
import jax.numpy as jnp
from jax import lax
from jax.experimental import pallas as pl
from jax.experimental.pallas import tpu as pltpu

F32 = jnp.float32
BF16 = jnp.bfloat16
I32 = jnp.int32
HI = lax.Precision.HIGHEST

D_MODEL = 1024
PAGE_SIZE = 128
NSA_HEADS = 8
NSA_KV_HEADS = 2
NSA_GROUP = NSA_HEADS // NSA_KV_HEADS
HEAD_DIM = 64
NSA_WIDTH = NSA_HEADS * HEAD_DIM
KV_WIDTH = NSA_KV_HEADS * HEAD_DIM
CMP_BLOCK = 32
CMP_STRIDE = 16
CMP_HIDDEN = 2 * HEAD_DIM
SEL_BLOCK = 64
N_SEL = 16
WINDOW = 512
Q_BLOCK = 128
SEL_FORCE = 1e9
GLA_HEADS = 4
GLA_DK = 64
GLA_DV = 128
GLA_K_WIDTH = GLA_HEADS * GLA_DK
GLA_V_WIDTH = GLA_HEADS * GLA_DV
GLA_RANK = 16
GLA_TAU = 16.0
GLA_CHUNK = 64
N_EXPERTS = 32
TOP_K = 4
D_FF = D_MODEL
SWIGLU_ALPHA = 1.702
SWIGLU_LIMIT = 7.0
REL_BUCKETS = 32
REL_EXACT = REL_BUCKETS // 2
REL_MAX_DIST = 1024
RMS_EPS = 1e-6
NEG_INF = -1e30
IN_SPLITS = (NSA_WIDTH, 6 * KV_WIDTH, 3 * NSA_HEADS, GLA_K_WIDTH, GLA_K_WIDTH, GLA_V_WIDTH, GLA_RANK,
             GLA_V_WIDTH, D_MODEL, D_MODEL)

LANES = 128
SUBLANES = 8
VMEM_LIMIT = 56 * 1024 * 1024

TM_PROJ = 256
MOE_ROWS = 256
TD_DISPATCH = 512
TC_COMBINE = 256
SEL_CHUNK = 512
PIPE_HEADS = 2
LG_PAD = LANES


def _nt(a, b, **kw):
    return lax.dot_general(a, b, (((1,), (1,)), ((), ())), preferred_element_type=F32, **kw)


def _tn(a, b, **kw):
    return lax.dot_general(a, b, (((0,), (0,)), ((), ())), preferred_element_type=F32, **kw)


def _dot(a, b, **kw):
    return jnp.dot(a, b, preferred_element_type=F32, **kw)


def _params(sem, vmem=VMEM_LIMIT):
    return pltpu.CompilerParams(dimension_semantics=sem, vmem_limit_bytes=vmem)


def _masked_softmax_parts(s, valid):
    s = jnp.where(valid, s, NEG_INF)
    m = jnp.max(s, axis=-1, keepdims=True)
    e = jnp.where(valid, jnp.exp(s - m), 0.0)
    return e, jnp.maximum(jnp.sum(e, axis=-1, keepdims=True), 1e-20)


_OFF_Q = 0
_OFF_KV = _OFF_Q + NSA_WIDTH
_OFF_QL = _OFF_KV + 6 * KV_WIDTH
_OFF_KL = _OFF_QL + GLA_K_WIDTH
_OFF_VL = _OFF_KL + GLA_K_WIDTH
_OFF_R = _OFF_VL + GLA_V_WIDTH
_OFF_MA = _OFF_R + GLA_V_WIDTH
_OFF_MB = _OFF_MA + D_MODEL
_OFF_GA = _OFF_MB + D_MODEL
_N_PROJ = _OFF_GA + LG_PAD


def _inproj_kernel(x_ref, g_ref, w_ref, wal_ref, bal_ref,
                   q_o, kck_o, kcv_o, ksk_o, ksv_o, kwk_o, kwv_o, gt_o, ql_o, kl_o, vl_o, lg_o, r_o, ma_o, mb_o,
                   *kv_t_o):
    x = x_ref[...]
    xn = x * lax.rsqrt(jnp.mean(x * x, axis=-1, keepdims=True) + RMS_EPS)
    xn = (xn * g_ref[...]).astype(BF16)

    def mm(lo, n):
        return _dot(xn, w_ref[:, lo:lo + n])

    q = mm(_OFF_Q, NSA_WIDTH) * (HEAD_DIM ** -0.5)
    for r in range(NSA_GROUP):
        q_o[r] = q[:, r * LANES:(r + 1) * LANES]
    kv = mm(_OFF_KV, 6 * KV_WIDTH)
    for j, o in enumerate((kck_o, kcv_o, ksk_o, ksv_o, kwk_o, kwv_o)):
        o[...] = kv[:, j * KV_WIDTH:(j + 1) * KV_WIDTH]
    for j, o in enumerate(kv_t_o):
        o[...] = kv[:, j * KV_WIDTH:(j + 1) * KV_WIDTH].T
    ql_o[...] = mm(_OFF_QL, GLA_K_WIDTH) * (GLA_DK ** -0.5)
    kl_o[...] = mm(_OFF_KL, GLA_K_WIDTH)
    vl_o[...] = mm(_OFF_VL, GLA_V_WIDTH)
    r_o[...] = mm(_OFF_R, GLA_V_WIDTH)
    ma_o[...] = mm(_OFF_MA, D_MODEL)
    mb_o[...] = mm(_OFF_MB, D_MODEL)
    ga = mm(_OFF_GA, LG_PAD)
    gt_o[...] = jax.nn.sigmoid(ga)
    al = _dot(ga, wal_ref[...], precision=HI) + bal_ref[...]
    lg_o[...] = (jnp.minimum(al, 0.0) - jnp.log1p(jnp.exp(-jnp.abs(al)))) * (1.0 / GLA_TAU)


def _in_projection(x, norm_g, w_p, w_al, b_al, seq_len=None):
    n = x.shape[0]
    tm = TM_PROJ
    assert n % tm == 0
    row = lambda w: pl.BlockSpec((tm, w), lambda i: (i, 0))
    full = lambda a: pl.BlockSpec(a.shape, lambda i: (0,) * a.ndim)
    widths = (KV_WIDTH,) * 6 + (LG_PAD, GLA_K_WIDTH, GLA_K_WIDTH, GLA_V_WIDTH, GLA_K_WIDTH, GLA_V_WIDTH,
                                 D_MODEL, D_MODEL)
    out_shape = [jax.ShapeDtypeStruct((NSA_GROUP, n, LANES), F32)] + [jax.ShapeDtypeStruct((n, w), F32) for w in widths]
    out_specs = [pl.BlockSpec((NSA_GROUP, tm, LANES), lambda i: (0, i, 0))] + [row(w) for w in widths]
    if seq_len is not None:
        assert seq_len % tm == 0
        per_seq = seq_len // tm
        out_shape += [jax.ShapeDtypeStruct((n // seq_len * KV_WIDTH, seq_len), F32)] * 6
        out_specs += [pl.BlockSpec((KV_WIDTH, tm), lambda i: (i // per_seq, i % per_seq))] * 6
    return pl.pallas_call(
        _inproj_kernel,
        grid=(n // tm,),
        in_specs=[row(D_MODEL), full(norm_g), full(w_p), full(w_al), full(b_al)],
        out_specs=out_specs,
        out_shape=out_shape,
        compiler_params=_params(("arbitrary",)),
        name="in_projection",
    )(x, norm_g, w_p, w_al, b_al)


def _gelu_tanh(x):
    return 0.5 * x * (1.0 + jnp.tanh(math.sqrt(2.0 / math.pi) * (x + 0.044715 * (x * x * x))))


def _compress_rows(src, n_ch, wcat_ref, pe_ref, w1f_ref, w2_ref):
    hid2 = NSA_KV_HEADS * CMP_HIDDEN
    acc = jnp.zeros((n_ch, 2 * hid2), F32)
    for l in range(0, CMP_STRIDE, 2):
        xl = jnp.concatenate([src[pl.ds(l, n_ch, stride=CMP_STRIDE), :],
                              src[pl.ds(l + 1, n_ch, stride=CMP_STRIDE), :]], axis=1).astype(BF16)
        acc = acc + _dot(xl, wcat_ref[l // 2])
    bias = _dot(pe_ref[...], w1f_ref[...], precision=HI)[0:1]
    bias2 = jnp.concatenate([bias] * NSA_KV_HEADS, axis=1)
    nxt = pltpu.roll(acc[:, hid2:], n_ch - 1, 0)
    h = acc[:, :hid2] + nxt + bias2
    return _dot(_gelu_tanh(h).astype(BF16), w2_ref[...])


def _compress_prompt_kernel(k_ref, v_ref, wk_ref, wv_ref, pek_ref, pev_ref, w1k_ref, w1v_ref, w2k_ref, w2v_ref,
                            kc_o, vc_o):
    n_ch = kc_o.shape[0]
    kc_o[...] = _compress_rows(k_ref, n_ch, wk_ref, pek_ref, w1k_ref, w2k_ref)
    vc_o[...] = _compress_rows(v_ref, n_ch, wv_ref, pev_ref, w1v_ref, w2v_ref)


def _compress_prompt(k_cmp, v_cmp, cw, bsz, t):
    n_ch = t // CMP_STRIDE
    full = lambda a: pl.BlockSpec(a.shape, lambda b: (0,) * a.ndim)
    seq = pl.BlockSpec((t, KV_WIDTH), lambda b: (b, 0))
    out = pl.BlockSpec((n_ch, KV_WIDTH), lambda b: (b, 0))
    ws = (cw["wcat_k"], cw["wcat_v"], cw["pe_k"], cw["pe_v"], cw["w1f_k"], cw["w1f_v"], cw["w2_k"], cw["w2_v"])
    return pl.pallas_call(
        _compress_prompt_kernel,
        grid=(bsz,),
        in_specs=[seq, seq] + [full(w) for w in ws],
        out_specs=[out, out],
        out_shape=[jax.ShapeDtypeStruct((bsz * n_ch, KV_WIDTH), F32)] * 2,
        compiler_params=_params(("arbitrary",)),
        name="nsa_compress_prompt",
    )(k_cmp, v_cmp, *ws)


def _paged_fetch(pt_ref, pools, bufs, sems, seq, slot, n_pages, pages_on_lanes):
    def body(p, _):
        pg = pt_ref[seq * n_pages + p]
        off = pl.multiple_of(p * PAGE_SIZE, PAGE_SIZE)
        for j, (pool, buf) in enumerate(zip(pools, bufs)):
            dst = buf.at[slot, :, pl.ds(off, PAGE_SIZE)] if pages_on_lanes else buf.at[slot, pl.ds(off, PAGE_SIZE)]
            pltpu.make_async_copy(pool.at[pl.ds(pg * KV_WIDTH, KV_WIDTH)], dst, sems.at[j, slot]).start()
        return 0
    lax.fori_loop(0, n_pages, body, 0)


def _paged_wait(bufs, sems, slot):
    for j, buf in enumerate(bufs):
        pltpu.make_async_copy(buf.at[slot], buf.at[slot], sems.at[j, slot]).wait()


def _compress_sample_kernel(pt_ref, pk_hbm, pv_hbm, wk_ref, wv_ref, pek_ref, pev_ref, w1k_ref, w1v_ref,
                            w2k_ref, w2v_ref, kc_o, vc_o, bufk, bufv, sems, rows_k, rows_v, *, n_pages):
    s = pl.program_id(0)
    n_seq = pl.num_programs(0)
    slot = s % 2
    pools, bufs = (pk_hbm, pv_hbm), (bufk, bufv)

    @pl.when(s == 0)
    def _():
        _paged_fetch(pt_ref, pools, bufs, sems, s, slot, n_pages, False)

    @pl.when(s + 1 < n_seq)
    def _():
        _paged_fetch(pt_ref, pools, bufs, sems, s + 1, 1 - slot, n_pages, False)

    _paged_wait(bufs, sems, slot)
    for p in range(n_pages):
        rows = slice(p * PAGE_SIZE, (p + 1) * PAGE_SIZE)
        rows_k[rows, :] = bufk[slot, rows, :].T
        rows_v[rows, :] = bufv[slot, rows, :].T
    n_ch = kc_o.shape[0]
    kc_o[...] = _compress_rows(rows_k, n_ch, wk_ref, pek_ref, w1k_ref, w2k_ref)
    vc_o[...] = _compress_rows(rows_v, n_ch, wv_ref, pev_ref, w1v_ref, w2v_ref)


def _compress_sample(pt_flat, pool_k, pool_v, cw, n_seq, n_pages):
    past = n_pages * PAGE_SIZE
    n_ch = past // CMP_STRIDE
    full = lambda a: pl.BlockSpec(a.shape, lambda s, pt: (0,) * a.ndim)
    hbm = pl.BlockSpec(memory_space=pl.ANY)
    out = pl.BlockSpec((n_ch, KV_WIDTH), lambda s, pt: (s, 0))
    ws = (cw["wcat_k"], cw["wcat_v"], cw["pe_k"], cw["pe_v"], cw["w1f_k"], cw["w1f_v"], cw["w2_k"], cw["w2_v"])
    return pl.pallas_call(
        functools.partial(_compress_sample_kernel, n_pages=n_pages),
        grid_spec=pltpu.PrefetchScalarGridSpec(
            num_scalar_prefetch=1,
            grid=(n_seq,),
            in_specs=[hbm, hbm] + [full(w) for w in ws],
            out_specs=[out, out],
            scratch_shapes=[pltpu.VMEM((2, past, KV_WIDTH), F32), pltpu.VMEM((2, past, KV_WIDTH), F32),
                            pltpu.SemaphoreType.DMA((2, 2)),
                            pltpu.VMEM((past, KV_WIDTH), F32), pltpu.VMEM((past, KV_WIDTH), F32)],
        ),
        out_shape=[jax.ShapeDtypeStruct((n_seq * n_ch, KV_WIDTH), F32)] * 2,
        compiler_params=_params(("arbitrary",)),
        name="nsa_compress_sample",
    )(pt_flat, pool_k, pool_v, *ws)


def _select_blocks(score, n_sel):
    blk = lax.broadcasted_iota(I32, score.shape, 1)
    cnt = jnp.zeros(score.shape, F32)
    for i in range(n_sel):
        col = score[:, i:i + 1]
        ahead = (col > score) | ((col == score) & (i < blk))
        cnt = cnt + jnp.where(ahead, 1.0, 0.0)
    n_top = min(N_SEL, n_sel)
    return jnp.where((cnt < n_top) & (blk < n_sel), 1.0, 0.0)


def _select_blocks_t(score_t, n_sel):
    blk = lax.broadcasted_iota(I32, score_t.shape, 0)
    cnt = jnp.zeros(score_t.shape, F32)
    for i in range(n_sel):
        row = jnp.broadcast_to(score_t[i:i + 1, :], score_t.shape)
        ahead = (row > score_t) | ((row == score_t) & (i < blk))
        cnt = cnt + jnp.where(ahead, 1.0, 0.0)
    return jnp.where(cnt < min(N_SEL, n_sel), 1.0, 0.0)


def _block_scores(imp, qpos, n_sel):
    blk = lax.broadcasted_iota(I32, imp.shape, 1)
    cur = qpos // SEL_BLOCK
    forced = (blk == 0) | (blk == cur) | (blk == cur - 1)
    causal = (blk * SEL_BLOCK) <= qpos
    score = jnp.where(causal, jnp.where(forced, SEL_FORCE, imp), -SEL_FORCE)
    return jnp.where(blk < n_sel, score, -3e38)


def _gate_mix(gexp, o_c, o_s, o_w, r):
    out = None
    for c, o in enumerate((o_c, o_s, o_w)):
        term = gexp[:, c * NSA_WIDTH + r * LANES:c * NSA_WIDTH + (r + 1) * LANES] * o
        out = term if out is None else out + term
    return out


def _nsa_prompt_kernel(q_ref, gt_ref, kc_ref, vc_ref, ks_ref, vs_ref, kw_ref, vw_ref,
                       gr_ref, bc_ref, wt_ref, mimp_ref, esel_ref, eg_ref, o_ref,
                       ksb, vsb, kwb, vwb, kcb, vcb, oc_s, msk_s, qs, *state, voff_blk):
    qb = pl.program_id(1)
    t = ks_ref.shape[0]
    n_ch = kc_ref.shape[0]
    n_sel = t // SEL_BLOCK
    tk = msk_s.shape[3]
    n_kc = t // tk
    tiles = tk // LANES

    @pl.when(qb == 0)
    def _():
        ksb[...] = ks_ref[...].astype(BF16)
        vsb[...] = vs_ref[...].astype(BF16)
        kwb[0:WINDOW, :] = jnp.zeros((WINDOW, KV_WIDTH), BF16)
        vwb[0:WINDOW, :] = jnp.zeros((WINDOW, KV_WIDTH), BF16)
        kwb[WINDOW:, :] = kw_ref[...].astype(BF16)
        vwb[WINDOW:, :] = vw_ref[...].astype(BF16)
        kcb[...] = kc_ref[...].astype(BF16)
        vcb[...] = vc_ref[...].astype(BF16)

    q0 = qb * Q_BLOCK
    lane = lax.broadcasted_iota(I32, (Q_BLOCK, LANES), 1)
    qpos = q0 + lax.broadcasted_iota(I32, (Q_BLOCK, 1), 0)
    upper = lane >= HEAD_DIM

    nh = NSA_HEADS
    rows = nh * Q_BLOCK
    for h in range(nh):
        keep = upper if h >= NSA_GROUP else jnp.logical_not(upper)
        qs[h * Q_BLOCK:(h + 1) * Q_BLOCK, :] = jnp.where(keep, q_ref[h % NSA_GROUP], 0.0).astype(BF16)
    q_all = qs[...]

    def per_group(x3, add2):
        return jnp.concatenate([x3[g * NSA_GROUP:(g + 1) * NSA_GROUP] + add2[g][None]
                                for g in range(NSA_KV_HEADS)], axis=0)

    any_c = (qpos >= (CMP_BLOCK - 1))[None]
    s = _nt(q_all, kcb[...]).reshape(nh, Q_BLOCK, n_ch) + bc_ref[0]
    e = jnp.exp(s - jnp.max(s, axis=-1, keepdims=True))
    den = jnp.maximum(jnp.sum(e, axis=-1, keepdims=True), 1e-20)
    p = e * jnp.where(any_c, 1.0 / den, 0.0)
    oc_s[...] = _dot(p.reshape(rows, n_ch).astype(BF16), vcb[...])
    q_end = q0 + Q_BLOCK
    must_rank = (q_end - 1) // SEL_BLOCK + 1 > N_SEL
    for c in range(n_kc):
        kpos = c * tk + lax.broadcasted_iota(I32, (Q_BLOCK, tk), 1)

        @pl.when((c * tk < q_end) & jnp.logical_not(must_rank))
        def _(c=c, kpos=kpos):
            causal_only = jnp.where(kpos <= qpos, 0.0, NEG_INF)
            for g in range(NSA_KV_HEADS):
                msk_s[g, c] = causal_only

    @pl.when(must_rank)
    def _():
        for g in range(NSA_KV_HEADS):
            psum = p[g * NSA_GROUP]
            for r in range(1, NSA_GROUP):
                psum = psum + p[g * NSA_GROUP + r]
            imp = _dot(psum, mimp_ref[...], precision=HI)
            sel_t = _select_blocks_t(_block_scores(imp, qpos, n_sel).T[0:n_sel], n_sel).astype(BF16)
            for c in range(n_kc):
                @pl.when(c * tk < q_end)
                def _(c=c, g=g, sel_t=sel_t):
                    kpos = c * tk + lax.broadcasted_iota(I32, (Q_BLOCK, tk), 1)
                    picked = _tn(sel_t, esel_ref[0:n_sel, c * tk:(c + 1) * tk]) > 0.5
                    msk_s[g, c] = jnp.where(picked & (kpos <= qpos), 0.0, NEG_INF)

    ph = PIPE_HEADS
    nb = nh // ph
    brow = ph * Q_BLOCK
    m_r, l_r, a_r = state[0:nb], state[nb:2 * nb], state[2 * nb:3 * nb]
    for b in range(nb):
        m_r[b][...] = jnp.full((brow, 1), NEG_INF, F32)
        l_r[b][...] = jnp.zeros((brow, 1), F32)
        a_r[b][...] = jnp.zeros((brow, KV_WIDTH), F32)
    q_blk = lambda b: qs[b * brow:(b + 1) * brow, :]

    def chunk(kt, _):
        k0 = pl.multiple_of(kt * tk, tk)
        kk = ksb[pl.ds(k0, tk), :]
        vv = vsb[pl.ds(k0, tk), :]
        base = voff_blk - qb + kt * tiles

        def scores(b):
            bias = jnp.concatenate([gr_ref[base + j, b * ph:(b + 1) * ph] for j in range(tiles)], axis=-1)
            s = _nt(q_blk(b), kk).reshape(ph, Q_BLOCK, tk) + bias + msk_s[(b * ph) // NSA_GROUP, kt][None]
            return s.reshape(brow, tk)

        s_next = scores(0)
        for b in range(nb):
            s = s_next
            if b + 1 < nb:
                s_next = scores(b + 1)
            m_old = m_r[b][...]
            m_new = jnp.maximum(m_old, jnp.max(s, axis=-1, keepdims=True))
            alpha = jnp.exp(m_old - m_new)
            e = jnp.exp(s - m_new)
            l_r[b][...] = alpha * l_r[b][...] + jnp.sum(e, axis=-1, keepdims=True)
            m_r[b][...] = m_new
            a_r[b][...] = alpha * a_r[b][...] + _dot(e.astype(BF16), vv)
        return 0
    lax.fori_loop(0, (q0 + Q_BLOCK + tk - 1) // tk, chunk, 0)

    n_w = WINDOW + Q_BLOCK
    wpos = q0 - WINDOW + lax.broadcasted_iota(I32, (Q_BLOCK, n_w), 1)
    before_start = jnp.where(wpos >= 0, 0.0, NEG_INF)[None]
    w0 = pl.multiple_of(q0, Q_BLOCK)
    kw = kwb[pl.ds(w0, n_w), :]
    vw = vwb[pl.ds(w0, n_w), :]

    def w_scores(b):
        s = _nt(q_blk(b), kw).reshape(ph, Q_BLOCK, n_w) + wt_ref[b * ph:(b + 1) * ph] + before_start
        return s.reshape(brow, n_w)

    o_w = []
    s_next = w_scores(0)
    for b in range(nb):
        s = s_next
        if b + 1 < nb:
            s_next = w_scores(b + 1)
        e = jnp.exp(s - jnp.max(s, axis=-1, keepdims=True))
        den = jnp.maximum(jnp.sum(e, axis=-1, keepdims=True), 1e-20)
        o_w.append(_dot(e.astype(BF16), vw) / den)
    o_w = jnp.concatenate(o_w, axis=0)
    o_s = jnp.concatenate([a_r[b][...] / jnp.maximum(l_r[b][...], 1e-20) for b in range(nb)], axis=0)
    o_c = oc_s[...]

    gexp = _dot(gt_ref[...], eg_ref[...], precision=HI)
    head = lambda x, h: x[h * Q_BLOCK:(h + 1) * Q_BLOCK]
    for r in range(NSA_GROUP):
        pick = lambda x: jnp.where(upper, head(x, NSA_GROUP + r), head(x, r))
        o_ref[r] = _gate_mix(gexp, pick(o_c), pick(o_s), pick(o_w), r)


def _nsa_prompt(q4, gates, kc, vc, ks, vs, kw, vw, tabs, bsz, t):
    n_qb = t // Q_BLOCK
    n_ch = t // CMP_STRIDE
    tk = min(SEL_CHUNK, t)
    full = lambda a: pl.BlockSpec(a.shape, lambda b, i: (0,) * a.ndim)
    seq = pl.BlockSpec((t, KV_WIDTH), lambda b, i: (b, 0))
    cseq = pl.BlockSpec((n_ch, KV_WIDTH), lambda b, i: (b, 0))
    qspec = pl.BlockSpec((NSA_GROUP, Q_BLOCK, LANES), lambda b, i: (0, b * n_qb + i, 0))
    gr, bc, wt, mimp, esel, eg = tabs["gr"], tabs["bc"], tabs["wt"], tabs["mimp"], tabs["esel"], tabs["eg"]
    head_tile = pltpu.VMEM((NSA_HEADS * Q_BLOCK, KV_WIDTH), F32)
    nb, brow = NSA_HEADS // PIPE_HEADS, PIPE_HEADS * Q_BLOCK
    per_block = [pltpu.VMEM((brow, 1), F32)] * (2 * nb) + [pltpu.VMEM((brow, KV_WIDTH), F32)] * nb
    return pl.pallas_call(
        functools.partial(_nsa_prompt_kernel, voff_blk=tabs["voff"] // LANES),
        grid=(bsz, n_qb),
        in_specs=[qspec, pl.BlockSpec((Q_BLOCK, LG_PAD), lambda b, i: (b * n_qb + i, 0)),
                  cseq, cseq, seq, seq, seq, seq, full(gr),
                  pl.BlockSpec((1,) + bc.shape[1:], lambda b, i: (i, 0, 0, 0)), full(wt),
                  full(mimp), full(esel), full(eg)],
        out_specs=qspec,
        out_shape=jax.ShapeDtypeStruct((NSA_GROUP, bsz * t, LANES), F32),
        scratch_shapes=[pltpu.VMEM((t, KV_WIDTH), BF16), pltpu.VMEM((t, KV_WIDTH), BF16),
                        pltpu.VMEM((t + WINDOW, KV_WIDTH), BF16), pltpu.VMEM((t + WINDOW, KV_WIDTH), BF16),
                        pltpu.VMEM((n_ch, KV_WIDTH), BF16), pltpu.VMEM((n_ch, KV_WIDTH), BF16),
                        head_tile,
                        pltpu.VMEM((NSA_KV_HEADS, t // tk, Q_BLOCK, tk), F32),
                        pltpu.VMEM((NSA_HEADS * Q_BLOCK, KV_WIDTH), BF16)] + per_block,
        compiler_params=_params(("arbitrary", "arbitrary")),
        name="nsa_attention_prompt",
    )(q4, gates, kc, vc, ks, vs, kw, vw, gr, bc, wt, mimp, esel, eg)


def _nsa_sample_kernel(pt_ref, q_ref, gt_ref, kc_ref, vc_ref, ksn_ref, vsn_ref, kwn_ref, vwn_ref, bkw_ref, bvw_ref,
                       pks_hbm, pvs_hbm, gs_ref, gn_ref, bcs_ref, bws_ref, mimp_ref, esel_ref, eg_ref, o_ref,
                       bufk, bufv, sems, ksb, vsb, *, n_pages):
    sq = pl.program_id(0)
    n_seq = pl.num_programs(0)
    slot = sq % 2
    pools, bufs = (pks_hbm, pvs_hbm), (bufk, bufv)
    past = n_pages * PAGE_SIZE
    s_new = ksn_ref.shape[0]
    rows = NSA_GROUP * s_new
    n_cmp_rows = kc_ref.shape[0]
    n_sel = (past + s_new + SEL_BLOCK - 1) // SEL_BLOCK
    n_past_blk = past // SEL_BLOCK
    w_buf = bkw_ref.shape[1]

    @pl.when(sq == 0)
    def _():
        _paged_fetch(pt_ref, pools, bufs, sems, sq, slot, n_pages, True)

    @pl.when(sq + 1 < n_seq)
    def _():
        _paged_fetch(pt_ref, pools, bufs, sems, sq + 1, 1 - slot, n_pages, True)

    _paged_wait(bufs, sems, slot)
    ksb[...] = bufk[slot].astype(BF16)
    vsb[...] = bufv[slot].astype(BF16)

    lane = lax.broadcasted_iota(I32, (rows, LANES), 1)
    upper = lane >= HEAD_DIM
    qi = lax.broadcasted_iota(I32, (rows, 1), 0) % s_new
    pad_new = lambda ref: jnp.concatenate([ref[...], jnp.zeros((LANES - s_new, KV_WIDTH), F32)], axis=0).astype(BF16)
    ksn, vsn, kwn, vwn = pad_new(ksn_ref), pad_new(vsn_ref), pad_new(kwn_ref), pad_new(vwn_ref)
    kcb, vcb = kc_ref[...].astype(BF16), vc_ref[...].astype(BF16)
    bkw, bvw = bkw_ref[...].astype(BF16), bvw_ref[...].astype(BF16)
    new_causal = (lane < s_new) & (lane <= qi)

    def stacked_q(g):
        keep = upper if g == 1 else jnp.logical_not(upper)
        qs = jnp.concatenate([q_ref[r] for r in range(NSA_GROUP)], axis=0)
        return jnp.where(keep, qs, 0.0).astype(BF16)

    cend = lax.broadcasted_iota(I32, (rows, n_cmp_rows), 1) * CMP_STRIDE + (CMP_BLOCK - 1)
    valid_c = (past + qi) >= cend
    o_c, psums = [], []
    for g in range(NSA_KV_HEADS):
        s = _nt(stacked_q(g), kcb) + bcs_ref[g]
        e, den = _masked_softmax_parts(s, valid_c)
        p = e / den
        o_c.append(_dot(p.astype(BF16), vcb))
        ps = p[0:s_new]
        for r in range(1, NSA_GROUP):
            ps = ps + p[r * s_new:(r + 1) * s_new]
        psums.append(ps)
    imp = _dot(jnp.concatenate(psums, axis=0), mimp_ref[...], precision=HI)
    qpos_sel = past + lax.broadcasted_iota(I32, (NSA_KV_HEADS * s_new, 1), 0) % s_new
    sel = _select_blocks(_block_scores(imp, qpos_sel, n_sel), n_sel)
    mask_past = _dot(sel[:, :n_past_blk].astype(BF16), esel_ref[...])
    mask_add = jnp.where(mask_past > 0.5, 0.0, NEG_INF)
    sel_new = sel[:, n_past_blk:n_past_blk + 1]

    gexp = _dot(gt_ref[...], eg_ref[...], precision=HI)
    o_s, o_w = [], []
    for g in range(NSA_KV_HEADS):
        qg = stacked_q(g)
        tile_rows = lambda a: jnp.concatenate([a[g * s_new:(g + 1) * s_new]] * NSA_GROUP, axis=0)
        s_p = _dot(qg, ksb[...]) + gs_ref[g] + tile_rows(mask_add)
        valid_n = new_causal & (tile_rows(sel_new) > 0.5)
        s_n = jnp.where(valid_n, _nt(qg, ksn) + gn_ref[g], NEG_INF)
        m = jnp.maximum(jnp.max(s_p, axis=-1, keepdims=True), jnp.max(s_n, axis=-1, keepdims=True))
        e_p = jnp.exp(s_p - m)
        e_n = jnp.where(valid_n, jnp.exp(s_n - m), 0.0)
        den = jnp.maximum(jnp.sum(e_p, axis=-1, keepdims=True) + jnp.sum(e_n, axis=-1, keepdims=True), 1e-20)
        o_s.append((_nt(e_p.astype(BF16), vsb[...]) + _dot(e_n.astype(BF16), vsn)) / den)
        jb = lax.broadcasted_iota(I32, (rows, w_buf), 1)
        dist_b = w_buf + qi - jb
        valid_b = (dist_b < WINDOW) & (past - w_buf + jb >= 0)
        s_b = jnp.where(valid_b, _dot(qg, bkw) + bws_ref[g], NEG_INF)
        s_n = jnp.where(new_causal, _nt(qg, kwn) + gn_ref[g], NEG_INF)
        m = jnp.maximum(jnp.max(s_b, axis=-1, keepdims=True), jnp.max(s_n, axis=-1, keepdims=True))
        e_b = jnp.where(valid_b, jnp.exp(s_b - m), 0.0)
        e_n = jnp.where(new_causal, jnp.exp(s_n - m), 0.0)
        den = jnp.maximum(jnp.sum(e_b, axis=-1, keepdims=True) + jnp.sum(e_n, axis=-1, keepdims=True), 1e-20)
        o_w.append((_nt(e_b.astype(BF16), bvw) + _dot(e_n.astype(BF16), vwn)) / den)

    up8 = upper[0:s_new]
    for r in range(NSA_GROUP):
        pick = lambda o: jnp.where(up8, o[1][r * s_new:(r + 1) * s_new], o[0][r * s_new:(r + 1) * s_new])
        o_ref[r] = _gate_mix(gexp, pick(o_c), pick(o_s), pick(o_w), r)


def _nsa_sample(pt_flat, q4, gates, kc, vc, ksn, vsn, kwn, vwn, buf_kw, buf_vw, pool_ks, pool_vs, tabs,
                n_seq, s_new, n_pages):
    past = n_pages * PAGE_SIZE
    n_ch = past // CMP_STRIDE
    w_buf = buf_kw.shape[1]
    full = lambda a: pl.BlockSpec(a.shape, lambda s, pt: (0,) * a.ndim)
    hbm = pl.BlockSpec(memory_space=pl.ANY)
    rows = lambda n, w: pl.BlockSpec((n, w), lambda s, pt: (s, 0))
    qspec = pl.BlockSpec((NSA_GROUP, s_new, LANES), lambda s, pt: (0, s, 0))
    consts = (tabs["gs"], tabs["gn"], tabs["bcs"], tabs["bws"], tabs["mimp_s"], tabs["esel_s"], tabs["eg"])
    return pl.pallas_call(
        functools.partial(_nsa_sample_kernel, n_pages=n_pages),
        grid_spec=pltpu.PrefetchScalarGridSpec(
            num_scalar_prefetch=1,
            grid=(n_seq,),
            in_specs=[qspec, rows(s_new, LG_PAD), rows(n_ch, KV_WIDTH), rows(n_ch, KV_WIDTH)]
                     + [rows(s_new, KV_WIDTH)] * 4 + [rows(KV_WIDTH, w_buf)] * 2 + [hbm, hbm]
                     + [full(c) for c in consts],
            out_specs=qspec,
            scratch_shapes=[pltpu.VMEM((2, KV_WIDTH, past), F32), pltpu.VMEM((2, KV_WIDTH, past), F32),
                            pltpu.SemaphoreType.DMA((2, 2)),
                            pltpu.VMEM((KV_WIDTH, past), BF16), pltpu.VMEM((KV_WIDTH, past), BF16)],
        ),
        out_shape=jax.ShapeDtypeStruct((NSA_GROUP, n_seq * s_new, LANES), F32),
        compiler_params=_params(("arbitrary",)),
        name="nsa_attention_sample",
    )(pt_flat, q4, gates, kc, vc, ksn, vsn, kwn, vwn, buf_kw, buf_vw, pool_ks, pool_vs, *consts)


def _gla_kernel(q_ref, k_ref, v_ref, lg_ref, r_ref, gn_ref, s0_ref, o_ref, sfin_ref, s_scr):
    c = pl.program_id(1)
    n_c = pl.num_programs(1)
    cl = q_ref.shape[0]

    @pl.when(c == 0)
    def _():
        s_scr[...] = s0_ref[...]

    lane_t = lax.broadcasted_iota(I32, (cl, LANES), 1)
    causal = lax.broadcasted_iota(I32, (cl, cl), 1) <= lax.broadcasted_iota(I32, (cl, cl), 0)
    row_s = lax.broadcasted_iota(I32, (LANES, GLA_DV), 0)
    pair = LANES // GLA_DK
    n_pairs = GLA_HEADS // pair
    b = lg_ref[...]
    row_a = lax.broadcasted_iota(I32, b.shape, 0)
    sh = 1
    while sh < cl:
        b = b + jnp.where(row_a >= sh, pltpu.roll(b, sh, 0), 0.0)
        sh *= 2
    b_last = b[cl - 1:cl, :]
    qt = q_ref[...] * jnp.exp(b)
    kp = k_ref[...]
    kt = (kp * jnp.exp(-b)).astype(BF16)
    khat = (kp * jnp.exp(b_last - b)).astype(BF16)
    s_old = [s_scr[p * LANES:(p + 1) * LANES, :] for p in range(n_pairs)]
    s_bf = [x.astype(BF16) for x in s_old]
    cols = lambda x, p: x[:, p * LANES:(p + 1) * LANES]
    vcols = lambda h: slice(h * GLA_DV, (h + 1) * GLA_DV)
    vh = [v_ref[:, vcols(h)].astype(BF16) for h in range(GLA_HEADS)]
    qm, att = [], []
    for h in range(GLA_HEADS):
        mine = (lane_t >= GLA_DK) if h % pair == 1 else (lane_t < GLA_DK)
        qm.append(jnp.where(mine, cols(qt, h // pair), 0.0).astype(BF16))
        att.append(jnp.where(causal, _nt(qm[h], cols(kt, h // pair)), 0.0).astype(BF16))
    o = [_dot(qm[h], s_bf[h // pair]) + _dot(att[h], vh[h]) for h in range(GLA_HEADS)]
    u = [_tn(cols(khat, h // pair), vh[h]) for h in range(GLA_HEADS)]
    for h in range(GLA_HEADS):
        oh = o[h] * lax.rsqrt(jnp.mean(o[h] * o[h], axis=-1, keepdims=True) + RMS_EPS) * gn_ref[...]
        rh = r_ref[:, vcols(h)]
        o_ref[:, vcols(h)] = oh * (rh * jax.nn.sigmoid(rh))
    for p in range(n_pairs):
        dec = jnp.exp(jnp.broadcast_to(cols(b_last, p), (LANES, LANES))).T
        upd = jnp.where(row_s >= GLA_DK, u[p * pair + 1], u[p * pair])
        s_scr[p * LANES:(p + 1) * LANES, :] = s_old[p] * dec + upd

    @pl.when(c == n_c - 1)
    def _():
        sfin_ref[...] = s_scr[...]


def _gla(q_l, k_l, v_l, lg, r, g_norm, s0, bsz, n_c, cl):
    srows = GLA_HEADS * GLA_DK
    blk = lambda w: pl.BlockSpec((cl, w), lambda b, c: (b * n_c + c, 0))
    st = pl.BlockSpec((srows, GLA_DV), lambda b, c: (b, 0))
    return pl.pallas_call(
        _gla_kernel,
        grid=(bsz, n_c),
        in_specs=[blk(GLA_K_WIDTH), blk(GLA_K_WIDTH), blk(GLA_V_WIDTH), blk(GLA_K_WIDTH), blk(GLA_V_WIDTH),
                  pl.BlockSpec(g_norm.shape, lambda b, c: (0, 0)), st],
        out_specs=[blk(GLA_V_WIDTH), st],
        out_shape=[jax.ShapeDtypeStruct((bsz * n_c * cl, GLA_V_WIDTH), F32),
                   jax.ShapeDtypeStruct((bsz * srows, GLA_DV), F32)],
        scratch_shapes=[pltpu.VMEM((srows, GLA_DV), F32)],
        compiler_params=_params(("arbitrary", "arbitrary")),
        name="gla",
    )(q_l, k_l, v_l, lg, r, g_norm, s0)


def _merge_kernel(x_ref, on_ref, og_ref, ma_ref, mb_ref, wn_ref, wg_ref, wo_ref, gf_ref, rw_ref, rb_ref,
                  x1_o, h2t_o, ei_o, gw_o, rk_o, cnt_o, carry):
    i = pl.program_id(0)
    tm = x_ref.shape[0]

    @pl.when(i == 0)
    def _():
        carry[...] = jnp.zeros(carry.shape, F32)

    on = jnp.concatenate([on_ref[r] for r in range(NSA_GROUP)], axis=1).astype(BF16)
    ya = _dot(on, wn_ref[...])
    yb = _dot(og_ref[...].astype(BF16), wg_ref[...])
    m = jax.nn.sigmoid(ma_ref[...]) * ya + jax.nn.sigmoid(mb_ref[...]) * yb
    x1 = x_ref[...] + _dot(m.astype(BF16), wo_ref[...])
    x1_o[...] = x1
    h2 = x1 * lax.rsqrt(jnp.mean(x1 * x1, axis=-1, keepdims=True) + RMS_EPS) * gf_ref[...]
    for s in range(D_MODEL // LANES):
        h2t_o[pl.ds(s, tm, stride=SUBLANES), :] = h2[:, s * LANES:(s + 1) * LANES]

    logits = _dot(h2, rw_ref[...], precision=HI) + rb_ref[...]
    lane = lax.broadcasted_iota(I32, (tm, LANES), 1)
    lane_f = lane.astype(F32)
    work = logits
    vals, idxs = [], []
    for _ in range(TOP_K):
        mk = jnp.max(work, axis=-1, keepdims=True)
        ik = jnp.min(jnp.where(work == mk, lane_f, float(LANES)), axis=-1, keepdims=True)
        vals.append(mk)
        idxs.append(ik)
        work = jnp.where(lane_f == ik, -jnp.inf, work)
    es = [jnp.exp(v - vals[0]) for v in vals]
    den = es[0]
    for e in es[1:]:
        den = den + e
    onehot = jnp.zeros((tm, LANES), F32)
    for ik in idxs:
        onehot = onehot + jnp.where(lane_f == ik, 1.0, 0.0)
    below = lax.broadcasted_iota(I32, (tm, tm), 1) < lax.broadcasted_iota(I32, (tm, tm), 0)
    before = _dot(jnp.where(below, 1.0, 0.0).astype(BF16), onehot.astype(BF16)) + carry[0:1, :]
    ei = jnp.zeros((tm, LANES), F32)
    gw = jnp.zeros((tm, LANES), F32)
    rk = jnp.zeros((tm, LANES), F32)
    for k in range(TOP_K):
        rank_k = jnp.sum(jnp.where(lane_f == idxs[k], before, 0.0), axis=-1, keepdims=True)
        ei = jnp.where(lane == k, idxs[k], ei)
        gw = jnp.where(lane == k, es[k] / den, gw)
        rk = jnp.where(lane == k, rank_k, rk)
    ei_o[...] = ei.astype(I32)
    gw_o[...] = gw
    rk_o[...] = rk.astype(I32)
    carry[0:1, :] = carry[0:1, :] + jnp.sum(onehot, axis=0, keepdims=True)
    cnt_o[...] = carry[...]


def _merge(x, o_nsa4, o_gla, m_a, m_b, mw):
    n = x.shape[0]
    tm = TM_PROJ
    row = lambda w: pl.BlockSpec((tm, w), lambda i: (i, 0))
    full = lambda a: pl.BlockSpec(a.shape, lambda i: (0,) * a.ndim)
    ws = (mw["wn"], mw["wg"], mw["wo"], mw["gf"], mw["rw"], mw["rb"])
    return pl.pallas_call(
        _merge_kernel,
        grid=(n // tm,),
        in_specs=[row(D_MODEL), pl.BlockSpec((NSA_GROUP, tm, LANES), lambda i: (0, i, 0)), row(GLA_V_WIDTH),
                  row(D_MODEL), row(D_MODEL)] + [full(w) for w in ws],
        out_specs=[row(D_MODEL), pl.BlockSpec((tm * SUBLANES, LANES), lambda i: (i, 0)),
                   row(LANES), row(LANES), row(LANES), pl.BlockSpec((SUBLANES, LANES), lambda i: (0, 0))],
        out_shape=[jax.ShapeDtypeStruct((n, D_MODEL), F32), jax.ShapeDtypeStruct((n * SUBLANES, LANES), F32),
                   jax.ShapeDtypeStruct((n, LANES), I32), jax.ShapeDtypeStruct((n, LANES), F32),
                   jax.ShapeDtypeStruct((n, LANES), I32), jax.ShapeDtypeStruct((SUBLANES, LANES), F32)],
        scratch_shapes=[pltpu.VMEM((SUBLANES, LANES), F32)],
        compiler_params=_params(("arbitrary",)),
        name="merge_router",
    )(x, o_nsa4, o_gla, m_a, m_b, *ws)


def _token_tile(ref, row):
    return ref.at[pl.ds(row * SUBLANES, SUBLANES)]


def _dispatch_kernel(meta_ref, dest_ref, h_ref, xs_hbm, dsm, ztile, sem_idx, sem_row, sem_pad, *, td):
    i = pl.program_id(0)
    n_steps = pl.num_programs(0)
    idx_copy = pltpu.make_async_copy(dest_ref, dsm, sem_idx)
    idx_copy.start()
    idx_copy.wait()

    per_row = LANES // TOP_K

    def body(rr, _):
        for c in range(LANES):
            tk = rr * per_row + c // TOP_K
            pltpu.make_async_copy(_token_tile(h_ref, tk), _token_tile(xs_hbm, dsm[rr, c]), sem_row).start()
        return 0
    lax.fori_loop(0, td // per_row, body, 0)

    @pl.when(i == n_steps - 1)
    def _():
        ztile[...] = jnp.zeros(ztile.shape, F32)
        blk_rows = ztile.shape[0]
        n_blocks = xs_hbm.shape[0] // blk_rows

        def pads(start_or_wait):
            def per_expert(e, _):
                first = meta_ref[N_EXPERTS + e] + meta_ref[e]
                last = meta_ref[N_EXPERTS + e] + meta_ref[2 * N_EXPERTS + e]

                def per_row(rw, _):
                    cp = pltpu.make_async_copy(ztile.at[pl.ds(0, SUBLANES)], _token_tile(xs_hbm, rw), sem_pad)
                    cp.start() if start_or_wait else cp.wait()
                    return 0
                lax.fori_loop(first, last, per_row, 0)
                return 0
            lax.fori_loop(0, N_EXPERTS, per_expert, 0)

            def per_block(bk, _):
                cp = pltpu.make_async_copy(ztile, xs_hbm.at[pl.ds(bk * blk_rows, blk_rows)], sem_pad)
                cp.start() if start_or_wait else cp.wait()
                return 0
            lax.fori_loop(meta_ref[3 * N_EXPERTS], n_blocks, per_block, 0)
        pads(True)
        pads(False)

    for _ in range(TOP_K):
        pltpu.make_async_copy(h_ref, xs_hbm.at[pl.ds(0, td * SUBLANES)], sem_row).wait()


def _dispatch(meta, dest2d, h2t, n_rows_total, td):
    n = h2t.shape[0] // SUBLANES
    hbm = pl.BlockSpec(memory_space=pl.ANY)
    drows = td * TOP_K // LANES
    return pl.pallas_call(
        functools.partial(_dispatch_kernel, td=td),
        grid_spec=pltpu.PrefetchScalarGridSpec(
            num_scalar_prefetch=1,
            grid=(n // td,),
            in_specs=[pl.BlockSpec((drows, LANES), lambda i, m: (i, 0)),
                      pl.BlockSpec((td * SUBLANES, LANES), lambda i, m: (i, 0))],
            out_specs=hbm,
            scratch_shapes=[pltpu.SMEM((drows, LANES), I32), pltpu.VMEM((MOE_ROWS * SUBLANES, LANES), F32),
                            pltpu.SemaphoreType.DMA, pltpu.SemaphoreType.DMA, pltpu.SemaphoreType.DMA],
        ),
        out_shape=jax.ShapeDtypeStruct((n_rows_total * SUBLANES, LANES), F32),
        compiler_params=_params(("arbitrary",)),
        name="moe_dispatch",
    )(meta, dest2d, h2t)


def _moe_kernel(be_ref, nu_ref, xs_ref, wg_ref, bg_ref, wu_ref, bu_ref, wd_ref, bd_ref, y_ref, wg_s, wu_s, wd_s):
    i = pl.program_id(0)
    rows = xs_ref.shape[0] // SUBLANES
    n_s = D_MODEL // LANES

    @pl.when(i < nu_ref[0])
    def _():
        @pl.when((i == 0) | (be_ref[i] != be_ref[jnp.maximum(i - 1, 0)]))
        def _():
            wg_s[...] = wg_ref[0].astype(BF16)
            wu_s[...] = wu_ref[0].astype(BF16)
            wd_s[...] = wd_ref[0].astype(BF16)

        xb = jnp.concatenate([xs_ref[pl.ds(s, rows, stride=SUBLANES), :] for s in range(n_s)], axis=1).astype(BF16)
        g = _dot(xb, wg_s[...]) + bg_ref[0]
        u = _dot(xb, wu_s[...]) + bu_ref[0]
        g = jnp.minimum(g, SWIGLU_LIMIT)
        u = jnp.clip(u, -SWIGLU_LIMIT, SWIGLU_LIMIT)
        hh = (u + 1.0) * (g * jax.nn.sigmoid(SWIGLU_ALPHA * g))
        y = _dot(hh.astype(BF16), wd_s[...]) + bd_ref[0]
        for s in range(n_s):
            y_ref[pl.ds(s, rows, stride=SUBLANES), :] = y[:, s * LANES:(s + 1) * LANES]

    @pl.when(i >= nu_ref[0])
    def _():
        y_ref[...] = jnp.zeros(y_ref.shape, F32)


def _moe_experts(block_e, n_used, xs, ew, n_blocks):
    blk = lambda i, be, nu: jnp.minimum(i, nu[0] - 1)
    rows = pl.BlockSpec((MOE_ROWS * SUBLANES, LANES), lambda i, be, nu: (i, 0))
    wspec = lambda a: pl.BlockSpec((1,) + a.shape[1:], lambda i, be, nu: (be[blk(i, be, nu)], 0, 0))
    ws = (ew["wg"], ew["bg"], ew["wu"], ew["bu"], ew["wd"], ew["bd"])
    return pl.pallas_call(
        _moe_kernel,
        grid_spec=pltpu.PrefetchScalarGridSpec(
            num_scalar_prefetch=2,
            grid=(n_blocks,),
            in_specs=[rows] + [wspec(w) for w in ws],
            out_specs=rows,
            scratch_shapes=[pltpu.VMEM((D_MODEL, D_FF), BF16), pltpu.VMEM((D_MODEL, D_FF), BF16),
                            pltpu.VMEM((D_FF, D_MODEL), BF16)],
        ),
        out_shape=jax.ShapeDtypeStruct(xs.shape, F32),
        compiler_params=_params(("arbitrary",)),
        name="moe_experts",
    )(block_e, n_used, xs, *ws)


def _combine_kernel(dest_ref, gw_ref, x1_ref, gfin_ref, y_hbm, out_ref, dsm, buf, sem_idx, sem_row):
    tc = x1_ref.shape[0]
    idx_copy = pltpu.make_async_copy(dest_ref, dsm, sem_idx)
    idx_copy.start()
    idx_copy.wait()

    per_row = LANES // TOP_K

    def body(rr, _):
        for c in range(LANES):
            tk = rr * per_row + c // TOP_K
            pltpu.make_async_copy(_token_tile(y_hbm, dsm[rr, c]), _token_tile(buf.at[c % TOP_K], tk), sem_row).start()
        return 0
    lax.fori_loop(0, tc // per_row, body, 0)
    for k in range(TOP_K):
        pltpu.make_async_copy(y_hbm.at[pl.ds(0, tc * SUBLANES)], buf.at[k], sem_row).wait()

    gw = gw_ref[...]
    parts = []
    for s in range(D_MODEL // LANES):
        acc = None
        for k in range(TOP_K):
            term = buf[k, pl.ds(s, tc, stride=SUBLANES), :] * gw[:, k:k + 1]
            acc = term if acc is None else acc + term
        parts.append(acc)
    x2 = x1_ref[...] + jnp.concatenate(parts, axis=1)
    out_ref[...] = x2 * lax.rsqrt(jnp.mean(x2 * x2, axis=-1, keepdims=True) + RMS_EPS) * gfin_ref[...]


def _combine(dest2d, gw, x1, g_final, y_rows):
    n = x1.shape[0]
    tc = TC_COMBINE
    drows = tc * TOP_K // LANES
    row = lambda w: pl.BlockSpec((tc, w), lambda i: (i, 0))
    return pl.pallas_call(
        _combine_kernel,
        grid=(n // tc,),
        in_specs=[pl.BlockSpec((drows, LANES), lambda i: (i, 0)), row(LANES), row(D_MODEL),
                  pl.BlockSpec(g_final.shape, lambda i: (0, 0)), pl.BlockSpec(memory_space=pl.ANY)],
        out_specs=row(D_MODEL),
        out_shape=jax.ShapeDtypeStruct((n, D_MODEL), F32),
        scratch_shapes=[pltpu.SMEM((drows, LANES), I32), pltpu.VMEM((TOP_K, tc * SUBLANES, LANES), F32),
                        pltpu.SemaphoreType.DMA, pltpu.SemaphoreType.DMA],
        compiler_params=_params(("arbitrary",)),
        name="moe_combine",
    )(dest2d, gw, x1, g_final, y_rows)


def _bucket_table(max_dist):
    n = np.arange(max_dist, dtype=np.int64)
    scaled = np.log(np.maximum(n, 1).astype(np.float64) / REL_EXACT) / math.log(REL_MAX_DIST / REL_EXACT)
    large = REL_EXACT + (scaled * (REL_BUCKETS - REL_EXACT)).astype(np.int64)
    return np.where(n < REL_EXACT, n, np.minimum(large, REL_BUCKETS - 1)).astype(np.int32)


def _bias_lookup(rel_bias, dist):
    d = np.maximum(dist, 0)
    buckets = _bucket_table(int(d.max()) + 1)[d]
    return jnp.take(rel_bias.astype(F32).T, jnp.asarray(buckets), axis=1)


def _skew(w, n_rows, step, width):
    h, l = w.shape
    flat = jnp.tile(w, (1, n_rows))[:, :n_rows * (l - step)]
    return flat.reshape(h, n_rows, l - step)[:, :, :width]


def _importance_matrix(n_rows, n_cmp, n_sel, n_cols):
    rc = CMP_BLOCK // CMP_STRIDE
    rs = SEL_BLOCK // CMP_STRIDE
    m = np.zeros((n_rows, n_cols), np.float32)
    for j in range(n_sel):
        for o in range(rs + rc - 1):
            w = min(o - (rc - 1) + rc, rs) - max(o - (rc - 1), 0)
            c = rs * j + o - (rc - 1)
            if 0 <= c < n_cmp:
                m[c, j] += w
    return jnp.asarray(m)


def _block_expand(n_blocks, n_rows=LANES):
    e = np.zeros((n_rows, n_blocks * SEL_BLOCK), np.float32)
    for j in range(n_blocks):
        e[j, j * SEL_BLOCK:(j + 1) * SEL_BLOCK] = 1.0
    return jnp.asarray(e, dtype=BF16)


def _gate_expand():
    e = np.zeros((LG_PAD, 3 * NSA_WIDTH), np.float32)
    for g in range(NSA_KV_HEADS):
        for r in range(NSA_GROUP):
            for c in range(3):
                lo = c * NSA_WIDTH + r * LANES + g * HEAD_DIM
                e[(g * NSA_GROUP + r) * 3 + c, lo:lo + HEAD_DIM] = 1.0
    return jnp.asarray(e)


def _prompt_tables(rel_bias, t):
    n_qb = t // Q_BLOCK
    n_ch = t // CMP_STRIDE
    tk = min(SEL_CHUNK, t)
    voff = max(t - Q_BLOCK, WINDOW)
    nv = voff // LANES + tk // LANES
    wd = nv * LANES
    wv = _bias_lookup(rel_bias, np.concatenate([voff - np.arange(wd), voff + np.arange(Q_BLOCK, 0, -1)]))
    gr = _skew(wv, Q_BLOCK, 1, wd).reshape(NSA_HEADS, Q_BLOCK, nv, LANES).transpose(2, 0, 1, 3)
    wc = _bias_lookup(rel_bias, np.concatenate([np.arange(t) - (CMP_BLOCK - 1), np.zeros(CMP_STRIDE * n_ch, np.int64)]))
    cend = np.arange(n_ch) * CMP_STRIDE + (CMP_BLOCK - 1)
    cmask = np.where(np.arange(t)[:, None] >= cend[None, :], 0.0, NEG_INF).astype(np.float32)
    bc = _skew(wc, n_ch, CMP_STRIDE, t).transpose(0, 2, 1) + cmask[None]
    bc = bc.reshape(NSA_HEADS, n_qb, Q_BLOCK, n_ch).transpose(1, 0, 2, 3)
    n_w = WINDOW + Q_BLOCK
    ww = _bias_lookup(rel_bias, np.concatenate([WINDOW - np.arange(n_w), WINDOW + np.arange(Q_BLOCK, 0, -1)]))
    dist_w = WINDOW + np.arange(Q_BLOCK)[:, None] - np.arange(n_w)[None, :]
    wmask = np.where((dist_w >= 0) & (dist_w < WINDOW), 0.0, NEG_INF).astype(np.float32)
    wt = _skew(ww, Q_BLOCK, 1, n_w) + wmask[None]
    return dict(gr=gr, bc=bc, wt=wt, voff=voff,
                mimp=_importance_matrix(n_ch, n_ch - 1, t // SEL_BLOCK, LANES),
                esel=_block_expand(t // SEL_BLOCK), eg=_gate_expand())


def _sample_tables(rel_bias, past, s_new, w_buf):
    n_ch = past // CMP_STRIDE
    n_sel = (past + s_new + SEL_BLOCK - 1) // SEL_BLOCK
    per_group = lambda a: a.reshape(NSA_KV_HEADS, NSA_GROUP, s_new, -1).reshape(NSA_KV_HEADS, NSA_GROUP * s_new, -1)
    qi = np.arange(s_new)

    def table(dist):
        return per_group(_bias_lookup(rel_bias, dist))
    ws = _bias_lookup(rel_bias, np.concatenate([past - np.arange(past), past + np.arange(s_new, 0, -1)]))
    gs = per_group(_skew(ws, s_new, 1, past))
    jn = np.arange(LANES)
    gn = table(np.where(jn[None, :] < s_new, qi[:, None] - jn[None, :], 0))
    cend = np.arange(n_ch) * CMP_STRIDE + (CMP_BLOCK - 1)
    bcs = table(past + qi[:, None] - cend[None, :])
    bws = table(w_buf + qi[:, None] - np.arange(w_buf)[None, :])
    sel_lanes = -(-n_sel // LANES) * LANES
    return dict(gs=gs, gn=gn, bcs=bcs, bws=bws,
                mimp_s=_importance_matrix(n_ch, n_ch - 1, n_sel, sel_lanes),
                esel_s=_block_expand(past // SEL_BLOCK, past // SEL_BLOCK), eg=_gate_expand())


def _compress_weights(pe, w1, w2):
    rc = CMP_BLOCK // CMP_STRIDE
    w1r = w1.reshape(rc, CMP_STRIDE, HEAD_DIM, CMP_HIDDEN)
    eye = jnp.eye(NSA_KV_HEADS, dtype=w1.dtype)
    wcat = jnp.einsum("rldh,ge->lgdreh", w1r, eye).reshape(CMP_STRIDE // 2, 2 * KV_WIDTH,
                                                           rc * NSA_KV_HEADS * CMP_HIDDEN)
    w2bd = jnp.einsum("hd,ge->ghed", w2, eye).reshape(NSA_KV_HEADS * CMP_HIDDEN, KV_WIDTH)
    pe_rows = jnp.concatenate([pe.reshape(1, CMP_BLOCK * HEAD_DIM),
                               jnp.zeros((SUBLANES - 1, CMP_BLOCK * HEAD_DIM), pe.dtype)], axis=0)
    return wcat.astype(BF16), pe_rows, w1.reshape(CMP_BLOCK * HEAD_DIM, CMP_HIDDEN), w2bd.astype(BF16)


def _layer_weights(w_in, gla_w_alpha, gla_b_alpha, w_branch_nsa, norm_ffn, router_w, router_b):
    offs = np.cumsum((0,) + IN_SPLITS)
    col = lambda j: w_in[:, offs[j]:offs[j + 1]]
    q_perm = col(0).reshape(D_MODEL, NSA_KV_HEADS, NSA_GROUP, HEAD_DIM).transpose(0, 2, 1, 3).reshape(D_MODEL, NSA_WIDTH)
    pad = jnp.zeros((D_MODEL, LG_PAD - 3 * NSA_HEADS - GLA_RANK), w_in.dtype)
    w_p = jnp.concatenate([q_perm, col(1), col(3), col(4), col(5), col(7), col(8), col(9), col(2), col(6), pad],
                          axis=1).astype(BF16)
    w_al = jnp.zeros((LG_PAD, GLA_K_WIDTH), F32).at[3 * NSA_HEADS:3 * NSA_HEADS + GLA_RANK].set(gla_w_alpha)
    wn = w_branch_nsa.reshape(NSA_KV_HEADS, NSA_GROUP, HEAD_DIM, D_MODEL).transpose(1, 0, 2, 3).reshape(NSA_WIDTH, D_MODEL)
    rw = jnp.concatenate([router_w, jnp.zeros((D_MODEL, LANES - N_EXPERTS), F32)], axis=1)
    rb = jnp.concatenate([router_b, jnp.full((LANES - N_EXPERTS,), NEG_INF, F32)]).reshape(1, LANES)
    return w_p, w_al, gla_b_alpha.reshape(1, GLA_K_WIDTH), wn.astype(BF16), norm_ffn.reshape(1, D_MODEL), rw, rb


def _moe(h2t, ei, gw, rk, counts, x1, g_final, ew):
    n = x1.shape[0]
    nk = n * TOP_K
    counts = counts[0, :N_EXPERTS].astype(I32)
    padded = (counts + MOE_ROWS - 1) // MOE_ROWS * MOE_ROWS
    pends = jnp.cumsum(padded)
    pstarts = pends - padded
    n_blocks = (nk + N_EXPERTS * (MOE_ROWS - 1) + MOE_ROWS - 1) // MOE_ROWS
    blk_start = jnp.arange(n_blocks, dtype=I32) * MOE_ROWS
    block_e = jnp.minimum(jnp.sum((pends[None, :] <= blk_start[:, None]).astype(I32), axis=1), N_EXPERTS - 1)
    n_used = (pends[-1] // MOE_ROWS).astype(I32).reshape(1)
    e_sel = ei[:, :TOP_K, None] == jnp.arange(N_EXPERTS, dtype=I32)
    dest = jnp.sum(jnp.where(e_sel, pstarts.astype(I32), 0), axis=-1) + rk[:, :TOP_K]
    dest2d = dest.reshape(nk // LANES, LANES)
    meta = jnp.concatenate([counts, pstarts, padded, n_used]).astype(I32)
    td = min(TD_DISPATCH, n)
    xs = _dispatch(meta, dest2d, h2t, n_blocks * MOE_ROWS, td)
    y_rows = _moe_experts(block_e, n_used, xs, ew, n_blocks)
    return _combine(dest2d, gw, x1, g_final, y_rows)


def kernel(x_prompt, x_sample, cache_cmp_k, cache_cmp_v, cache_sel_k, cache_sel_v, state_win_k, state_win_v, state_gla, page_table, rel_bias, norm_mix, w_in, nsa_pe_k, nsa_pe_v, nsa_w1_k, nsa_w1_v, nsa_w2_k, nsa_w2_v, gla_w_alpha, gla_b_alpha, gla_norm, w_branch_nsa, w_branch_gla, w_out, norm_ffn, router_w, router_b, exp_w_gate, exp_b_gate, exp_w_up, exp_b_up, exp_w_down, exp_b_down, norm_final):
    depth = w_in.shape[0]
    assert depth == 1, "single-layer trunk"
    bsz, t, d = x_prompt.shape
    n_seq, s_new, _ = x_sample.shape
    n_pages = page_table.shape[1]
    past = n_pages * PAGE_SIZE
    w_buf = state_win_k.shape[2]
    assert d == D_MODEL and t % Q_BLOCK == 0 and t % GLA_CHUNK == 0

    w_p, w_al, b_al, wn, gf, rw, rb = _layer_weights(w_in[0], gla_w_alpha[0], gla_b_alpha[0], w_branch_nsa[0],
                                                     norm_ffn[0], router_w[0], router_b[0])
    g_mix = norm_mix[0].reshape(1, D_MODEL)
    g_fin = norm_final.reshape(1, D_MODEL)
    g_gla = gla_norm[0].reshape(1, GLA_DV)
    mw = dict(wn=wn, wg=w_branch_gla[0].astype(BF16), wo=w_out[0].astype(BF16), gf=gf, rw=rw, rb=rb)
    ew = dict(wg=exp_w_gate[0], bg=exp_b_gate[0].reshape(N_EXPERTS, 1, D_FF),
              wu=exp_w_up[0], bu=exp_b_up[0].reshape(N_EXPERTS, 1, D_FF),
              wd=exp_w_down[0], bd=exp_b_down[0].reshape(N_EXPERTS, 1, D_MODEL))
    cw = {}
    for nm, pe, w1, w2 in (("k", nsa_pe_k[0], nsa_w1_k[0], nsa_w2_k[0]), ("v", nsa_pe_v[0], nsa_w1_v[0], nsa_w2_v[0])):
        cw["wcat_" + nm], cw["pe_" + nm], cw["w1f_" + nm], cw["w2_" + nm] = _compress_weights(pe, w1, w2)

    xp = x_prompt.reshape(bsz * t, d)
    (q4, kck, kcv, ksk, ksv, kwk, kwv, gates, q_l, k_l, v_l, lg, r_l, m_a, m_b, *kv_t) = _in_projection(
        xp, g_mix, w_p, w_al, b_al, seq_len=t)
    kc, vc = _compress_prompt(kck, kcv, cw, bsz, t)
    o_nsa = _nsa_prompt(q4, gates, kc, vc, ksk, ksv, kwk, kwv, _prompt_tables(rel_bias, t), bsz, t)
    s_zero = jnp.zeros((bsz * GLA_HEADS * GLA_DK, GLA_DV), F32)
    o_gla, p_gla = _gla(q_l, k_l, v_l, lg, r_l, g_gla, s_zero, bsz, t // GLA_CHUNK, GLA_CHUNK)
    x1, h2t, ei, gw, rk, counts = _merge(xp, o_nsa, o_gla, m_a, m_b, mw)
    y_prompt = _moe(h2t, ei, gw, rk, counts, x1, g_fin, ew).reshape(bsz, t, d)

    w_len = min(WINDOW, t)
    rows_of = lambda a: a.reshape(bsz, NSA_KV_HEADS, HEAD_DIM, t).transpose(0, 3, 1, 2)[None]
    p_states = tuple(rows_of(a) for a in kv_t[:4]) + tuple(rows_of(a)[:, :, t - w_len:] for a in kv_t[4:]) + (
        p_gla.reshape(1, bsz, GLA_HEADS, GLA_DK, GLA_DV),)

    xs = x_sample.reshape(n_seq * s_new, d)
    (q4, kck, kcv, ksk, ksv, kwk, kwv, gates, q_l, k_l, v_l, lg, r_l, m_a, m_b) = _in_projection(xs, g_mix, w_p, w_al, b_al)
    pt_flat = page_table.reshape(n_seq * n_pages).astype(I32)
    pool = lambda c: c[0].transpose(0, 2, 3, 1).reshape(-1, PAGE_SIZE)
    kc, vc = _compress_sample(pt_flat, pool(cache_cmp_k), pool(cache_cmp_v), cw, n_seq, n_pages)
    buf_kw = state_win_k[0].transpose(0, 2, 3, 1).reshape(n_seq * KV_WIDTH, w_buf)
    buf_vw = state_win_v[0].transpose(0, 2, 3, 1).reshape(n_seq * KV_WIDTH, w_buf)
    o_nsa = _nsa_sample(pt_flat, q4, gates, kc, vc, ksk, ksv, kwk, kwv, buf_kw, buf_vw,
                        pool(cache_sel_k), pool(cache_sel_v), _sample_tables(rel_bias, past, s_new, w_buf),
                        n_seq, s_new, n_pages)
    cl = 16
    padc = lambda a: jnp.pad(a.reshape(n_seq, s_new, -1), ((0, 0), (0, cl - s_new), (0, 0))).reshape(n_seq * cl, -1)
    s_in = state_gla[0].reshape(n_seq * GLA_HEADS * GLA_DK, GLA_DV)
    o_gla, s_gla = _gla(padc(q_l), padc(k_l), padc(v_l), padc(lg), padc(r_l), g_gla, s_in, n_seq, 1, cl)
    o_gla = o_gla.reshape(n_seq, cl, GLA_V_WIDTH)[:, :s_new].reshape(n_seq * s_new, GLA_V_WIDTH)
    x1, h2t, ei, gw, rk, counts = _merge(xs, o_nsa, o_gla, m_a, m_b, mw)
    y_sample = _moe(h2t, ei, gw, rk, counts, x1, g_fin, ew).reshape(n_seq, s_new, d)

    kvs = (1, n_seq, s_new, NSA_KV_HEADS, HEAD_DIM)
    new_win = lambda buf, new: jnp.concatenate([buf, new.reshape(kvs).astype(buf.dtype)], axis=2)[:, :, s_new:]
    s_states = (kck.reshape(kvs), kcv.reshape(kvs), ksk.reshape(kvs), ksv.reshape(kvs),
                new_win(state_win_k, kwk), new_win(state_win_v, kwv),
                s_gla.reshape(1, n_seq, GLA_HEADS, GLA_DK, GLA_DV))
    return (y_prompt, y_sample) + p_states + s_states
```

```python
import functools
import math

import numpy as np
import jax
import jax.numpy as jnp
from jax import lax
from jax.experimental import pallas as pl
from jax.experimental.pallas import tpu as pltpu

F32 = jnp.float32
BF16 = jnp.bfloat16
I32 = jnp.int32
HI = lax.Precision.HIGHEST

D_MODEL = 1024
PAGE_SIZE = 128
NSA_HEADS = 8
NSA_KV_HEADS = 2
NSA_GROUP = NSA_HEADS // NSA_KV_HEADS
HEAD_DIM = 64
NSA_WIDTH = NSA_HEADS * HEAD_DIM
KV_WIDTH = NSA_KV_HEADS * HEAD_DIM
CMP_BLOCK = 32
CMP_STRIDE = 16
CMP_HIDDEN = 2 * HEAD_DIM
SEL_BLOCK = 64
N_SEL = 16
WINDOW = 512
Q_BLOCK = 128
SEL_FORCE = 1e9
GLA_HEADS = 4
GLA_DK = 64
GLA_DV = 128
GLA_K_WIDTH = GLA_HEADS * GLA_DK
GLA_V_WIDTH = GLA_HEADS * GLA_DV
GLA_RANK = 16
GLA_TAU = 16.0
GLA_CHUNK = 64
N_EXPERTS = 32
TOP_K = 4
D_FF = D_MODEL
SWIGLU_ALPHA = 1.702
SWIGLU_LIMIT = 7.0
REL_BUCKETS = 32
REL_EXACT = REL_BUCKETS // 2
REL_MAX_DIST = 1024
RMS_EPS = 1e-6
NEG_INF = -1e30
IN_SPLITS = (NSA_WIDTH, 6 * KV_WIDTH, 3 * NSA_HEADS, GLA_K_WIDTH, GLA_K_WIDTH, GLA_V_WIDTH, GLA_RANK,
             GLA_V_WIDTH, D_MODEL, D_MODEL)

LANES = 128
SUBLANES = 8
VMEM_LIMIT = 56 * 1024 * 1024

TM_PROJ = 256
TM_MERGE = 512
MOE_ROWS = 512
MOE_ROWS_SMALL = 256
TD_DISPATCH = 512
TC_COMBINE = 256
SEL_CHUNK = 512
PIPE_HEADS = 2
GLA_CHUNKS_PER_STEP = 4
LG_PAD = LANES


def _nt(a, b, **kw):
    return lax.dot_general(a, b, (((1,), (1,)), ((), ())), preferred_element_type=F32, **kw)


def _tn(a, b, **kw):
    return lax.dot_general(a, b, (((0,), (0,)), ((), ())), preferred_element_type=F32, **kw)


def _dot(a, b, **kw):
    return jnp.dot(a, b, preferred_element_type=F32, **kw)


def _params(sem, vmem=VMEM_LIMIT):
    return pltpu.CompilerParams(dimension_semantics=sem, vmem_limit_bytes=vmem)


def _masked_softmax_parts(s, valid):
    s = jnp.where(valid, s, NEG_INF)
    m = jnp.max(s, axis=-1, keepdims=True)
    e = jnp.where(valid, jnp.exp(s - m), 0.0)
    return e, jnp.maximum(jnp.sum(e, axis=-1, keepdims=True), 1e-20)


_OFF_Q = 0
_OFF_KV = _OFF_Q + NSA_WIDTH
_OFF_QL = _OFF_KV + 6 * KV_WIDTH
_OFF_KL = _OFF_QL + GLA_K_WIDTH
_OFF_VL = _OFF_KL + GLA_K_WIDTH
_OFF_R = _OFF_VL + GLA_V_WIDTH
_OFF_MA = _OFF_R + GLA_V_WIDTH
_OFF_MB = _OFF_MA + D_MODEL
_OFF_GA = _OFF_MB + D_MODEL
_N_PROJ = _OFF_GA + LG_PAD


def _inproj_kernel(x_ref, g_ref, w_ref, wal_ref, bal_ref,
                   q_o, kck_o, kcv_o, ksk_o, ksv_o, kwk_o, kwv_o, gt_o, ql_o, kl_o, vl_o, lg_o, r_o, ma_o, mb_o,
                   *kv_t_o):
    x = x_ref[...]
    xn = x * lax.rsqrt(jnp.mean(x * x, axis=-1, keepdims=True) + RMS_EPS)
    xn = (xn * g_ref[...]).astype(BF16)

    def mm(lo, n):
        return _dot(xn, w_ref[:, lo:lo + n])

    q = mm(_OFF_Q, NSA_WIDTH) * (HEAD_DIM ** -0.5)
    for r in range(NSA_GROUP):
        q_o[r] = q[:, r * LANES:(r + 1) * LANES]
    kv = mm(_OFF_KV, 6 * KV_WIDTH)
    for j, o in enumerate((kck_o, kcv_o, ksk_o, ksv_o, kwk_o, kwv_o)):
        o[...] = kv[:, j * KV_WIDTH:(j + 1) * KV_WIDTH]
    for j, o in enumerate(kv_t_o):
        o[...] = kv[:, j * KV_WIDTH:(j + 1) * KV_WIDTH].T
    ql_o[...] = mm(_OFF_QL, GLA_K_WIDTH) * (GLA_DK ** -0.5)
    kl_o[...] = mm(_OFF_KL, GLA_K_WIDTH)
    vl_o[...] = mm(_OFF_VL, GLA_V_WIDTH)
    r_o[...] = mm(_OFF_R, GLA_V_WIDTH)
    ma_o[...] = mm(_OFF_MA, D_MODEL)
    mb_o[...] = mm(_OFF_MB, D_MODEL)
    ga = mm(_OFF_GA, LG_PAD)
    gt_o[...] = jax.nn.sigmoid(ga)
    al = _dot(ga, wal_ref[...], precision=HI) + bal_ref[...]
    lg_o[...] = (jnp.minimum(al, 0.0) - jnp.log1p(jnp.exp(-jnp.abs(al)))) * (1.0 / GLA_TAU)


def _in_projection(x, norm_g, w_p, w_al, b_al, seq_len=None):
    n = x.shape[0]
    tm = TM_PROJ
    assert n % tm == 0
    row = lambda w: pl.BlockSpec((tm, w), lambda i: (i, 0))
    full = lambda a: pl.BlockSpec(a.shape, lambda i: (0,) * a.ndim)
    widths = (KV_WIDTH,) * 6 + (LG_PAD, GLA_K_WIDTH, GLA_K_WIDTH, GLA_V_WIDTH, GLA_K_WIDTH, GLA_V_WIDTH,
                                 D_MODEL, D_MODEL)
    out_shape = [jax.ShapeDtypeStruct((NSA_GROUP, n, LANES), F32)] + [jax.ShapeDtypeStruct((n, w), F32) for w in widths]
    out_specs = [pl.BlockSpec((NSA_GROUP, tm, LANES), lambda i: (0, i, 0))] + [row(w) for w in widths]
    if seq_len is not None:
        assert seq_len % tm == 0
        per_seq = seq_len // tm
        out_shape += [jax.ShapeDtypeStruct((n // seq_len * KV_WIDTH, seq_len), F32)] * 6
        out_specs += [pl.BlockSpec((KV_WIDTH, tm), lambda i: (i // per_seq, i % per_seq))] * 6
    return pl.pallas_call(
        _inproj_kernel,
        grid=(n // tm,),
        in_specs=[row(D_MODEL), full(norm_g), full(w_p), full(w_al), full(b_al)],
        out_specs=out_specs,
        out_shape=out_shape,
        compiler_params=_params(("arbitrary",)),
        name="in_projection",
    )(x, norm_g, w_p, w_al, b_al)


def _gelu_tanh(x):
    return 0.5 * x * (1.0 + jnp.tanh(math.sqrt(2.0 / math.pi) * (x + 0.044715 * (x * x * x))))


def _compress_rows(src, n_ch, wcat_ref, pe_ref, w1f_ref, w2_ref):
    hid2 = NSA_KV_HEADS * CMP_HIDDEN
    acc = jnp.zeros((n_ch, 2 * hid2), F32)
    for l in range(0, CMP_STRIDE, 2):
        xl = jnp.concatenate([src[pl.ds(l, n_ch, stride=CMP_STRIDE), :],
                              src[pl.ds(l + 1, n_ch, stride=CMP_STRIDE), :]], axis=1).astype(BF16)
        acc = acc + _dot(xl, wcat_ref[l // 2])
    bias = _dot(pe_ref[...], w1f_ref[...], precision=HI)[0:1]
    bias2 = jnp.concatenate([bias] * NSA_KV_HEADS, axis=1)
    nxt = pltpu.roll(acc[:, hid2:], n_ch - 1, 0)
    h = acc[:, :hid2] + nxt + bias2
    return _dot(_gelu_tanh(h).astype(BF16), w2_ref[...])


def _compress_prompt_kernel(k_ref, v_ref, wk_ref, wv_ref, pek_ref, pev_ref, w1k_ref, w1v_ref, w2k_ref, w2v_ref,
                            kc_o, vc_o):
    n_ch = kc_o.shape[0]
    kc_o[...] = _compress_rows(k_ref, n_ch, wk_ref, pek_ref, w1k_ref, w2k_ref)
    vc_o[...] = _compress_rows(v_ref, n_ch, wv_ref, pev_ref, w1v_ref, w2v_ref)


def _compress_prompt(k_cmp, v_cmp, cw, bsz, t):
    n_ch = t // CMP_STRIDE
    full = lambda a: pl.BlockSpec(a.shape, lambda b: (0,) * a.ndim)
    seq = pl.BlockSpec((t, KV_WIDTH), lambda b: (b, 0))
    out = pl.BlockSpec((n_ch, KV_WIDTH), lambda b: (b, 0))
    ws = (cw["wcat_k"], cw["wcat_v"], cw["pe_k"], cw["pe_v"], cw["w1f_k"], cw["w1f_v"], cw["w2_k"], cw["w2_v"])
    return pl.pallas_call(
        _compress_prompt_kernel,
        grid=(bsz,),
        in_specs=[seq, seq] + [full(w) for w in ws],
        out_specs=[out, out],
        out_shape=[jax.ShapeDtypeStruct((bsz * n_ch, KV_WIDTH), F32)] * 2,
        compiler_params=_params(("arbitrary",)),
        name="nsa_compress_prompt",
    )(k_cmp, v_cmp, *ws)


def _paged_fetch(pt_ref, pools, bufs, sems, seq, slot, n_pages, pages_on_lanes):
    def body(p, _):
        pg = pt_ref[seq * n_pages + p]
        off = pl.multiple_of(p * PAGE_SIZE, PAGE_SIZE)
        for j, (pool, buf) in enumerate(zip(pools, bufs)):
            dst = buf.at[slot, :, pl.ds(off, PAGE_SIZE)] if pages_on_lanes else buf.at[slot, pl.ds(off, PAGE_SIZE)]
            pltpu.make_async_copy(pool.at[pl.ds(pg * KV_WIDTH, KV_WIDTH)], dst, sems.at[j, slot]).start()
        return 0
    lax.fori_loop(0, n_pages, body, 0)


def _paged_wait(bufs, sems, slot):
    for j, buf in enumerate(bufs):
        pltpu.make_async_copy(buf.at[slot], buf.at[slot], sems.at[j, slot]).wait()


def _compress_sample_kernel(pt_ref, pk_hbm, pv_hbm, wk_ref, wv_ref, pek_ref, pev_ref, w1k_ref, w1v_ref,
                            w2k_ref, w2v_ref, kc_o, vc_o, bufk, bufv, sems, rows_k, rows_v, *, n_pages):
    s = pl.program_id(0)
    n_seq = pl.num_programs(0)
    slot = s % 2
    pools, bufs = (pk_hbm, pv_hbm), (bufk, bufv)

    @pl.when(s == 0)
    def _():
        _paged_fetch(pt_ref, pools, bufs, sems, s, slot, n_pages, False)

    @pl.when(s + 1 < n_seq)
    def _():
        _paged_fetch(pt_ref, pools, bufs, sems, s + 1, 1 - slot, n_pages, False)

    _paged_wait(bufs, sems, slot)
    for p in range(n_pages):
        rows = slice(p * PAGE_SIZE, (p + 1) * PAGE_SIZE)
        rows_k[rows, :] = bufk[slot, rows, :].T
        rows_v[rows, :] = bufv[slot, rows, :].T
    n_ch = kc_o.shape[0]
    kc_o[...] = _compress_rows(rows_k, n_ch, wk_ref, pek_ref, w1k_ref, w2k_ref)
    vc_o[...] = _compress_rows(rows_v, n_ch, wv_ref, pev_ref, w1v_ref, w2v_ref)


def _compress_sample(pt_flat, pool_k, pool_v, cw, n_seq, n_pages):
    past = n_pages * PAGE_SIZE
    n_ch = past // CMP_STRIDE
    full = lambda a: pl.BlockSpec(a.shape, lambda s, pt: (0,) * a.ndim)
    hbm = pl.BlockSpec(memory_space=pl.ANY)
    out = pl.BlockSpec((n_ch, KV_WIDTH), lambda s, pt: (s, 0))
    ws = (cw["wcat_k"], cw["wcat_v"], cw["pe_k"], cw["pe_v"], cw["w1f_k"], cw["w1f_v"], cw["w2_k"], cw["w2_v"])
    return pl.pallas_call(
        functools.partial(_compress_sample_kernel, n_pages=n_pages),
        grid_spec=pltpu.PrefetchScalarGridSpec(
            num_scalar_prefetch=1,
            grid=(n_seq,),
            in_specs=[hbm, hbm] + [full(w) for w in ws],
            out_specs=[out, out],
            scratch_shapes=[pltpu.VMEM((2, past, KV_WIDTH), F32), pltpu.VMEM((2, past, KV_WIDTH), F32),
                            pltpu.SemaphoreType.DMA((2, 2)),
                            pltpu.VMEM((past, KV_WIDTH), F32), pltpu.VMEM((past, KV_WIDTH), F32)],
        ),
        out_shape=[jax.ShapeDtypeStruct((n_seq * n_ch, KV_WIDTH), F32)] * 2,
        compiler_params=_params(("arbitrary",)),
        name="nsa_compress_sample",
    )(pt_flat, pool_k, pool_v, *ws)


def _select_blocks(score, n_sel):
    blk = lax.broadcasted_iota(I32, score.shape, 1)
    cnt = jnp.zeros(score.shape, F32)
    for i in range(n_sel):
        col = score[:, i:i + 1]
        ahead = (col > score) | ((col == score) & (i < blk))
        cnt = cnt + jnp.where(ahead, 1.0, 0.0)
    n_top = min(N_SEL, n_sel)
    return jnp.where((cnt < n_top) & (blk < n_sel), 1.0, 0.0)


def _select_blocks_t(score_t, n_sel):
    blk = lax.broadcasted_iota(I32, score_t.shape, 0)
    cnt = jnp.zeros(score_t.shape, F32)
    for i in range(n_sel):
        row = jnp.broadcast_to(score_t[i:i + 1, :], score_t.shape)
        ahead = (row > score_t) | ((row == score_t) & (i < blk))
        cnt = cnt + jnp.where(ahead, 1.0, 0.0)
    return jnp.where(cnt < min(N_SEL, n_sel), 1.0, 0.0)


def _block_scores(imp, qpos, n_sel):
    blk = lax.broadcasted_iota(I32, imp.shape, 1)
    cur = qpos // SEL_BLOCK
    forced = (blk == 0) | (blk == cur) | (blk == cur - 1)
    causal = (blk * SEL_BLOCK) <= qpos
    score = jnp.where(causal, jnp.where(forced, SEL_FORCE, imp), -SEL_FORCE)
    return jnp.where(blk < n_sel, score, -3e38)


def _gate_mix(gexp, o_c, o_s, o_w, r):
    out = None
    for c, o in enumerate((o_c, o_s, o_w)):
        term = gexp[:, c * NSA_WIDTH + r * LANES:c * NSA_WIDTH + (r + 1) * LANES] * o
        out = term if out is None else out + term
    return out


def _nsa_prompt_kernel(q_ref, gt_ref, kc_ref, vc_ref, ks_ref, vs_ref, kw_ref, vw_ref,
                       gr_ref, bc_ref, wt_ref, mimp_ref, esel_ref, eg_ref, o_ref,
                       ksb, vsb, kwb, vwb, kcb, vcb, oc_s, msk_s, qs, *state, voff_blk):
    qb = pl.program_id(1)
    t = ks_ref.shape[0]
    n_ch = kc_ref.shape[0]
    n_sel = t // SEL_BLOCK
    tk = msk_s.shape[3]
    n_kc = t // tk
    tiles = tk // LANES

    @pl.when(qb == 0)
    def _():
        ksb[...] = ks_ref[...].astype(BF16)
        vsb[...] = vs_ref[...].astype(BF16)
        kwb[0:WINDOW, :] = jnp.zeros((WINDOW, KV_WIDTH), BF16)
        vwb[0:WINDOW, :] = jnp.zeros((WINDOW, KV_WIDTH), BF16)
        kwb[WINDOW:, :] = kw_ref[...].astype(BF16)
        vwb[WINDOW:, :] = vw_ref[...].astype(BF16)
        kcb[...] = kc_ref[...].astype(BF16)
        vcb[...] = vc_ref[...].astype(BF16)

    q0 = qb * Q_BLOCK
    lane = lax.broadcasted_iota(I32, (Q_BLOCK, LANES), 1)
    qpos = q0 + lax.broadcasted_iota(I32, (Q_BLOCK, 1), 0)
    upper = lane >= HEAD_DIM

    nh = NSA_HEADS
    rows = nh * Q_BLOCK
    for h in range(nh):
        keep = upper if h >= NSA_GROUP else jnp.logical_not(upper)
        qs[h * Q_BLOCK:(h + 1) * Q_BLOCK, :] = jnp.where(keep, q_ref[h % NSA_GROUP], 0.0).astype(BF16)
    q_all = qs[...]

    def per_group(x3, add2):
        return jnp.concatenate([x3[g * NSA_GROUP:(g + 1) * NSA_GROUP] + add2[g][None]
                                for g in range(NSA_KV_HEADS)], axis=0)

    any_c = (qpos >= (CMP_BLOCK - 1))[None]
    s = _nt(q_all, kcb[...]).reshape(nh, Q_BLOCK, n_ch) + bc_ref[0]
    e = jnp.exp(s - jnp.max(s, axis=-1, keepdims=True))
    den = jnp.maximum(jnp.sum(e, axis=-1, keepdims=True), 1e-20)
    p = e * jnp.where(any_c, 1.0 / den, 0.0)
    oc_s[...] = _dot(p.reshape(rows, n_ch).astype(BF16), vcb[...])
    q_end = q0 + Q_BLOCK
    must_rank = (q_end - 1) // SEL_BLOCK + 1 > N_SEL
    for c in range(n_kc):
        kpos = c * tk + lax.broadcasted_iota(I32, (Q_BLOCK, tk), 1)

        @pl.when((c * tk < q_end) & jnp.logical_not(must_rank))
        def _(c=c, kpos=kpos):
            causal_only = jnp.where(kpos <= qpos, 0.0, NEG_INF)
            for g in range(NSA_KV_HEADS):
                msk_s[g, c] = causal_only

    @pl.when(must_rank)
    def _():
        for g in range(NSA_KV_HEADS):
            psum = p[g * NSA_GROUP]
            for r in range(1, NSA_GROUP):
                psum = psum + p[g * NSA_GROUP + r]
            imp = _dot(psum, mimp_ref[...], precision=HI)
            sel_t = _select_blocks_t(_block_scores(imp, qpos, n_sel).T[0:n_sel], n_sel).astype(BF16)
            for c in range(n_kc):
                @pl.when(c * tk < q_end)
                def _(c=c, g=g, sel_t=sel_t):
                    kpos = c * tk + lax.broadcasted_iota(I32, (Q_BLOCK, tk), 1)
                    picked = _tn(sel_t, esel_ref[0:n_sel, c * tk:(c + 1) * tk]) > 0.5
                    msk_s[g, c] = jnp.where(picked & (kpos <= qpos), 0.0, NEG_INF)

    ph = PIPE_HEADS
    nb = nh // ph
    brow = ph * Q_BLOCK
    m_r, l_r, a_r = state[0:nb], state[nb:2 * nb], state[2 * nb:3 * nb]
    for b in range(nb):
        m_r[b][...] = jnp.full((brow, 1), NEG_INF, F32)
        l_r[b][...] = jnp.zeros((brow, 1), F32)
        a_r[b][...] = jnp.zeros((brow, KV_WIDTH), F32)
    q_blk = lambda b: qs[b * brow:(b + 1) * brow, :]

    def chunk(kt, _):
        k0 = pl.multiple_of(kt * tk, tk)
        kk = ksb[pl.ds(k0, tk), :]
        vv = vsb[pl.ds(k0, tk), :]
        base = voff_blk - qb + kt * tiles

        def scores(b):
            bias = jnp.concatenate([gr_ref[base + j, b * ph:(b + 1) * ph] for j in range(tiles)], axis=-1)
            s = _nt(q_blk(b), kk).reshape(ph, Q_BLOCK, tk) + bias + msk_s[(b * ph) // NSA_GROUP, kt][None]
            return s.reshape(brow, tk)

        s_next = scores(0)
        for b in range(nb):
            s = s_next
            if b + 1 < nb:
                s_next = scores(b + 1)
            m_old = m_r[b][...]
            m_new = jnp.maximum(m_old, jnp.max(s, axis=-1, keepdims=True))
            alpha = jnp.exp(m_old - m_new)
            e = jnp.exp(s - m_new)
            l_r[b][...] = alpha * l_r[b][...] + jnp.sum(e, axis=-1, keepdims=True)
            m_r[b][...] = m_new
            a_r[b][...] = alpha * a_r[b][...] + _dot(e.astype(BF16), vv)
        return 0
    lax.fori_loop(0, (q0 + Q_BLOCK + tk - 1) // tk, chunk, 0)

    n_w = WINDOW + Q_BLOCK
    wpos = q0 - WINDOW + lax.broadcasted_iota(I32, (Q_BLOCK, n_w), 1)
    before_start = jnp.where(wpos >= 0, 0.0, NEG_INF)[None]
    w0 = pl.multiple_of(q0, Q_BLOCK)
    kw = kwb[pl.ds(w0, n_w), :]
    vw = vwb[pl.ds(w0, n_w), :]

    def w_scores(b):
        s = _nt(q_blk(b), kw).reshape(ph, Q_BLOCK, n_w) + wt_ref[b * ph:(b + 1) * ph] + before_start
        return s.reshape(brow, n_w)

    o_w = []
    s_next = w_scores(0)
    for b in range(nb):
        s = s_next
        if b + 1 < nb:
            s_next = w_scores(b + 1)
        e = jnp.exp(s - jnp.max(s, axis=-1, keepdims=True))
        den = jnp.maximum(jnp.sum(e, axis=-1, keepdims=True), 1e-20)
        o_w.append(_dot(e.astype(BF16), vw) / den)
    o_w = jnp.concatenate(o_w, axis=0)
    o_s = jnp.concatenate([a_r[b][...] / jnp.maximum(l_r[b][...], 1e-20) for b in range(nb)], axis=0)
    o_c = oc_s[...]

    gexp = _dot(gt_ref[...], eg_ref[...], precision=HI)
    head = lambda x, h: x[h * Q_BLOCK:(h + 1) * Q_BLOCK]
    for r in range(NSA_GROUP):
        pick = lambda x: jnp.where(upper, head(x, NSA_GROUP + r), head(x, r))
        o_ref[r] = _gate_mix(gexp, pick(o_c), pick(o_s), pick(o_w), r)


def _nsa_prompt(q4, gates, kc, vc, ks, vs, kw, vw, tabs, bsz, t):
    n_qb = t // Q_BLOCK
    n_ch = t // CMP_STRIDE
    tk = min(SEL_CHUNK, t)
    full = lambda a: pl.BlockSpec(a.shape, lambda b, i: (0,) * a.ndim)
    seq = pl.BlockSpec((t, KV_WIDTH), lambda b, i: (b, 0))
    cseq = pl.BlockSpec((n_ch, KV_WIDTH), lambda b, i: (b, 0))
    qspec = pl.BlockSpec((NSA_GROUP, Q_BLOCK, LANES), lambda b, i: (0, b * n_qb + i, 0))
    gr, bc, wt, mimp, esel, eg = tabs["gr"], tabs["bc"], tabs["wt"], tabs["mimp"], tabs["esel"], tabs["eg"]
    head_tile = pltpu.VMEM((NSA_HEADS * Q_BLOCK, KV_WIDTH), F32)
    nb, brow = NSA_HEADS // PIPE_HEADS, PIPE_HEADS * Q_BLOCK
    per_block = [pltpu.VMEM((brow, 1), F32)] * (2 * nb) + [pltpu.VMEM((brow, KV_WIDTH), F32)] * nb
    return pl.pallas_call(
        functools.partial(_nsa_prompt_kernel, voff_blk=tabs["voff"] // LANES),
        grid=(bsz, n_qb),
        in_specs=[qspec, pl.BlockSpec((Q_BLOCK, LG_PAD), lambda b, i: (b * n_qb + i, 0)),
                  cseq, cseq, seq, seq, seq, seq, full(gr),
                  pl.BlockSpec((1,) + bc.shape[1:], lambda b, i: (i, 0, 0, 0)), full(wt),
                  full(mimp), full(esel), full(eg)],
        out_specs=qspec,
        out_shape=jax.ShapeDtypeStruct((NSA_GROUP, bsz * t, LANES), F32),
        scratch_shapes=[pltpu.VMEM((t, KV_WIDTH), BF16), pltpu.VMEM((t, KV_WIDTH), BF16),
                        pltpu.VMEM((t + WINDOW, KV_WIDTH), BF16), pltpu.VMEM((t + WINDOW, KV_WIDTH), BF16),
                        pltpu.VMEM((n_ch, KV_WIDTH), BF16), pltpu.VMEM((n_ch, KV_WIDTH), BF16),
                        head_tile,
                        pltpu.VMEM((NSA_KV_HEADS, t // tk, Q_BLOCK, tk), F32),
                        pltpu.VMEM((NSA_HEADS * Q_BLOCK, KV_WIDTH), BF16)] + per_block,
        compiler_params=_params(("arbitrary", "arbitrary")),
        name="nsa_attention_prompt",
    )(q4, gates, kc, vc, ks, vs, kw, vw, gr, bc, wt, mimp, esel, eg)


def _nsa_sample_kernel(pt_ref, q_ref, gt_ref, kc_ref, vc_ref, ksn_ref, vsn_ref, kwn_ref, vwn_ref, bkw_ref, bvw_ref,
                       pks_hbm, pvs_hbm, gs_ref, gn_ref, bcs_ref, bws_ref, mimp_ref, esel_ref, eg_ref, o_ref,
                       bufk, bufv, sems, ksb, vsb, *, n_pages):
    sq = pl.program_id(0)
    n_seq = pl.num_programs(0)
    slot = sq % 2
    pools, bufs = (pks_hbm, pvs_hbm), (bufk, bufv)
    past = n_pages * PAGE_SIZE
    s_new = ksn_ref.shape[0]
    rows = NSA_GROUP * s_new
    n_cmp_rows = kc_ref.shape[0]
    n_sel = (past + s_new + SEL_BLOCK - 1) // SEL_BLOCK
    n_past_blk = past // SEL_BLOCK
    w_buf = bkw_ref.shape[1]

    @pl.when(sq == 0)
    def _():
        _paged_fetch(pt_ref, pools, bufs, sems, sq, slot, n_pages, True)

    @pl.when(sq + 1 < n_seq)
    def _():
        _paged_fetch(pt_ref, pools, bufs, sems, sq + 1, 1 - slot, n_pages, True)

    _paged_wait(bufs, sems, slot)
    ksb[...] = bufk[slot].astype(BF16)
    vsb[...] = bufv[slot].astype(BF16)

    lane = lax.broadcasted_iota(I32, (rows, LANES), 1)
    upper = lane >= HEAD_DIM
    qi = lax.broadcasted_iota(I32, (rows, 1), 0) % s_new
    pad_new = lambda ref: jnp.concatenate([ref[...], jnp.zeros((LANES - s_new, KV_WIDTH), F32)], axis=0).astype(BF16)
    ksn, vsn, kwn, vwn = pad_new(ksn_ref), pad_new(vsn_ref), pad_new(kwn_ref), pad_new(vwn_ref)
    kcb, vcb = kc_ref[...].astype(BF16), vc_ref[...].astype(BF16)
    bkw, bvw = bkw_ref[...].astype(BF16), bvw_ref[...].astype(BF16)
    new_causal = (lane < s_new) & (lane <= qi)

    def stacked_q(g):
        keep = upper if g == 1 else jnp.logical_not(upper)
        qs = jnp.concatenate([q_ref[r] for r in range(NSA_GROUP)], axis=0)
        return jnp.where(keep, qs, 0.0).astype(BF16)

    cend = lax.broadcasted_iota(I32, (rows, n_cmp_rows), 1) * CMP_STRIDE + (CMP_BLOCK - 1)
    valid_c = (past + qi) >= cend
    o_c, psums = [], []
    for g in range(NSA_KV_HEADS):
        s = _nt(stacked_q(g), kcb) + bcs_ref[g]
        e, den = _masked_softmax_parts(s, valid_c)
        p = e / den
        o_c.append(_dot(p.astype(BF16), vcb))
        ps = p[0:s_new]
        for r in range(1, NSA_GROUP):
            ps = ps + p[r * s_new:(r + 1) * s_new]
        psums.append(ps)
    imp = _dot(jnp.concatenate(psums, axis=0), mimp_ref[...], precision=HI)
    qpos_sel = past + lax.broadcasted_iota(I32, (NSA_KV_HEADS * s_new, 1), 0) % s_new
    sel = _select_blocks(_block_scores(imp, qpos_sel, n_sel), n_sel)
    mask_past = _dot(sel[:, :n_past_blk].astype(BF16), esel_ref[...])
    mask_add = jnp.where(mask_past > 0.5, 0.0, NEG_INF)
    sel_new = sel[:, n_past_blk:n_past_blk + 1]

    gexp = _dot(gt_ref[...], eg_ref[...], precision=HI)
    o_s, o_w = [], []
    for g in range(NSA_KV_HEADS):
        qg = stacked_q(g)
        tile_rows = lambda a: jnp.concatenate([a[g * s_new:(g + 1) * s_new]] * NSA_GROUP, axis=0)
        s_p = _dot(qg, ksb[...]) + gs_ref[g] + tile_rows(mask_add)
        valid_n = new_causal & (tile_rows(sel_new) > 0.5)
        s_n = jnp.where(valid_n, _nt(qg, ksn) + gn_ref[g], NEG_INF)
        m = jnp.maximum(jnp.max(s_p, axis=-1, keepdims=True), jnp.max(s_n, axis=-1, keepdims=True))
        e_p = jnp.exp(s_p - m)
        e_n = jnp.where(valid_n, jnp.exp(s_n - m), 0.0)
        den = jnp.maximum(jnp.sum(e_p, axis=-1, keepdims=True) + jnp.sum(e_n, axis=-1, keepdims=True), 1e-20)
        o_s.append((_nt(e_p.astype(BF16), vsb[...]) + _dot(e_n.astype(BF16), vsn)) / den)
        jb = lax.broadcasted_iota(I32, (rows, w_buf), 1)
        dist_b = w_buf + qi - jb
        valid_b = (dist_b < WINDOW) & (past - w_buf + jb >= 0)
        s_b = jnp.where(valid_b, _dot(qg, bkw) + bws_ref[g], NEG_INF)
        s_n = jnp.where(new_causal, _nt(qg, kwn) + gn_ref[g], NEG_INF)
        m = jnp.maximum(jnp.max(s_b, axis=-1, keepdims=True), jnp.max(s_n, axis=-1, keepdims=True))
        e_b = jnp.where(valid_b, jnp.exp(s_b - m), 0.0)
        e_n = jnp.where(new_causal, jnp.exp(s_n - m), 0.0)
        den = jnp.maximum(jnp.sum(e_b, axis=-1, keepdims=True) + jnp.sum(e_n, axis=-1, keepdims=True), 1e-20)
        o_w.append((_nt(e_b.astype(BF16), bvw) + _dot(e_n.astype(BF16), vwn)) / den)

    up8 = upper[0:s_new]
    for r in range(NSA_GROUP):
        pick = lambda o: jnp.where(up8, o[1][r * s_new:(r + 1) * s_new], o[0][r * s_new:(r + 1) * s_new])
        o_ref[r] = _gate_mix(gexp, pick(o_c), pick(o_s), pick(o_w), r)


def _nsa_sample(pt_flat, q4, gates, kc, vc, ksn, vsn, kwn, vwn, buf_kw, buf_vw, pool_ks, pool_vs, tabs,
                n_seq, s_new, n_pages):
    past = n_pages * PAGE_SIZE
    n_ch = past // CMP_STRIDE
    w_buf = buf_kw.shape[1]
    full = lambda a: pl.BlockSpec(a.shape, lambda s, pt: (0,) * a.ndim)
    hbm = pl.BlockSpec(memory_space=pl.ANY)
    rows = lambda n, w: pl.BlockSpec((n, w), lambda s, pt: (s, 0))
    qspec = pl.BlockSpec((NSA_GROUP, s_new, LANES), lambda s, pt: (0, s, 0))
    consts = (tabs["gs"], tabs["gn"], tabs["bcs"], tabs["bws"], tabs["mimp_s"], tabs["esel_s"], tabs["eg"])
    return pl.pallas_call(
        functools.partial(_nsa_sample_kernel, n_pages=n_pages),
        grid_spec=pltpu.PrefetchScalarGridSpec(
            num_scalar_prefetch=1,
            grid=(n_seq,),
            in_specs=[qspec, rows(s_new, LG_PAD), rows(n_ch, KV_WIDTH), rows(n_ch, KV_WIDTH)]
                     + [rows(s_new, KV_WIDTH)] * 4 + [rows(KV_WIDTH, w_buf)] * 2 + [hbm, hbm]
                     + [full(c) for c in consts],
            out_specs=qspec,
            scratch_shapes=[pltpu.VMEM((2, KV_WIDTH, past), F32), pltpu.VMEM((2, KV_WIDTH, past), F32),
                            pltpu.SemaphoreType.DMA((2, 2)),
                            pltpu.VMEM((KV_WIDTH, past), BF16), pltpu.VMEM((KV_WIDTH, past), BF16)],
        ),
        out_shape=jax.ShapeDtypeStruct((NSA_GROUP, n_seq * s_new, LANES), F32),
        compiler_params=_params(("arbitrary",)),
        name="nsa_attention_sample",
    )(pt_flat, q4, gates, kc, vc, ksn, vsn, kwn, vwn, buf_kw, buf_vw, pool_ks, pool_vs, *consts)


def _gla_kernel(q_ref, k_ref, v_ref, lg_ref, r_ref, gn_ref, s0_ref, o_ref, sfin_ref, s_scr, *, cl):
    c = pl.program_id(1)
    n_c = pl.num_programs(1)
    n_sub = q_ref.shape[0] // cl

    @pl.when(c == 0)
    def _():
        s_scr[...] = s0_ref[...]

    lane_t = lax.broadcasted_iota(I32, (cl, LANES), 1)
    row_a = lax.broadcasted_iota(I32, (cl, GLA_K_WIDTH), 0)
    causal = lax.broadcasted_iota(I32, (cl, cl), 1) <= lax.broadcasted_iota(I32, (cl, cl), 0)
    row_s = lax.broadcasted_iota(I32, (LANES, GLA_DV), 0)
    pair = LANES // GLA_DK
    n_pairs = GLA_HEADS // pair
    cols = lambda x, p: x[:, p * LANES:(p + 1) * LANES]
    vcols = lambda h: slice(h * GLA_DV, (h + 1) * GLA_DV)

    pre = []
    for ci in range(n_sub):
        rows = slice(ci * cl, (ci + 1) * cl)
        b = lg_ref[rows, :]
        sh = 1
        while sh < cl:
            b = b + jnp.where(row_a >= sh, pltpu.roll(b, sh, 0), 0.0)
            sh *= 2
        b_last = b[cl - 1:cl, :]
        qt = q_ref[rows, :] * jnp.exp(b)
        kp = k_ref[rows, :]
        kt = (kp * jnp.exp(-b)).astype(BF16)
        khat = (kp * jnp.exp(b_last - b)).astype(BF16)
        qm, intra, u = [], [], []
        for h in range(GLA_HEADS):
            vh = v_ref[rows, vcols(h)].astype(BF16)
            mine = (lane_t >= GLA_DK) if h % pair == 1 else (lane_t < GLA_DK)
            qm.append(jnp.where(mine, cols(qt, h // pair), 0.0).astype(BF16))
            att = jnp.where(causal, _nt(qm[h], cols(kt, h // pair)), 0.0).astype(BF16)
            intra.append(_dot(att, vh))
            u.append(_tn(cols(khat, h // pair), vh))
        dec = [jnp.exp(jnp.broadcast_to(cols(b_last, p), (LANES, LANES))).T for p in range(n_pairs)]
        upd = [jnp.where(row_s >= GLA_DK, u[p * pair + 1], u[p * pair]) for p in range(n_pairs)]
        pre.append((rows, qm, intra, dec, upd))

    state = [s_scr[p * LANES:(p + 1) * LANES, :] for p in range(n_pairs)]
    for rows, qm, intra, dec, upd in pre:
        s_bf = [x.astype(BF16) for x in state]
        for h in range(GLA_HEADS):
            o = _dot(qm[h], s_bf[h // pair]) + intra[h]
            o = o * lax.rsqrt(jnp.mean(o * o, axis=-1, keepdims=True) + RMS_EPS) * gn_ref[...]
            rh = r_ref[rows, vcols(h)]
            o_ref[rows, vcols(h)] = o * (rh * jax.nn.sigmoid(rh))
        state = [state[p] * dec[p] + upd[p] for p in range(n_pairs)]
    for p in range(n_pairs):
        s_scr[p * LANES:(p + 1) * LANES, :] = state[p]

    @pl.when(c == n_c - 1)
    def _():
        sfin_ref[...] = s_scr[...]


def _gla(q_l, k_l, v_l, lg, r, g_norm, s0, bsz, n_c, cl):
    srows = GLA_HEADS * GLA_DK
    per_step = math.gcd(n_c, GLA_CHUNKS_PER_STEP)
    n_steps = n_c // per_step
    blk = lambda w: pl.BlockSpec((per_step * cl, w), lambda b, c: (b * n_steps + c, 0))
    st = pl.BlockSpec((srows, GLA_DV), lambda b, c: (b, 0))
    return pl.pallas_call(
        functools.partial(_gla_kernel, cl=cl),
        grid=(bsz, n_steps),
        in_specs=[blk(GLA_K_WIDTH), blk(GLA_K_WIDTH), blk(GLA_V_WIDTH), blk(GLA_K_WIDTH), blk(GLA_V_WIDTH),
                  pl.BlockSpec(g_norm.shape, lambda b, c: (0, 0)), st],
        out_specs=[blk(GLA_V_WIDTH), st],
        out_shape=[jax.ShapeDtypeStruct((bsz * n_c * cl, GLA_V_WIDTH), F32),
                   jax.ShapeDtypeStruct((bsz * srows, GLA_DV), F32)],
        scratch_shapes=[pltpu.VMEM((srows, GLA_DV), F32)],
        compiler_params=_params(("arbitrary", "arbitrary")),
        name="gla",
    )(q_l, k_l, v_l, lg, r, g_norm, s0)


def _merge_kernel(x_ref, on_ref, og_ref, ma_ref, mb_ref, wn_ref, wg_ref, wo_ref, gf_ref, rw_ref, rb_ref,
                  x1_o, h2t_o, ei_o, gw_o, rk_o, cnt_o, carry):
    i = pl.program_id(0)
    tm = x_ref.shape[0]

    @pl.when(i == 0)
    def _():
        carry[...] = jnp.zeros(carry.shape, F32)

    on = jnp.concatenate([on_ref[r] for r in range(NSA_GROUP)], axis=1).astype(BF16)
    ya = _dot(on, wn_ref[...])
    yb = _dot(og_ref[...].astype(BF16), wg_ref[...])
    m = jax.nn.sigmoid(ma_ref[...]) * ya + jax.nn.sigmoid(mb_ref[...]) * yb
    x1 = x_ref[...] + _dot(m.astype(BF16), wo_ref[...])
    x1_o[...] = x1
    h2 = x1 * lax.rsqrt(jnp.mean(x1 * x1, axis=-1, keepdims=True) + RMS_EPS) * gf_ref[...]
    for s in range(D_MODEL // LANES):
        h2t_o[pl.ds(s, tm, stride=SUBLANES), :] = h2[:, s * LANES:(s + 1) * LANES]

    logits = _dot(h2, rw_ref[...], precision=HI) + rb_ref[...]
    lane = lax.broadcasted_iota(I32, (tm, LANES), 1)
    lane_f = lane.astype(F32)
    work = logits
    vals, idxs = [], []
    for _ in range(TOP_K):
        mk = jnp.max(work, axis=-1, keepdims=True)
        ik = jnp.min(jnp.where(work == mk, lane_f, float(LANES)), axis=-1, keepdims=True)
        vals.append(mk)
        idxs.append(ik)
        work = jnp.where(lane_f == ik, -jnp.inf, work)
    es = [jnp.exp(v - vals[0]) for v in vals]
    den = es[0]
    for e in es[1:]:
        den = den + e
    onehot = jnp.zeros((tm, LANES), F32)
    for ik in idxs:
        onehot = onehot + jnp.where(lane_f == ik, 1.0, 0.0)
    below = lax.broadcasted_iota(I32, (tm, tm), 1) < lax.broadcasted_iota(I32, (tm, tm), 0)
    before = _dot(jnp.where(below, 1.0, 0.0).astype(BF16), onehot.astype(BF16)) + carry[0:1, :]
    ei = jnp.zeros((tm, LANES), F32)
    gw = jnp.zeros((tm, LANES), F32)
    rk = jnp.zeros((tm, LANES), F32)
    for k in range(TOP_K):
        rank_k = jnp.sum(jnp.where(lane_f == idxs[k], before, 0.0), axis=-1, keepdims=True)
        ei = jnp.where(lane == k, idxs[k], ei)
        gw = jnp.where(lane == k, es[k] / den, gw)
        rk = jnp.where(lane == k, rank_k, rk)
    ei_o[...] = ei.astype(I32)
    gw_o[...] = gw
    rk_o[...] = rk.astype(I32)
    carry[0:1, :] = carry[0:1, :] + jnp.sum(onehot, axis=0, keepdims=True)
    cnt_o[...] = carry[...]


def _merge(x, o_nsa4, o_gla, m_a, m_b, mw):
    n = x.shape[0]
    tm = min(TM_MERGE, n)
    row = lambda w: pl.BlockSpec((tm, w), lambda i: (i, 0))
    full = lambda a: pl.BlockSpec(a.shape, lambda i: (0,) * a.ndim)
    ws = (mw["wn"], mw["wg"], mw["wo"], mw["gf"], mw["rw"], mw["rb"])
    return pl.pallas_call(
        _merge_kernel,
        grid=(n // tm,),
        in_specs=[row(D_MODEL), pl.BlockSpec((NSA_GROUP, tm, LANES), lambda i: (0, i, 0)), row(GLA_V_WIDTH),
                  row(D_MODEL), row(D_MODEL)] + [full(w) for w in ws],
        out_specs=[row(D_MODEL), pl.BlockSpec((tm * SUBLANES, LANES), lambda i: (i, 0)),
                   row(LANES), row(LANES), row(LANES), pl.BlockSpec((SUBLANES, LANES), lambda i: (0, 0))],
        out_shape=[jax.ShapeDtypeStruct((n, D_MODEL), F32), jax.ShapeDtypeStruct((n * SUBLANES, LANES), F32),
                   jax.ShapeDtypeStruct((n, LANES), I32), jax.ShapeDtypeStruct((n, LANES), F32),
                   jax.ShapeDtypeStruct((n, LANES), I32), jax.ShapeDtypeStruct((SUBLANES, LANES), F32)],
        scratch_shapes=[pltpu.VMEM((SUBLANES, LANES), F32)],
        compiler_params=_params(("arbitrary",)),
        name="merge_router",
    )(x, o_nsa4, o_gla, m_a, m_b, *ws)


def _token_tile(ref, row):
    return ref.at[pl.ds(row * SUBLANES, SUBLANES)]


def _dispatch_kernel(meta_ref, dest_ref, h_ref, xs_hbm, dsm, ztile, sem_idx, sem_row, sem_pad, *, td):
    i = pl.program_id(0)
    n_steps = pl.num_programs(0)
    idx_copy = pltpu.make_async_copy(dest_ref, dsm, sem_idx)
    idx_copy.start()
    idx_copy.wait()

    per_row = LANES // TOP_K

    def body(rr, _):
        for c in range(LANES):
            tk = rr * per_row + c // TOP_K
            pltpu.make_async_copy(_token_tile(h_ref, tk), _token_tile(xs_hbm, dsm[rr, c]), sem_row).start()
        return 0
    lax.fori_loop(0, td // per_row, body, 0)

    @pl.when(i == n_steps - 1)
    def _():
        ztile[...] = jnp.zeros(ztile.shape, F32)
        blk_rows = ztile.shape[0]
        n_blocks = xs_hbm.shape[0] // blk_rows

        def pads(start_or_wait):
            def per_expert(e, _):
                first = meta_ref[N_EXPERTS + e] + meta_ref[e]
                last = meta_ref[N_EXPERTS + e] + meta_ref[2 * N_EXPERTS + e]

                def per_row(rw, _):
                    cp = pltpu.make_async_copy(ztile.at[pl.ds(0, SUBLANES)], _token_tile(xs_hbm, rw), sem_pad)
                    cp.start() if start_or_wait else cp.wait()
                    return 0
                lax.fori_loop(first, last, per_row, 0)
                return 0
            lax.fori_loop(0, N_EXPERTS, per_expert, 0)

            def per_block(bk, _):
                cp = pltpu.make_async_copy(ztile, xs_hbm.at[pl.ds(bk * blk_rows, blk_rows)], sem_pad)
                cp.start() if start_or_wait else cp.wait()
                return 0
            lax.fori_loop(meta_ref[3 * N_EXPERTS], n_blocks, per_block, 0)
        pads(True)
        pads(False)

    for _ in range(TOP_K):
        pltpu.make_async_copy(h_ref, xs_hbm.at[pl.ds(0, td * SUBLANES)], sem_row).wait()


def _dispatch(meta, dest2d, h2t, n_blocks, mrows, td):
    n = h2t.shape[0] // SUBLANES
    n_rows_total = n_blocks * mrows
    hbm = pl.BlockSpec(memory_space=pl.ANY)
    drows = td * TOP_K // LANES
    return pl.pallas_call(
        functools.partial(_dispatch_kernel, td=td),
        grid_spec=pltpu.PrefetchScalarGridSpec(
            num_scalar_prefetch=1,
            grid=(n // td,),
            in_specs=[pl.BlockSpec((drows, LANES), lambda i, m: (i, 0)),
                      pl.BlockSpec((td * SUBLANES, LANES), lambda i, m: (i, 0))],
            out_specs=hbm,
            scratch_shapes=[pltpu.SMEM((drows, LANES), I32), pltpu.VMEM((mrows * SUBLANES, LANES), F32),
                            pltpu.SemaphoreType.DMA, pltpu.SemaphoreType.DMA, pltpu.SemaphoreType.DMA],
        ),
        out_shape=jax.ShapeDtypeStruct((n_rows_total * SUBLANES, LANES), F32),
        compiler_params=_params(("arbitrary",)),
        name="moe_dispatch",
    )(meta, dest2d, h2t)


def _moe_kernel(be_ref, nu_ref, xs_ref, wg_ref, bg_ref, wu_ref, bu_ref, wd_ref, bd_ref, y_ref, wg_s, wu_s, wd_s):
    i = pl.program_id(0)
    rows = xs_ref.shape[0] // SUBLANES
    n_s = D_MODEL // LANES

    @pl.when(i < nu_ref[0])
    def _():
        @pl.when((i == 0) | (be_ref[i] != be_ref[jnp.maximum(i - 1, 0)]))
        def _():
            wg_s[...] = wg_ref[0].astype(BF16)
            wu_s[...] = wu_ref[0].astype(BF16)
            wd_s[...] = wd_ref[0].astype(BF16)

        xb = jnp.concatenate([xs_ref[pl.ds(s, rows, stride=SUBLANES), :] for s in range(n_s)], axis=1).astype(BF16)
        g = _dot(xb, wg_s[...]) + bg_ref[0]
        u = _dot(xb, wu_s[...]) + bu_ref[0]
        g = jnp.minimum(g, SWIGLU_LIMIT)
        u = jnp.clip(u, -SWIGLU_LIMIT, SWIGLU_LIMIT)
        hh = (u + 1.0) * (g * jax.nn.sigmoid(SWIGLU_ALPHA * g))
        y = _dot(hh.astype(BF16), wd_s[...]) + bd_ref[0]
        for s in range(n_s):
            y_ref[pl.ds(s, rows, stride=SUBLANES), :] = y[:, s * LANES:(s + 1) * LANES]

    @pl.when(i >= nu_ref[0])
    def _():
        y_ref[...] = jnp.zeros(y_ref.shape, F32)


def _moe_experts(block_e, n_used, xs, ew, n_blocks, mrows):
    blk = lambda i, be, nu: jnp.minimum(i, nu[0] - 1)
    rows = pl.BlockSpec((mrows * SUBLANES, LANES), lambda i, be, nu: (i, 0))
    wspec = lambda a: pl.BlockSpec((1,) + a.shape[1:], lambda i, be, nu: (be[blk(i, be, nu)], 0, 0))
    ws = (ew["wg"], ew["bg"], ew["wu"], ew["bu"], ew["wd"], ew["bd"])
    return pl.pallas_call(
        _moe_kernel,
        grid_spec=pltpu.PrefetchScalarGridSpec(
            num_scalar_prefetch=2,
            grid=(n_blocks,),
            in_specs=[rows] + [wspec(w) for w in ws],
            out_specs=rows,
            scratch_shapes=[pltpu.VMEM((D_MODEL, D_FF), BF16), pltpu.VMEM((D_MODEL, D_FF), BF16),
                            pltpu.VMEM((D_FF, D_MODEL), BF16)],
        ),
        out_shape=jax.ShapeDtypeStruct(xs.shape, F32),
        compiler_params=_params(("arbitrary",)),
        name="moe_experts",
    )(block_e, n_used, xs, *ws)


def _combine_kernel(dest_ref, gw_ref, x1_ref, gfin_ref, y_hbm, out_ref, dsm, buf, sem_idx, sem_row):
    tc = x1_ref.shape[0]
    idx_copy = pltpu.make_async_copy(dest_ref, dsm, sem_idx)
    idx_copy.start()
    idx_copy.wait()

    per_row = LANES // TOP_K

    def body(rr, _):
        for c in range(LANES):
            tk = rr * per_row + c // TOP_K
            pltpu.make_async_copy(_token_tile(y_hbm, dsm[rr, c]), _token_tile(buf.at[c % TOP_K], tk), sem_row).start()
        return 0
    lax.fori_loop(0, tc // per_row, body, 0)
    for k in range(TOP_K):
        pltpu.make_async_copy(y_hbm.at[pl.ds(0, tc * SUBLANES)], buf.at[k], sem_row).wait()

    gw = gw_ref[...]
    parts = []
    for s in range(D_MODEL // LANES):
        acc = None
        for k in range(TOP_K):
            term = buf[k, pl.ds(s, tc, stride=SUBLANES), :] * gw[:, k:k + 1]
            acc = term if acc is None else acc + term
        parts.append(acc)
    x2 = x1_ref[...] + jnp.concatenate(parts, axis=1)
    out_ref[...] = x2 * lax.rsqrt(jnp.mean(x2 * x2, axis=-1, keepdims=True) + RMS_EPS) * gfin_ref[...]


def _combine(dest2d, gw, x1, g_final, y_rows):
    n = x1.shape[0]
    tc = min(TC_COMBINE, n)
    drows = tc * TOP_K // LANES
    row = lambda w: pl.BlockSpec((tc, w), lambda i: (i, 0))
    return pl.pallas_call(
        _combine_kernel,
        grid=(n // tc,),
        in_specs=[pl.BlockSpec((drows, LANES), lambda i: (i, 0)), row(LANES), row(D_MODEL),
                  pl.BlockSpec(g_final.shape, lambda i: (0, 0)), pl.BlockSpec(memory_space=pl.ANY)],
        out_specs=row(D_MODEL),
        out_shape=jax.ShapeDtypeStruct((n, D_MODEL), F32),
        scratch_shapes=[pltpu.SMEM((drows, LANES), I32), pltpu.VMEM((TOP_K, tc * SUBLANES, LANES), F32),
                        pltpu.SemaphoreType.DMA, pltpu.SemaphoreType.DMA],
        compiler_params=_params(("arbitrary",)),
        name="moe_combine",
    )(dest2d, gw, x1, g_final, y_rows)


def _bucket_table(max_dist):
    n = np.arange(max_dist, dtype=np.int64)
    scaled = np.log(np.maximum(n, 1).astype(np.float64) / REL_EXACT) / math.log(REL_MAX_DIST / REL_EXACT)
    large = REL_EXACT + (scaled * (REL_BUCKETS - REL_EXACT)).astype(np.int64)
    return np.where(n < REL_EXACT, n, np.minimum(large, REL_BUCKETS - 1)).astype(np.int32)


def _bias_lookup(rel_bias, dist):
    d = np.maximum(dist, 0)
    buckets = _bucket_table(int(d.max()) + 1)[d]
    return jnp.take(rel_bias.astype(F32).T, jnp.asarray(buckets), axis=1)


def _skew(w, n_rows, step, width):
    h, l = w.shape
    flat = jnp.tile(w, (1, n_rows))[:, :n_rows * (l - step)]
    return flat.reshape(h, n_rows, l - step)[:, :, :width]


def _importance_matrix(n_rows, n_cmp, n_sel, n_cols):
    rc = CMP_BLOCK // CMP_STRIDE
    rs = SEL_BLOCK // CMP_STRIDE
    m = np.zeros((n_rows, n_cols), np.float32)
    for j in range(n_sel):
        for o in range(rs + rc - 1):
            w = min(o - (rc - 1) + rc, rs) - max(o - (rc - 1), 0)
            c = rs * j + o - (rc - 1)
            if 0 <= c < n_cmp:
                m[c, j] += w
    return jnp.asarray(m)


def _block_expand(n_blocks, n_rows=LANES):
    e = np.zeros((n_rows, n_blocks * SEL_BLOCK), np.float32)
    for j in range(n_blocks):
        e[j, j * SEL_BLOCK:(j + 1) * SEL_BLOCK] = 1.0
    return jnp.asarray(e, dtype=BF16)


def _gate_expand():
    e = np.zeros((LG_PAD, 3 * NSA_WIDTH), np.float32)
    for g in range(NSA_KV_HEADS):
        for r in range(NSA_GROUP):
            for c in range(3):
                lo = c * NSA_WIDTH + r * LANES + g * HEAD_DIM
                e[(g * NSA_GROUP + r) * 3 + c, lo:lo + HEAD_DIM] = 1.0
    return jnp.asarray(e)


def _prompt_tables(rel_bias, t):
    n_qb = t // Q_BLOCK
    n_ch = t // CMP_STRIDE
    tk = min(SEL_CHUNK, t)
    voff = max(t - Q_BLOCK, WINDOW)
    nv = voff // LANES + tk // LANES
    wd = nv * LANES
    wv = _bias_lookup(rel_bias, np.concatenate([voff - np.arange(wd), voff + np.arange(Q_BLOCK, 0, -1)]))
    gr = _skew(wv, Q_BLOCK, 1, wd).reshape(NSA_HEADS, Q_BLOCK, nv, LANES).transpose(2, 0, 1, 3)
    wc = _bias_lookup(rel_bias, np.concatenate([np.arange(t) - (CMP_BLOCK - 1), np.zeros(CMP_STRIDE * n_ch, np.int64)]))
    cend = np.arange(n_ch) * CMP_STRIDE + (CMP_BLOCK - 1)
    cmask = np.where(np.arange(t)[:, None] >= cend[None, :], 0.0, NEG_INF).astype(np.float32)
    bc = _skew(wc, n_ch, CMP_STRIDE, t).transpose(0, 2, 1) + cmask[None]
    bc = bc.reshape(NSA_HEADS, n_qb, Q_BLOCK, n_ch).transpose(1, 0, 2, 3)
    n_w = WINDOW + Q_BLOCK
    ww = _bias_lookup(rel_bias, np.concatenate([WINDOW - np.arange(n_w), WINDOW + np.arange(Q_BLOCK, 0, -1)]))
    dist_w = WINDOW + np.arange(Q_BLOCK)[:, None] - np.arange(n_w)[None, :]
    wmask = np.where((dist_w >= 0) & (dist_w < WINDOW), 0.0, NEG_INF).astype(np.float32)
    wt = _skew(ww, Q_BLOCK, 1, n_w) + wmask[None]
    return dict(gr=gr, bc=bc, wt=wt, voff=voff,
                mimp=_importance_matrix(n_ch, n_ch - 1, t // SEL_BLOCK, LANES),
                esel=_block_expand(t // SEL_BLOCK), eg=_gate_expand())


def _sample_tables(rel_bias, past, s_new, w_buf):
    n_ch = past // CMP_STRIDE
    n_sel = (past + s_new + SEL_BLOCK - 1) // SEL_BLOCK
    per_group = lambda a: a.reshape(NSA_KV_HEADS, NSA_GROUP, s_new, -1).reshape(NSA_KV_HEADS, NSA_GROUP * s_new, -1)
    qi = np.arange(s_new)

    def table(dist):
        return per_group(_bias_lookup(rel_bias, dist))
    ws = _bias_lookup(rel_bias, np.concatenate([past - np.arange(past), past + np.arange(s_new, 0, -1)]))
    gs = per_group(_skew(ws, s_new, 1, past))
    jn = np.arange(LANES)
    gn = table(np.where(jn[None, :] < s_new, qi[:, None] - jn[None, :], 0))
    cend = np.arange(n_ch) * CMP_STRIDE + (CMP_BLOCK - 1)
    bcs = table(past + qi[:, None] - cend[None, :])
    bws = table(w_buf + qi[:, None] - np.arange(w_buf)[None, :])
    sel_lanes = -(-n_sel // LANES) * LANES
    return dict(gs=gs, gn=gn, bcs=bcs, bws=bws,
                mimp_s=_importance_matrix(n_ch, n_ch - 1, n_sel, sel_lanes),
                esel_s=_block_expand(past // SEL_BLOCK, past // SEL_BLOCK), eg=_gate_expand())


def _compress_weights(pe, w1, w2):
    rc = CMP_BLOCK // CMP_STRIDE
    w1r = w1.reshape(rc, CMP_STRIDE, HEAD_DIM, CMP_HIDDEN)
    eye = jnp.eye(NSA_KV_HEADS, dtype=w1.dtype)
    wcat = jnp.einsum("rldh,ge->lgdreh", w1r, eye).reshape(CMP_STRIDE // 2, 2 * KV_WIDTH,
                                                           rc * NSA_KV_HEADS * CMP_HIDDEN)
    w2bd = jnp.einsum("hd,ge->ghed", w2, eye).reshape(NSA_KV_HEADS * CMP_HIDDEN, KV_WIDTH)
    pe_rows = jnp.concatenate([pe.reshape(1, CMP_BLOCK * HEAD_DIM),
                               jnp.zeros((SUBLANES - 1, CMP_BLOCK * HEAD_DIM), pe.dtype)], axis=0)
    return wcat.astype(BF16), pe_rows, w1.reshape(CMP_BLOCK * HEAD_DIM, CMP_HIDDEN), w2bd.astype(BF16)


def _layer_weights(w_in, gla_w_alpha, gla_b_alpha, w_branch_nsa, norm_ffn, router_w, router_b):
    offs = np.cumsum((0,) + IN_SPLITS)
    col = lambda j: w_in[:, offs[j]:offs[j + 1]]
    q_perm = col(0).reshape(D_MODEL, NSA_KV_HEADS, NSA_GROUP, HEAD_DIM).transpose(0, 2, 1, 3).reshape(D_MODEL, NSA_WIDTH)
    pad = jnp.zeros((D_MODEL, LG_PAD - 3 * NSA_HEADS - GLA_RANK), w_in.dtype)
    w_p = jnp.concatenate([q_perm, col(1), col(3), col(4), col(5), col(7), col(8), col(9), col(2), col(6), pad],
                          axis=1).astype(BF16)
    w_al = jnp.zeros((LG_PAD, GLA_K_WIDTH), F32).at[3 * NSA_HEADS:3 * NSA_HEADS + GLA_RANK].set(gla_w_alpha)
    wn = w_branch_nsa.reshape(NSA_KV_HEADS, NSA_GROUP, HEAD_DIM, D_MODEL).transpose(1, 0, 2, 3).reshape(NSA_WIDTH, D_MODEL)
    rw = jnp.concatenate([router_w, jnp.zeros((D_MODEL, LANES - N_EXPERTS), F32)], axis=1)
    rb = jnp.concatenate([router_b, jnp.full((LANES - N_EXPERTS,), NEG_INF, F32)]).reshape(1, LANES)
    return w_p, w_al, gla_b_alpha.reshape(1, GLA_K_WIDTH), wn.astype(BF16), norm_ffn.reshape(1, D_MODEL), rw, rb


def _moe(h2t, ei, gw, rk, counts, x1, g_final, ew):
    n = x1.shape[0]
    nk = n * TOP_K
    counts = counts[0, :N_EXPERTS].astype(I32)
    mrows = MOE_ROWS if nk // N_EXPERTS >= 4 * MOE_ROWS else MOE_ROWS_SMALL
    padded = (counts + mrows - 1) // mrows * mrows
    pends = jnp.cumsum(padded)
    pstarts = pends - padded
    n_blocks = (nk + N_EXPERTS * (mrows - 1) + mrows - 1) // mrows
    blk_start = jnp.arange(n_blocks, dtype=I32) * mrows
    block_e = jnp.minimum(jnp.sum((pends[None, :] <= blk_start[:, None]).astype(I32), axis=1), N_EXPERTS - 1)
    n_used = (pends[-1] // mrows).astype(I32).reshape(1)
    e_sel = ei[:, :TOP_K, None] == jnp.arange(N_EXPERTS, dtype=I32)
    dest = jnp.sum(jnp.where(e_sel, pstarts.astype(I32), 0), axis=-1) + rk[:, :TOP_K]
    dest2d = dest.reshape(nk // LANES, LANES)
    meta = jnp.concatenate([counts, pstarts, padded, n_used]).astype(I32)
    td = min(TD_DISPATCH, n)
    xs = _dispatch(meta, dest2d, h2t, n_blocks, mrows, td)
    y_rows = _moe_experts(block_e, n_used, xs, ew, n_blocks, mrows)
    return _combine(dest2d, gw, x1, g_final, y_rows)


def kernel(x_prompt, x_sample, cache_cmp_k, cache_cmp_v, cache_sel_k, cache_sel_v, state_win_k, state_win_v, state_gla, page_table, rel_bias, norm_mix, w_in, nsa_pe_k, nsa_pe_v, nsa_w1_k, nsa_w1_v, nsa_w2_k, nsa_w2_v, gla_w_alpha, gla_b_alpha, gla_norm, w_branch_nsa, w_branch_gla, w_out, norm_ffn, router_w, router_b, exp_w_gate, exp_b_gate, exp_w_up, exp_b_up, exp_w_down, exp_b_down, norm_final):
    depth = w_in.shape[0]
    assert depth == 1, "single-layer trunk"
    bsz, t, d = x_prompt.shape
    n_seq, s_new, _ = x_sample.shape
    n_pages = page_table.shape[1]
    past = n_pages * PAGE_SIZE
    w_buf = state_win_k.shape[2]
    assert d == D_MODEL and t % Q_BLOCK == 0 and t % GLA_CHUNK == 0

    w_p, w_al, b_al, wn, gf, rw, rb = _layer_weights(w_in[0], gla_w_alpha[0], gla_b_alpha[0], w_branch_nsa[0],
                                                     norm_ffn[0], router_w[0], router_b[0])
    g_mix = norm_mix[0].reshape(1, D_MODEL)
    g_fin = norm_final.reshape(1, D_MODEL)
    g_gla = gla_norm[0].reshape(1, GLA_DV)
    mw = dict(wn=wn, wg=w_branch_gla[0].astype(BF16), wo=w_out[0].astype(BF16), gf=gf, rw=rw, rb=rb)
    ew = dict(wg=exp_w_gate[0], bg=exp_b_gate[0].reshape(N_EXPERTS, 1, D_FF),
              wu=exp_w_up[0], bu=exp_b_up[0].reshape(N_EXPERTS, 1, D_FF),
              wd=exp_w_down[0], bd=exp_b_down[0].reshape(N_EXPERTS, 1, D_MODEL))
    cw = {}
    for nm, pe, w1, w2 in (("k", nsa_pe_k[0], nsa_w1_k[0], nsa_w2_k[0]), ("v", nsa_pe_v[0], nsa_w1_v[0], nsa_w2_v[0])):
        cw["wcat_" + nm], cw["pe_" + nm], cw["w1f_" + nm], cw["w2_" + nm] = _compress_weights(pe, w1, w2)

    xp = x_prompt.reshape(bsz * t, d)
    (q4, kck, kcv, ksk, ksv, kwk, kwv, gates, q_l, k_l, v_l, lg, r_l, m_a, m_b, *kv_t) = _in_projection(
        xp, g_mix, w_p, w_al, b_al, seq_len=t)
    kc, vc = _compress_prompt(kck, kcv, cw, bsz, t)
    o_nsa = _nsa_prompt(q4, gates, kc, vc, ksk, ksv, kwk, kwv, _prompt_tables(rel_bias, t), bsz, t)
    s_zero = jnp.zeros((bsz * GLA_HEADS * GLA_DK, GLA_DV), F32)
    o_gla, p_gla = _gla(q_l, k_l, v_l, lg, r_l, g_gla, s_zero, bsz, t // GLA_CHUNK, GLA_CHUNK)
    x1, h2t, ei, gw, rk, counts = _merge(xp, o_nsa, o_gla, m_a, m_b, mw)
    y_prompt = _moe(h2t, ei, gw, rk, counts, x1, g_fin, ew).reshape(bsz, t, d)

    w_len = min(WINDOW, t)
    rows_of = lambda a: a.reshape(bsz, NSA_KV_HEADS, HEAD_DIM, t).transpose(0, 3, 1, 2)[None]
    p_states = tuple(rows_of(a) for a in kv_t[:4]) + tuple(rows_of(a)[:, :, t - w_len:] for a in kv_t[4:]) + (
        p_gla.reshape(1, bsz, GLA_HEADS, GLA_DK, GLA_DV),)

    xs = x_sample.reshape(n_seq * s_new, d)
    (q4, kck, kcv, ksk, ksv, kwk, kwv, gates, q_l, k_l, v_l, lg, r_l, m_a, m_b) = _in_projection(xs, g_mix, w_p, w_al, b_al)
    pt_flat = page_table.reshape(n_seq * n_pages).astype(I32)
    pool = lambda c: c[0].transpose(0, 2, 3, 1).reshape(-1, PAGE_SIZE)
    kc, vc = _compress_sample(pt_flat, pool(cache_cmp_k), pool(cache_cmp_v), cw, n_seq, n_pages)
    buf_kw = state_win_k[0].transpose(0, 2, 3, 1).reshape(n_seq * KV_WIDTH, w_buf)
    buf_vw = state_win_v[0].transpose(0, 2, 3, 1).reshape(n_seq * KV_WIDTH, w_buf)
    o_nsa = _nsa_sample(pt_flat, q4, gates, kc, vc, ksk, ksv, kwk, kwv, buf_kw, buf_vw,
                        pool(cache_sel_k), pool(cache_sel_v), _sample_tables(rel_bias, past, s_new, w_buf),
                        n_seq, s_new, n_pages)
    cl = 16
    padc = lambda a: jnp.pad(a.reshape(n_seq, s_new, -1), ((0, 0), (0, cl - s_new), (0, 0))).reshape(n_seq * cl, -1)
    s_in = state_gla[0].reshape(n_seq * GLA_HEADS * GLA_DK, GLA_DV)
    o_gla, s_gla = _gla(padc(q_l), padc(k_l), padc(v_l), padc(lg), padc(r_l), g_gla, s_in, n_seq, 1, cl)
    o_gla = o_gla.reshape(n_seq, cl, GLA_V_WIDTH)[:, :s_new].reshape(n_seq * s_new, GLA_V_WIDTH)
    x1, h2t, ei, gw, rk, counts = _merge(xs, o_nsa, o_gla, m_a, m_b, mw)
    y_sample = _moe(h2t, ei, gw, rk, counts, x1, g_fin, ew).reshape(n_seq, s_new, d)

    kvs = (1, n_seq, s_new, NSA_KV_HEADS, HEAD_DIM)
    new_win = lambda buf, new: jnp.concatenate([buf, new.reshape(kvs).astype(buf.dtype)], axis=2)[:, :, s_new:]
    s_states = (kck.reshape(kvs), kcv.reshape(kvs), ksk.reshape(kvs), ksv.reshape(kvs),
                new_win(state_win_k, kwk), new_win(state_win_v, kwv),
                s_gla.reshape(1, n_seq, GLA_HEADS, GLA_DK, GLA_DV))
    return (y_prompt, y_sample) + p_states + s_states
```

```python
import functools
import math

import numpy as np
import jax
import jax.numpy as jnp
from jax import lax
from jax.experimental import pallas as pl
from jax.experimental.pallas import tpu as pltpu

F32 = jnp.float32
BF16 = jnp.bfloat16
I32 = jnp.int32
HI = lax.Precision.HIGHEST

D_MODEL = 1024
PAGE_SIZE = 128
NSA_HEADS = 8
NSA_KV_HEADS = 2
NSA_GROUP = NSA_HEADS // NSA_KV_HEADS
HEAD_DIM = 64
NSA_WIDTH = NSA_HEADS * HEAD_DIM
KV_WIDTH = NSA_KV_HEADS * HEAD_DIM
CMP_BLOCK = 32
CMP_STRIDE = 16
CMP_HIDDEN = 2 * HEAD_DIM
SEL_BLOCK = 64
N_SEL = 16
WINDOW = 512
Q_BLOCK = 128
SEL_FORCE = 1e9
GLA_HEADS = 4
GLA_DK = 64
GLA_DV = 128
GLA_K_WIDTH = GLA_HEADS * GLA_DK
GLA_V_WIDTH = GLA_HEADS * GLA_DV
GLA_RANK = 16
GLA_TAU = 16.0
GLA_CHUNK = 64
N_EXPERTS = 32
TOP_K = 4
D_FF = D_MODEL
SWIGLU_ALPHA = 1.702
SWIGLU_LIMIT = 7.0
REL_BUCKETS = 32
REL_EXACT = REL_BUCKETS // 2
REL_MAX_DIST = 1024
RMS_EPS = 1e-6
NEG_INF = -1e30
IN_SPLITS = (NSA_WIDTH, 6 * KV_WIDTH, 3 * NSA_HEADS, GLA_K_WIDTH, GLA_K_WIDTH, GLA_V_WIDTH, GLA_RANK,
             GLA_V_WIDTH, D_MODEL, D_MODEL)

LANES = 128
SUBLANES = 8
VMEM_LIMIT = 56 * 1024 * 1024

TM_PROJ = 256
TM_MERGE = 512
MOE_ROWS = 512
MOE_ROWS_SMALL = 256
TD_DISPATCH = 512
TC_COMBINE = 256
SEL_CHUNK = 1024
PIPE_HEADS = 2
GLA_CHUNKS_PER_STEP = 4
LG_PAD = LANES


def _nt(a, b, **kw):
    return lax.dot_general(a, b, (((1,), (1,)), ((), ())), preferred_element_type=F32, **kw)


def _tn(a, b, **kw):
    return lax.dot_general(a, b, (((0,), (0,)), ((), ())), preferred_element_type=F32, **kw)


def _dot(a, b, **kw):
    return jnp.dot(a, b, preferred_element_type=F32, **kw)


def _params(sem, vmem=VMEM_LIMIT):
    return pltpu.CompilerParams(dimension_semantics=sem, vmem_limit_bytes=vmem)


def _masked_softmax_parts(s, valid):
    s = jnp.where(valid, s, NEG_INF)
    m = jnp.max(s, axis=-1, keepdims=True)
    e = jnp.where(valid, jnp.exp(s - m), 0.0)
    return e, jnp.maximum(jnp.sum(e, axis=-1, keepdims=True), 1e-20)


_OFF_Q = 0
_OFF_KV = _OFF_Q + NSA_WIDTH
_OFF_QL = _OFF_KV + 6 * KV_WIDTH
_OFF_KL = _OFF_QL + GLA_K_WIDTH
_OFF_VL = _OFF_KL + GLA_K_WIDTH
_OFF_R = _OFF_VL + GLA_V_WIDTH
_OFF_MA = _OFF_R + GLA_V_WIDTH
_OFF_MB = _OFF_MA + D_MODEL
_OFF_GA = _OFF_MB + D_MODEL
_N_PROJ = _OFF_GA + LG_PAD


def _inproj_kernel(x_ref, g_ref, w_ref, wal_ref, bal_ref,
                   q_o, kck_o, kcv_o, ksk_o, ksv_o, kwk_o, kwv_o, gt_o, ql_o, kl_o, vl_o, lg_o, r_o, ma_o, mb_o,
                   *kv_t_o):
    x = x_ref[...]
    xn = x * lax.rsqrt(jnp.mean(x * x, axis=-1, keepdims=True) + RMS_EPS)
    xn = (xn * g_ref[...]).astype(BF16)

    def mm(lo, n):
        return _dot(xn, w_ref[:, lo:lo + n])

    q = mm(_OFF_Q, NSA_WIDTH) * (HEAD_DIM ** -0.5)
    for r in range(NSA_GROUP):
        q_o[r] = q[:, r * LANES:(r + 1) * LANES]
    kv = mm(_OFF_KV, 6 * KV_WIDTH)
    for j, o in enumerate((kck_o, kcv_o, ksk_o, ksv_o, kwk_o, kwv_o)):
        o[...] = kv[:, j * KV_WIDTH:(j + 1) * KV_WIDTH]
    for j, o in enumerate(kv_t_o):
        o[...] = kv[:, j * KV_WIDTH:(j + 1) * KV_WIDTH].T
    ql_o[...] = mm(_OFF_QL, GLA_K_WIDTH) * (GLA_DK ** -0.5)
    kl_o[...] = mm(_OFF_KL, GLA_K_WIDTH)
    vl_o[...] = mm(_OFF_VL, GLA_V_WIDTH)
    r_o[...] = mm(_OFF_R, GLA_V_WIDTH)
    ma_o[...] = mm(_OFF_MA, D_MODEL)
    mb_o[...] = mm(_OFF_MB, D_MODEL)
    ga = mm(_OFF_GA, LG_PAD)
    gt_o[...] = jax.nn.sigmoid(ga)
    al = _dot(ga, wal_ref[...], precision=HI) + bal_ref[...]
    lg_o[...] = (jnp.minimum(al, 0.0) - jnp.log1p(jnp.exp(-jnp.abs(al)))) * (1.0 / GLA_TAU)


def _in_projection(x, norm_g, w_p, w_al, b_al, seq_len=None):
    n = x.shape[0]
    tm = TM_PROJ
    assert n % tm == 0
    row = lambda w: pl.BlockSpec((tm, w), lambda i: (i, 0))
    full = lambda a: pl.BlockSpec(a.shape, lambda i: (0,) * a.ndim)
    widths = (KV_WIDTH,) * 6 + (LG_PAD, GLA_K_WIDTH, GLA_K_WIDTH, GLA_V_WIDTH, GLA_K_WIDTH, GLA_V_WIDTH,
                                 D_MODEL, D_MODEL)
    out_shape = [jax.ShapeDtypeStruct((NSA_GROUP, n, LANES), F32)] + [jax.ShapeDtypeStruct((n, w), F32) for w in widths]
    out_specs = [pl.BlockSpec((NSA_GROUP, tm, LANES), lambda i: (0, i, 0))] + [row(w) for w in widths]
    if seq_len is not None:
        assert seq_len % tm == 0
        per_seq = seq_len // tm
        out_shape += [jax.ShapeDtypeStruct((n // seq_len * KV_WIDTH, seq_len), F32)] * 6
        out_specs += [pl.BlockSpec((KV_WIDTH, tm), lambda i: (i // per_seq, i % per_seq))] * 6
    return pl.pallas_call(
        _inproj_kernel,
        grid=(n // tm,),
        in_specs=[row(D_MODEL), full(norm_g), full(w_p), full(w_al), full(b_al)],
        out_specs=out_specs,
        out_shape=out_shape,
        compiler_params=_params(("arbitrary",)),
        name="in_projection",
    )(x, norm_g, w_p, w_al, b_al)


def _gelu_tanh(x):
    return 0.5 * x * (1.0 + jnp.tanh(math.sqrt(2.0 / math.pi) * (x + 0.044715 * (x * x * x))))


def _compress_rows(src, n_ch, wcat_ref, pe_ref, w1f_ref, w2_ref):
    hid2 = NSA_KV_HEADS * CMP_HIDDEN
    acc = jnp.zeros((n_ch, 2 * hid2), F32)
    for l in range(0, CMP_STRIDE, 2):
        xl = jnp.concatenate([src[pl.ds(l, n_ch, stride=CMP_STRIDE), :],
                              src[pl.ds(l + 1, n_ch, stride=CMP_STRIDE), :]], axis=1).astype(BF16)
        acc = acc + _dot(xl, wcat_ref[l // 2])
    bias = _dot(pe_ref[...], w1f_ref[...], precision=HI)[0:1]
    bias2 = jnp.concatenate([bias] * NSA_KV_HEADS, axis=1)
    nxt = pltpu.roll(acc[:, hid2:], n_ch - 1, 0)
    h = acc[:, :hid2] + nxt + bias2
    return _dot(_gelu_tanh(h).astype(BF16), w2_ref[...])


def _compress_prompt_kernel(k_ref, v_ref, wk_ref, wv_ref, pek_ref, pev_ref, w1k_ref, w1v_ref, w2k_ref, w2v_ref,
                            kc_o, vc_o):
    n_ch = kc_o.shape[0]
    kc_o[...] = _compress_rows(k_ref, n_ch, wk_ref, pek_ref, w1k_ref, w2k_ref)
    vc_o[...] = _compress_rows(v_ref, n_ch, wv_ref, pev_ref, w1v_ref, w2v_ref)


def _compress_prompt(k_cmp, v_cmp, cw, bsz, t):
    n_ch = t // CMP_STRIDE
    full = lambda a: pl.BlockSpec(a.shape, lambda b: (0,) * a.ndim)
    seq = pl.BlockSpec((t, KV_WIDTH), lambda b: (b, 0))
    out = pl.BlockSpec((n_ch, KV_WIDTH), lambda b: (b, 0))
    ws = (cw["wcat_k"], cw["wcat_v"], cw["pe_k"], cw["pe_v"], cw["w1f_k"], cw["w1f_v"], cw["w2_k"], cw["w2_v"])
    return pl.pallas_call(
        _compress_prompt_kernel,
        grid=(bsz,),
        in_specs=[seq, seq] + [full(w) for w in ws],
        out_specs=[out, out],
        out_shape=[jax.ShapeDtypeStruct((bsz * n_ch, KV_WIDTH), F32)] * 2,
        compiler_params=_params(("arbitrary",)),
        name="nsa_compress_prompt",
    )(k_cmp, v_cmp, *ws)


def _paged_fetch(pt_ref, pools, bufs, sems, seq, slot, n_pages, pages_on_lanes):
    def body(p, _):
        pg = pt_ref[seq * n_pages + p]
        off = pl.multiple_of(p * PAGE_SIZE, PAGE_SIZE)
        for j, (pool, buf) in enumerate(zip(pools, bufs)):
            dst = buf.at[slot, :, pl.ds(off, PAGE_SIZE)] if pages_on_lanes else buf.at[slot, pl.ds(off, PAGE_SIZE)]
            pltpu.make_async_copy(pool.at[pl.ds(pg * KV_WIDTH, KV_WIDTH)], dst, sems.at[j, slot]).start()
        return 0
    lax.fori_loop(0, n_pages, body, 0)


def _paged_wait(bufs, sems, slot):
    for j, buf in enumerate(bufs):
        pltpu.make_async_copy(buf.at[slot], buf.at[slot], sems.at[j, slot]).wait()


def _compress_sample_kernel(pt_ref, pk_hbm, pv_hbm, wk_ref, wv_ref, pek_ref, pev_ref, w1k_ref, w1v_ref,
                            w2k_ref, w2v_ref, kc_o, vc_o, bufk, bufv, sems, rows_k, rows_v, *, n_pages):
    s = pl.program_id(0)
    n_seq = pl.num_programs(0)
    slot = s % 2
    pools, bufs = (pk_hbm, pv_hbm), (bufk, bufv)

    @pl.when(s == 0)
    def _():
        _paged_fetch(pt_ref, pools, bufs, sems, s, slot, n_pages, False)

    @pl.when(s + 1 < n_seq)
    def _():
        _paged_fetch(pt_ref, pools, bufs, sems, s + 1, 1 - slot, n_pages, False)

    _paged_wait(bufs, sems, slot)
    for p in range(n_pages):
        rows = slice(p * PAGE_SIZE, (p + 1) * PAGE_SIZE)
        rows_k[rows, :] = bufk[slot, rows, :].T
        rows_v[rows, :] = bufv[slot, rows, :].T
    n_ch = kc_o.shape[0]
    kc_o[...] = _compress_rows(rows_k, n_ch, wk_ref, pek_ref, w1k_ref, w2k_ref)
    vc_o[...] = _compress_rows(rows_v, n_ch, wv_ref, pev_ref, w1v_ref, w2v_ref)


def _compress_sample(pt_flat, pool_k, pool_v, cw, n_seq, n_pages):
    past = n_pages * PAGE_SIZE
    n_ch = past // CMP_STRIDE
    full = lambda a: pl.BlockSpec(a.shape, lambda s, pt: (0,) * a.ndim)
    hbm = pl.BlockSpec(memory_space=pl.ANY)
    out = pl.BlockSpec((n_ch, KV_WIDTH), lambda s, pt: (s, 0))
    ws = (cw["wcat_k"], cw["wcat_v"], cw["pe_k"], cw["pe_v"], cw["w1f_k"], cw["w1f_v"], cw["w2_k"], cw["w2_v"])
    return pl.pallas_call(
        functools.partial(_compress_sample_kernel, n_pages=n_pages),
        grid_spec=pltpu.PrefetchScalarGridSpec(
            num_scalar_prefetch=1,
            grid=(n_seq,),
            in_specs=[hbm, hbm] + [full(w) for w in ws],
            out_specs=[out, out],
            scratch_shapes=[pltpu.VMEM((2, past, KV_WIDTH), F32), pltpu.VMEM((2, past, KV_WIDTH), F32),
                            pltpu.SemaphoreType.DMA((2, 2)),
                            pltpu.VMEM((past, KV_WIDTH), F32), pltpu.VMEM((past, KV_WIDTH), F32)],
        ),
        out_shape=[jax.ShapeDtypeStruct((n_seq * n_ch, KV_WIDTH), F32)] * 2,
        compiler_params=_params(("arbitrary",)),
        name="nsa_compress_sample",
    )(pt_flat, pool_k, pool_v, *ws)


def _select_blocks(score, n_sel):
    blk = lax.broadcasted_iota(I32, score.shape, 1)
    cnt = jnp.zeros(score.shape, F32)
    for i in range(n_sel):
        col = score[:, i:i + 1]
        ahead = (col > score) | ((col == score) & (i < blk))
        cnt = cnt + jnp.where(ahead, 1.0, 0.0)
    n_top = min(N_SEL, n_sel)
    return jnp.where((cnt < n_top) & (blk < n_sel), 1.0, 0.0)


def _select_blocks_t(score_t, n_sel):
    blk = lax.broadcasted_iota(I32, score_t.shape, 0)
    cnt = jnp.zeros(score_t.shape, F32)
    for i in range(n_sel):
        row = jnp.broadcast_to(score_t[i:i + 1, :], score_t.shape)
        ahead = (row > score_t) | ((row == score_t) & (i < blk))
        cnt = cnt + jnp.where(ahead, 1.0, 0.0)
    return jnp.where(cnt < min(N_SEL, n_sel), 1.0, 0.0)


def _block_scores(imp, qpos, n_sel):
    blk = lax.broadcasted_iota(I32, imp.shape, 1)
    cur = qpos // SEL_BLOCK
    forced = (blk == 0) | (blk == cur) | (blk == cur - 1)
    causal = (blk * SEL_BLOCK) <= qpos
    score = jnp.where(causal, jnp.where(forced, SEL_FORCE, imp), -SEL_FORCE)
    return jnp.where(blk < n_sel, score, -3e38)


def _gate_mix(gexp, o_c, o_s, o_w, r):
    out = None
    for c, o in enumerate((o_c, o_s, o_w)):
        term = gexp[:, c * NSA_WIDTH + r * LANES:c * NSA_WIDTH + (r + 1) * LANES] * o
        out = term if out is None else out + term
    return out


def _nsa_prompt_kernel(q_ref, gt_ref, kc_ref, vc_ref, ks_ref, vs_ref, kw_ref, vw_ref,
                       gr_ref, bc_ref, wt_ref, mimp_ref, esel_ref, eg_ref, o_ref,
                       ksb, vsb, kwb, vwb, kcb, vcb, oc_s, msk_s, qs, *state, voff_blk):
    qb = pl.program_id(1)
    t = ks_ref.shape[0]
    n_ch = kc_ref.shape[0]
    n_sel = t // SEL_BLOCK
    tk = msk_s.shape[3]
    n_kc = t // tk
    tiles = tk // LANES

    @pl.when(qb == 0)
    def _():
        ksb[...] = ks_ref[...].astype(BF16)
        vsb[...] = vs_ref[...].astype(BF16)
        kwb[0:WINDOW, :] = jnp.zeros((WINDOW, KV_WIDTH), BF16)
        vwb[0:WINDOW, :] = jnp.zeros((WINDOW, KV_WIDTH), BF16)
        kwb[WINDOW:, :] = kw_ref[...].astype(BF16)
        vwb[WINDOW:, :] = vw_ref[...].astype(BF16)
        kcb[...] = kc_ref[...].astype(BF16)
        vcb[...] = vc_ref[...].astype(BF16)

    q0 = qb * Q_BLOCK
    lane = lax.broadcasted_iota(I32, (Q_BLOCK, LANES), 1)
    qpos = q0 + lax.broadcasted_iota(I32, (Q_BLOCK, 1), 0)
    upper = lane >= HEAD_DIM

    nh = NSA_HEADS
    rows = nh * Q_BLOCK
    for h in range(nh):
        keep = upper if h >= NSA_GROUP else jnp.logical_not(upper)
        qs[h * Q_BLOCK:(h + 1) * Q_BLOCK, :] = jnp.where(keep, q_ref[h % NSA_GROUP], 0.0).astype(BF16)
    q_all = qs[...]

    def per_group(x3, add2):
        return jnp.concatenate([x3[g * NSA_GROUP:(g + 1) * NSA_GROUP] + add2[g][None]
                                for g in range(NSA_KV_HEADS)], axis=0)

    any_c = (qpos >= (CMP_BLOCK - 1))[None]
    s = _nt(q_all, kcb[...]).reshape(nh, Q_BLOCK, n_ch) + bc_ref[0]
    e = jnp.exp(s - jnp.max(s, axis=-1, keepdims=True))
    den = jnp.maximum(jnp.sum(e, axis=-1, keepdims=True), 1e-20)
    p = e * jnp.where(any_c, 1.0 / den, 0.0)
    oc_s[...] = _dot(p.reshape(rows, n_ch).astype(BF16), vcb[...])
    q_end = q0 + Q_BLOCK
    must_rank = (q_end - 1) // SEL_BLOCK + 1 > N_SEL
    for c in range(n_kc):
        kpos = c * tk + lax.broadcasted_iota(I32, (Q_BLOCK, tk), 1)

        @pl.when((c * tk < q_end) & jnp.logical_not(must_rank))
        def _(c=c, kpos=kpos):
            causal_only = jnp.where(kpos <= qpos, 0.0, NEG_INF)
            for g in range(NSA_KV_HEADS):
                msk_s[g, c] = causal_only

    @pl.when(must_rank)
    def _():
        for g in range(NSA_KV_HEADS):
            psum = p[g * NSA_GROUP]
            for r in range(1, NSA_GROUP):
                psum = psum + p[g * NSA_GROUP + r]
            imp = _dot(psum, mimp_ref[...], precision=HI)
            sel_t = _select_blocks_t(_block_scores(imp, qpos, n_sel).T[0:n_sel], n_sel).astype(BF16)
            for c in range(n_kc):
                @pl.when(c * tk < q_end)
                def _(c=c, g=g, sel_t=sel_t):
                    kpos = c * tk + lax.broadcasted_iota(I32, (Q_BLOCK, tk), 1)
                    picked = _tn(sel_t, esel_ref[0:n_sel, c * tk:(c + 1) * tk]) > 0.5
                    msk_s[g, c] = jnp.where(picked & (kpos <= qpos), 0.0, NEG_INF)

    ph = PIPE_HEADS
    nb = nh // ph
    brow = ph * Q_BLOCK
    m_r, l_r, a_r = state[0:nb], state[nb:2 * nb], state[2 * nb:3 * nb]
    for b in range(nb):
        m_r[b][...] = jnp.full((brow, 1), NEG_INF, F32)
        l_r[b][...] = jnp.zeros((brow, 1), F32)
        a_r[b][...] = jnp.zeros((brow, KV_WIDTH), F32)
    q_blk = lambda b: qs[b * brow:(b + 1) * brow, :]

    def chunk(kt, _):
        k0 = pl.multiple_of(kt * tk, tk)
        kk = ksb[pl.ds(k0, tk), :]
        vv = vsb[pl.ds(k0, tk), :]
        base = voff_blk - qb + kt * tiles

        def scores(b):
            bias = jnp.concatenate([gr_ref[base + j, b * ph:(b + 1) * ph] for j in range(tiles)], axis=-1)
            s = _nt(q_blk(b), kk).reshape(ph, Q_BLOCK, tk) + bias + msk_s[(b * ph) // NSA_GROUP, kt][None]
            return s.reshape(brow, tk)

        s_next = scores(0)
        for b in range(nb):
            s = s_next
            if b + 1 < nb:
                s_next = scores(b + 1)
            m_old = m_r[b][...]
            m_new = jnp.maximum(m_old, jnp.max(s, axis=-1, keepdims=True))
            alpha = jnp.exp(m_old - m_new)
            e = jnp.exp(s - m_new)
            l_r[b][...] = alpha * l_r[b][...] + jnp.sum(e, axis=-1, keepdims=True)
            m_r[b][...] = m_new
            a_r[b][...] = alpha * a_r[b][...] + _dot(e.astype(BF16), vv)
        return 0
    lax.fori_loop(0, (q0 + Q_BLOCK + tk - 1) // tk, chunk, 0)

    n_w = WINDOW + Q_BLOCK
    wpos = q0 - WINDOW + lax.broadcasted_iota(I32, (Q_BLOCK, n_w), 1)
    before_start = jnp.where(wpos >= 0, 0.0, NEG_INF)[None]
    w0 = pl.multiple_of(q0, Q_BLOCK)
    kw = kwb[pl.ds(w0, n_w), :]
    vw = vwb[pl.ds(w0, n_w), :]

    def w_scores(b):
        s = _nt(q_blk(b), kw).reshape(ph, Q_BLOCK, n_w) + wt_ref[b * ph:(b + 1) * ph] + before_start
        return s.reshape(brow, n_w)

    o_w = []
    s_next = w_scores(0)
    for b in range(nb):
        s = s_next
        if b + 1 < nb:
            s_next = w_scores(b + 1)
        e = jnp.exp(s - jnp.max(s, axis=-1, keepdims=True))
        den = jnp.maximum(jnp.sum(e, axis=-1, keepdims=True), 1e-20)
        o_w.append(_dot(e.astype(BF16), vw) / den)
    o_w = jnp.concatenate(o_w, axis=0)
    o_s = jnp.concatenate([a_r[b][...] / jnp.maximum(l_r[b][...], 1e-20) for b in range(nb)], axis=0)
    o_c = oc_s[...]

    gexp = _dot(gt_ref[...], eg_ref[...], precision=HI)
    head = lambda x, h: x[h * Q_BLOCK:(h + 1) * Q_BLOCK]
    for r in range(NSA_GROUP):
        pick = lambda x: jnp.where(upper, head(x, NSA_GROUP + r), head(x, r))
        o_ref[r] = _gate_mix(gexp, pick(o_c), pick(o_s), pick(o_w), r)


def _nsa_prompt(q4, gates, kc, vc, ks, vs, kw, vw, tabs, bsz, t):
    n_qb = t // Q_BLOCK
    n_ch = t // CMP_STRIDE
    tk = min(SEL_CHUNK, t)
    full = lambda a: pl.BlockSpec(a.shape, lambda b, i: (0,) * a.ndim)
    seq = pl.BlockSpec((t, KV_WIDTH), lambda b, i: (b, 0))
    cseq = pl.BlockSpec((n_ch, KV_WIDTH), lambda b, i: (b, 0))
    qspec = pl.BlockSpec((NSA_GROUP, Q_BLOCK, LANES), lambda b, i: (0, b * n_qb + i, 0))
    gr, bc, wt, mimp, esel, eg = tabs["gr"], tabs["bc"], tabs["wt"], tabs["mimp"], tabs["esel"], tabs["eg"]
    head_tile = pltpu.VMEM((NSA_HEADS * Q_BLOCK, KV_WIDTH), F32)
    nb, brow = NSA_HEADS // PIPE_HEADS, PIPE_HEADS * Q_BLOCK
    per_block = [pltpu.VMEM((brow, 1), F32)] * (2 * nb) + [pltpu.VMEM((brow, KV_WIDTH), F32)] * nb
    return pl.pallas_call(
        functools.partial(_nsa_prompt_kernel, voff_blk=tabs["voff"] // LANES),
        grid=(bsz, n_qb),
        in_specs=[qspec, pl.BlockSpec((Q_BLOCK, LG_PAD), lambda b, i: (b * n_qb + i, 0)),
                  cseq, cseq, seq, seq, seq, seq, full(gr),
                  pl.BlockSpec((1,) + bc.shape[1:], lambda b, i: (i, 0, 0, 0)), full(wt),
                  full(mimp), full(esel), full(eg)],
        out_specs=qspec,
        out_shape=jax.ShapeDtypeStruct((NSA_GROUP, bsz * t, LANES), F32),
        scratch_shapes=[pltpu.VMEM((t, KV_WIDTH), BF16), pltpu.VMEM((t, KV_WIDTH), BF16),
                        pltpu.VMEM((t + WINDOW, KV_WIDTH), BF16), pltpu.VMEM((t + WINDOW, KV_WIDTH), BF16),
                        pltpu.VMEM((n_ch, KV_WIDTH), BF16), pltpu.VMEM((n_ch, KV_WIDTH), BF16),
                        head_tile,
                        pltpu.VMEM((NSA_KV_HEADS, t // tk, Q_BLOCK, tk), F32),
                        pltpu.VMEM((NSA_HEADS * Q_BLOCK, KV_WIDTH), BF16)] + per_block,
        compiler_params=_params(("arbitrary", "arbitrary")),
        name="nsa_attention_prompt",
    )(q4, gates, kc, vc, ks, vs, kw, vw, gr, bc, wt, mimp, esel, eg)


def _nsa_sample_kernel(pt_ref, q_ref, gt_ref, kc_ref, vc_ref, ksn_ref, vsn_ref, kwn_ref, vwn_ref, bkw_ref, bvw_ref,
                       pks_hbm, pvs_hbm, gs_ref, gn_ref, bcs_ref, bws_ref, mimp_ref, esel_ref, eg_ref, o_ref,
                       bufk, bufv, sems, ksb, vsb, *, n_pages):
    sq = pl.program_id(0)
    n_seq = pl.num_programs(0)
    slot = sq % 2
    pools, bufs = (pks_hbm, pvs_hbm), (bufk, bufv)
    past = n_pages * PAGE_SIZE
    s_new = ksn_ref.shape[0]
    rows = NSA_GROUP * s_new
    n_cmp_rows = kc_ref.shape[0]
    n_sel = (past + s_new + SEL_BLOCK - 1) // SEL_BLOCK
    n_past_blk = past // SEL_BLOCK
    w_buf = bkw_ref.shape[1]

    @pl.when(sq == 0)
    def _():
        _paged_fetch(pt_ref, pools, bufs, sems, sq, slot, n_pages, True)

    @pl.when(sq + 1 < n_seq)
    def _():
        _paged_fetch(pt_ref, pools, bufs, sems, sq + 1, 1 - slot, n_pages, True)

    _paged_wait(bufs, sems, slot)
    ksb[...] = bufk[slot].astype(BF16)
    vsb[...] = bufv[slot].astype(BF16)

    lane = lax.broadcasted_iota(I32, (rows, LANES), 1)
    upper = lane >= HEAD_DIM
    qi = lax.broadcasted_iota(I32, (rows, 1), 0) % s_new
    pad_new = lambda ref: jnp.concatenate([ref[...], jnp.zeros((LANES - s_new, KV_WIDTH), F32)], axis=0).astype(BF16)
    ksn, vsn, kwn, vwn = pad_new(ksn_ref), pad_new(vsn_ref), pad_new(kwn_ref), pad_new(vwn_ref)
    kcb, vcb = kc_ref[...].astype(BF16), vc_ref[...].astype(BF16)
    bkw, bvw = bkw_ref[...].astype(BF16), bvw_ref[...].astype(BF16)
    new_causal = (lane < s_new) & (lane <= qi)

    def stacked_q(g):
        keep = upper if g == 1 else jnp.logical_not(upper)
        qs = jnp.concatenate([q_ref[r] for r in range(NSA_GROUP)], axis=0)
        return jnp.where(keep, qs, 0.0).astype(BF16)

    cend = lax.broadcasted_iota(I32, (rows, n_cmp_rows), 1) * CMP_STRIDE + (CMP_BLOCK - 1)
    valid_c = (past + qi) >= cend
    o_c, psums = [], []
    for g in range(NSA_KV_HEADS):
        s = _nt(stacked_q(g), kcb) + bcs_ref[g]
        e, den = _masked_softmax_parts(s, valid_c)
        p = e / den
        o_c.append(_dot(p.astype(BF16), vcb))
        ps = p[0:s_new]
        for r in range(1, NSA_GROUP):
            ps = ps + p[r * s_new:(r + 1) * s_new]
        psums.append(ps)
    imp = _dot(jnp.concatenate(psums, axis=0), mimp_ref[...], precision=HI)
    qpos_sel = past + lax.broadcasted_iota(I32, (NSA_KV_HEADS * s_new, 1), 0) % s_new
    sel = _select_blocks(_block_scores(imp, qpos_sel, n_sel), n_sel)
    mask_past = _dot(sel[:, :n_past_blk].astype(BF16), esel_ref[...])
    mask_add = jnp.where(mask_past > 0.5, 0.0, NEG_INF)
    sel_new = sel[:, n_past_blk:n_past_blk + 1]

    gexp = _dot(gt_ref[...], eg_ref[...], precision=HI)
    o_s, o_w = [], []
    for g in range(NSA_KV_HEADS):
        qg = stacked_q(g)
        tile_rows = lambda a: jnp.concatenate([a[g * s_new:(g + 1) * s_new]] * NSA_GROUP, axis=0)
        s_p = _dot(qg, ksb[...]) + gs_ref[g] + tile_rows(mask_add)
        valid_n = new_causal & (tile_rows(sel_new) > 0.5)
        s_n = jnp.where(valid_n, _nt(qg, ksn) + gn_ref[g], NEG_INF)
        m = jnp.maximum(jnp.max(s_p, axis=-1, keepdims=True), jnp.max(s_n, axis=-1, keepdims=True))
        e_p = jnp.exp(s_p - m)
        e_n = jnp.where(valid_n, jnp.exp(s_n - m), 0.0)
        den = jnp.maximum(jnp.sum(e_p, axis=-1, keepdims=True) + jnp.sum(e_n, axis=-1, keepdims=True), 1e-20)
        o_s.append((_nt(e_p.astype(BF16), vsb[...]) + _dot(e_n.astype(BF16), vsn)) / den)
        jb = lax.broadcasted_iota(I32, (rows, w_buf), 1)
        dist_b = w_buf + qi - jb
        valid_b = (dist_b < WINDOW) & (past - w_buf + jb >= 0)
        s_b = jnp.where(valid_b, _dot(qg, bkw) + bws_ref[g], NEG_INF)
        s_n = jnp.where(new_causal, _nt(qg, kwn) + gn_ref[g], NEG_INF)
        m = jnp.maximum(jnp.max(s_b, axis=-1, keepdims=True), jnp.max(s_n, axis=-1, keepdims=True))
        e_b = jnp.where(valid_b, jnp.exp(s_b - m), 0.0)
        e_n = jnp.where(new_causal, jnp.exp(s_n - m), 0.0)
        den = jnp.maximum(jnp.sum(e_b, axis=-1, keepdims=True) + jnp.sum(e_n, axis=-1, keepdims=True), 1e-20)
        o_w.append((_nt(e_b.astype(BF16), bvw) + _dot(e_n.astype(BF16), vwn)) / den)

    up8 = upper[0:s_new]
    for r in range(NSA_GROUP):
        pick = lambda o: jnp.where(up8, o[1][r * s_new:(r + 1) * s_new], o[0][r * s_new:(r + 1) * s_new])
        o_ref[r] = _gate_mix(gexp, pick(o_c), pick(o_s), pick(o_w), r)


def _nsa_sample(pt_flat, q4, gates, kc, vc, ksn, vsn, kwn, vwn, buf_kw, buf_vw, pool_ks, pool_vs, tabs,
                n_seq, s_new, n_pages):
    past = n_pages * PAGE_SIZE
    n_ch = past // CMP_STRIDE
    w_buf = buf_kw.shape[1]
    full = lambda a: pl.BlockSpec(a.shape, lambda s, pt: (0,) * a.ndim)
    hbm = pl.BlockSpec(memory_space=pl.ANY)
    rows = lambda n, w: pl.BlockSpec((n, w), lambda s, pt: (s, 0))
    qspec = pl.BlockSpec((NSA_GROUP, s_new, LANES), lambda s, pt: (0, s, 0))
    consts = (tabs["gs"], tabs["gn"], tabs["bcs"], tabs["bws"], tabs["mimp_s"], tabs["esel_s"], tabs["eg"])
    return pl.pallas_call(
        functools.partial(_nsa_sample_kernel, n_pages=n_pages),
        grid_spec=pltpu.PrefetchScalarGridSpec(
            num_scalar_prefetch=1,
            grid=(n_seq,),
            in_specs=[qspec, rows(s_new, LG_PAD), rows(n_ch, KV_WIDTH), rows(n_ch, KV_WIDTH)]
                     + [rows(s_new, KV_WIDTH)] * 4 + [rows(KV_WIDTH, w_buf)] * 2 + [hbm, hbm]
                     + [full(c) for c in consts],
            out_specs=qspec,
            scratch_shapes=[pltpu.VMEM((2, KV_WIDTH, past), F32), pltpu.VMEM((2, KV_WIDTH, past), F32),
                            pltpu.SemaphoreType.DMA((2, 2)),
                            pltpu.VMEM((KV_WIDTH, past), BF16), pltpu.VMEM((KV_WIDTH, past), BF16)],
        ),
        out_shape=jax.ShapeDtypeStruct((NSA_GROUP, n_seq * s_new, LANES), F32),
        compiler_params=_params(("arbitrary",)),
        name="nsa_attention_sample",
    )(pt_flat, q4, gates, kc, vc, ksn, vsn, kwn, vwn, buf_kw, buf_vw, pool_ks, pool_vs, *consts)


def _gla_kernel(q_ref, k_ref, v_ref, lg_ref, r_ref, gn_ref, s0_ref, o_ref, sfin_ref, s_scr, *, cl):
    c = pl.program_id(1)
    n_c = pl.num_programs(1)
    n_sub = q_ref.shape[0] // cl

    @pl.when(c == 0)
    def _():
        s_scr[...] = s0_ref[...]

    lane_t = lax.broadcasted_iota(I32, (cl, LANES), 1)
    row_a = lax.broadcasted_iota(I32, (cl, GLA_K_WIDTH), 0)
    causal = lax.broadcasted_iota(I32, (cl, cl), 1) <= lax.broadcasted_iota(I32, (cl, cl), 0)
    row_s = lax.broadcasted_iota(I32, (LANES, GLA_DV), 0)
    pair = LANES // GLA_DK
    n_pairs = GLA_HEADS // pair
    cols = lambda x, p: x[:, p * LANES:(p + 1) * LANES]
    vcols = lambda h: slice(h * GLA_DV, (h + 1) * GLA_DV)

    pre = []
    for ci in range(n_sub):
        rows = slice(ci * cl, (ci + 1) * cl)
        b = lg_ref[rows, :]
        sh = 1
        while sh < cl:
            b = b + jnp.where(row_a >= sh, pltpu.roll(b, sh, 0), 0.0)
            sh *= 2
        b_last = b[cl - 1:cl, :]
        qt = q_ref[rows, :] * jnp.exp(b)
        kp = k_ref[rows, :]
        kt = (kp * jnp.exp(-b)).astype(BF16)
        khat = (kp * jnp.exp(b_last - b)).astype(BF16)
        qm, intra, u = [], [], []
        for h in range(GLA_HEADS):
            vh = v_ref[rows, vcols(h)].astype(BF16)
            mine = (lane_t >= GLA_DK) if h % pair == 1 else (lane_t < GLA_DK)
            qm.append(jnp.where(mine, cols(qt, h // pair), 0.0).astype(BF16))
            att = jnp.where(causal, _nt(qm[h], cols(kt, h // pair)), 0.0).astype(BF16)
            intra.append(_dot(att, vh))
            u.append(_tn(cols(khat, h // pair), vh))
        dec = [jnp.exp(jnp.broadcast_to(cols(b_last, p), (LANES, LANES))).T for p in range(n_pairs)]
        upd = [jnp.where(row_s >= GLA_DK, u[p * pair + 1], u[p * pair]) for p in range(n_pairs)]
        pre.append((rows, qm, intra, dec, upd))

    state = [s_scr[p * LANES:(p + 1) * LANES, :] for p in range(n_pairs)]
    for rows, qm, intra, dec, upd in pre:
        s_bf = [x.astype(BF16) for x in state]
        for h in range(GLA_HEADS):
            o = _dot(qm[h], s_bf[h // pair]) + intra[h]
            o = o * lax.rsqrt(jnp.mean(o * o, axis=-1, keepdims=True) + RMS_EPS) * gn_ref[...]
            rh = r_ref[rows, vcols(h)]
            o_ref[rows, vcols(h)] = o * (rh * jax.nn.sigmoid(rh))
        state = [state[p] * dec[p] + upd[p] for p in range(n_pairs)]
    for p in range(n_pairs):
        s_scr[p * LANES:(p + 1) * LANES, :] = state[p]

    @pl.when(c == n_c - 1)
    def _():
        sfin_ref[...] = s_scr[...]


def _gla(q_l, k_l, v_l, lg, r, g_norm, s0, bsz, n_c, cl):
    srows = GLA_HEADS * GLA_DK
    per_step = math.gcd(n_c, GLA_CHUNKS_PER_STEP)
    n_steps = n_c // per_step
    blk = lambda w: pl.BlockSpec((per_step * cl, w), lambda b, c: (b * n_steps + c, 0))
    st = pl.BlockSpec((srows, GLA_DV), lambda b, c: (b, 0))
    return pl.pallas_call(
        functools.partial(_gla_kernel, cl=cl),
        grid=(bsz, n_steps),
        in_specs=[blk(GLA_K_WIDTH), blk(GLA_K_WIDTH), blk(GLA_V_WIDTH), blk(GLA_K_WIDTH), blk(GLA_V_WIDTH),
                  pl.BlockSpec(g_norm.shape, lambda b, c: (0, 0)), st],
        out_specs=[blk(GLA_V_WIDTH), st],
        out_shape=[jax.ShapeDtypeStruct((bsz * n_c * cl, GLA_V_WIDTH), F32),
                   jax.ShapeDtypeStruct((bsz * srows, GLA_DV), F32)],
        scratch_shapes=[pltpu.VMEM((srows, GLA_DV), F32)],
        compiler_params=_params(("arbitrary", "arbitrary")),
        name="gla",
    )(q_l, k_l, v_l, lg, r, g_norm, s0)


def _merge_kernel(x_ref, on_ref, og_ref, ma_ref, mb_ref, wn_ref, wg_ref, wo_ref, gf_ref, rw_ref, rb_ref,
                  x1_o, h2t_o, ei_o, gw_o, rk_o, cnt_o, carry):
    i = pl.program_id(0)
    tm = x_ref.shape[0]

    @pl.when(i == 0)
    def _():
        carry[...] = jnp.zeros(carry.shape, F32)

    on = jnp.concatenate([on_ref[r] for r in range(NSA_GROUP)], axis=1).astype(BF16)
    ya = _dot(on, wn_ref[...])
    yb = _dot(og_ref[...].astype(BF16), wg_ref[...])
    m = jax.nn.sigmoid(ma_ref[...]) * ya + jax.nn.sigmoid(mb_ref[...]) * yb
    x1 = x_ref[...] + _dot(m.astype(BF16), wo_ref[...])
    x1_o[...] = x1
    h2 = x1 * lax.rsqrt(jnp.mean(x1 * x1, axis=-1, keepdims=True) + RMS_EPS) * gf_ref[...]
    for s in range(D_MODEL // LANES):
        h2t_o[pl.ds(s, tm, stride=SUBLANES), :] = h2[:, s * LANES:(s + 1) * LANES]

    logits = _dot(h2, rw_ref[...], precision=HI) + rb_ref[...]
    lane = lax.broadcasted_iota(I32, (tm, LANES), 1)
    lane_f = lane.astype(F32)
    work = logits
    vals, idxs = [], []
    for _ in range(TOP_K):
        mk = jnp.max(work, axis=-1, keepdims=True)
        ik = jnp.min(jnp.where(work == mk, lane_f, float(LANES)), axis=-1, keepdims=True)
        vals.append(mk)
        idxs.append(ik)
        work = jnp.where(lane_f == ik, -jnp.inf, work)
    es = [jnp.exp(v - vals[0]) for v in vals]
    den = es[0]
    for e in es[1:]:
        den = den + e
    onehot = jnp.zeros((tm, LANES), F32)
    for ik in idxs:
        onehot = onehot + jnp.where(lane_f == ik, 1.0, 0.0)
    below = lax.broadcasted_iota(I32, (tm, tm), 1) < lax.broadcasted_iota(I32, (tm, tm), 0)
    before = _dot(jnp.where(below, 1.0, 0.0).astype(BF16), onehot.astype(BF16)) + carry[0:1, :]
    ei = jnp.zeros((tm, LANES), F32)
    gw = jnp.zeros((tm, LANES), F32)
    rk = jnp.zeros((tm, LANES), F32)
    for k in range(TOP_K):
        rank_k = jnp.sum(jnp.where(lane_f == idxs[k], before, 0.0), axis=-1, keepdims=True)
        ei = jnp.where(lane == k, idxs[k], ei)
        gw = jnp.where(lane == k, es[k] / den, gw)
        rk = jnp.where(lane == k, rank_k, rk)
    ei_o[...] = ei.astype(I32)
    gw_o[...] = gw
    rk_o[...] = rk.astype(I32)
    carry[0:1, :] = carry[0:1, :] + jnp.sum(onehot, axis=0, keepdims=True)
    cnt_o[...] = carry[...]


def _merge(x, o_nsa4, o_gla, m_a, m_b, mw):
    n = x.shape[0]
    tm = min(TM_MERGE, n)
    row = lambda w: pl.BlockSpec((tm, w), lambda i: (i, 0))
    full = lambda a: pl.BlockSpec(a.shape, lambda i: (0,) * a.ndim)
    ws = (mw["wn"], mw["wg"], mw["wo"], mw["gf"], mw["rw"], mw["rb"])
    return pl.pallas_call(
        _merge_kernel,
        grid=(n // tm,),
        in_specs=[row(D_MODEL), pl.BlockSpec((NSA_GROUP, tm, LANES), lambda i: (0, i, 0)), row(GLA_V_WIDTH),
                  row(D_MODEL), row(D_MODEL)] + [full(w) for w in ws],
        out_specs=[row(D_MODEL), pl.BlockSpec((tm * SUBLANES, LANES), lambda i: (i, 0)),
                   row(LANES), row(LANES), row(LANES), pl.BlockSpec((SUBLANES, LANES), lambda i: (0, 0))],
        out_shape=[jax.ShapeDtypeStruct((n, D_MODEL), F32), jax.ShapeDtypeStruct((n * SUBLANES, LANES), F32),
                   jax.ShapeDtypeStruct((n, LANES), I32), jax.ShapeDtypeStruct((n, LANES), F32),
                   jax.ShapeDtypeStruct((n, LANES), I32), jax.ShapeDtypeStruct((SUBLANES, LANES), F32)],
        scratch_shapes=[pltpu.VMEM((SUBLANES, LANES), F32)],
        compiler_params=_params(("arbitrary",)),
        name="merge_router",
    )(x, o_nsa4, o_gla, m_a, m_b, *ws)


def _token_tile(ref, row):
    return ref.at[pl.ds(row * SUBLANES, SUBLANES)]


def _dispatch_kernel(meta_ref, dest_ref, h_ref, xs_hbm, dsm, ztile, sem_idx, sem_row, sem_pad, *, td):
    i = pl.program_id(0)
    n_steps = pl.num_programs(0)
    idx_copy = pltpu.make_async_copy(dest_ref, dsm, sem_idx)
    idx_copy.start()
    idx_copy.wait()

    per_row = LANES // TOP_K

    def body(rr, _):
        for c in range(LANES):
            tk = rr * per_row + c // TOP_K
            pltpu.make_async_copy(_token_tile(h_ref, tk), _token_tile(xs_hbm, dsm[rr, c]), sem_row).start()
        return 0
    lax.fori_loop(0, td // per_row, body, 0)

    @pl.when(i == n_steps - 1)
    def _():
        ztile[...] = jnp.zeros(ztile.shape, F32)
        blk_rows = ztile.shape[0]
        n_blocks = xs_hbm.shape[0] // blk_rows

        def pads(start_or_wait):
            def per_expert(e, _):
                first = meta_ref[N_EXPERTS + e] + meta_ref[e]
                last = meta_ref[N_EXPERTS + e] + meta_ref[2 * N_EXPERTS + e]

                def per_row(rw, _):
                    cp = pltpu.make_async_copy(ztile.at[pl.ds(0, SUBLANES)], _token_tile(xs_hbm, rw), sem_pad)
                    cp.start() if start_or_wait else cp.wait()
                    return 0
                lax.fori_loop(first, last, per_row, 0)
                return 0
            lax.fori_loop(0, N_EXPERTS, per_expert, 0)

            def per_block(bk, _):
                cp = pltpu.make_async_copy(ztile, xs_hbm.at[pl.ds(bk * blk_rows, blk_rows)], sem_pad)
                cp.start() if start_or_wait else cp.wait()
                return 0
            lax.fori_loop(meta_ref[3 * N_EXPERTS], n_blocks, per_block, 0)
        pads(True)
        pads(False)

    for _ in range(TOP_K):
        pltpu.make_async_copy(h_ref, xs_hbm.at[pl.ds(0, td * SUBLANES)], sem_row).wait()


def _dispatch(meta, dest2d, h2t, n_blocks, mrows, td):
    n = h2t.shape[0] // SUBLANES
    n_rows_total = n_blocks * mrows
    hbm = pl.BlockSpec(memory_space=pl.ANY)
    drows = td * TOP_K // LANES
    return pl.pallas_call(
        functools.partial(_dispatch_kernel, td=td),
        grid_spec=pltpu.PrefetchScalarGridSpec(
            num_scalar_prefetch=1,
            grid=(n // td,),
            in_specs=[pl.BlockSpec((drows, LANES), lambda i, m: (i, 0)),
                      pl.BlockSpec((td * SUBLANES, LANES), lambda i, m: (i, 0))],
            out_specs=hbm,
            scratch_shapes=[pltpu.SMEM((drows, LANES), I32), pltpu.VMEM((mrows * SUBLANES, LANES), F32),
                            pltpu.SemaphoreType.DMA, pltpu.SemaphoreType.DMA, pltpu.SemaphoreType.DMA],
        ),
        out_shape=jax.ShapeDtypeStruct((n_rows_total * SUBLANES, LANES), F32),
        compiler_params=_params(("arbitrary",)),
        name="moe_dispatch",
    )(meta, dest2d, h2t)


def _moe_kernel(be_ref, nu_ref, xs_ref, wg_ref, bg_ref, wu_ref, bu_ref, wd_ref, bd_ref, y_ref, wg_s, wu_s, wd_s):
    i = pl.program_id(0)
    rows = xs_ref.shape[0] // SUBLANES
    n_s = D_MODEL // LANES

    @pl.when(i < nu_ref[0])
    def _():
        @pl.when((i == 0) | (be_ref[i] != be_ref[jnp.maximum(i - 1, 0)]))
        def _():
            wg_s[...] = wg_ref[0].astype(BF16)
            wu_s[...] = wu_ref[0].astype(BF16)
            wd_s[...] = wd_ref[0].astype(BF16)

        xb = jnp.concatenate([xs_ref[pl.ds(s, rows, stride=SUBLANES), :] for s in range(n_s)], axis=1).astype(BF16)
        g = _dot(xb, wg_s[...]) + bg_ref[0]
        u = _dot(xb, wu_s[...]) + bu_ref[0]
        g = jnp.minimum(g, SWIGLU_LIMIT)
        u = jnp.clip(u, -SWIGLU_LIMIT, SWIGLU_LIMIT)
        hh = (u + 1.0) * (g * jax.nn.sigmoid(SWIGLU_ALPHA * g))
        y = _dot(hh.astype(BF16), wd_s[...]) + bd_ref[0]
        for s in range(n_s):
            y_ref[pl.ds(s, rows, stride=SUBLANES), :] = y[:, s * LANES:(s + 1) * LANES]

    @pl.when(i >= nu_ref[0])
    def _():
        y_ref[...] = jnp.zeros(y_ref.shape, F32)


def _moe_experts(block_e, n_used, xs, ew, n_blocks, mrows):
    blk = lambda i, be, nu: jnp.minimum(i, nu[0] - 1)
    rows = pl.BlockSpec((mrows * SUBLANES, LANES), lambda i, be, nu: (i, 0))
    wspec = lambda a: pl.BlockSpec((1,) + a.shape[1:], lambda i, be, nu: (be[blk(i, be, nu)], 0, 0))
    ws = (ew["wg"], ew["bg"], ew["wu"], ew["bu"], ew["wd"], ew["bd"])
    return pl.pallas_call(
        _moe_kernel,
        grid_spec=pltpu.PrefetchScalarGridSpec(
            num_scalar_prefetch=2,
            grid=(n_blocks,),
            in_specs=[rows] + [wspec(w) for w in ws],
            out_specs=rows,
            scratch_shapes=[pltpu.VMEM((D_MODEL, D_FF), BF16), pltpu.VMEM((D_MODEL, D_FF), BF16),
                            pltpu.VMEM((D_FF, D_MODEL), BF16)],
        ),
        out_shape=jax.ShapeDtypeStruct(xs.shape, F32),
        compiler_params=_params(("arbitrary",)),
        name="moe_experts",
    )(block_e, n_used, xs, *ws)


def _combine_kernel(dest_ref, gw_ref, x1_ref, gfin_ref, y_hbm, out_ref, dsm, buf, sem_idx, sem_row):
    tc = x1_ref.shape[0]
    idx_copy = pltpu.make_async_copy(dest_ref, dsm, sem_idx)
    idx_copy.start()
    idx_copy.wait()

    per_row = LANES // TOP_K

    def body(rr, _):
        for c in range(LANES):
            tk = rr * per_row + c // TOP_K
            pltpu.make_async_copy(_token_tile(y_hbm, dsm[rr, c]), _token_tile(buf.at[c % TOP_K], tk), sem_row).start()
        return 0
    lax.fori_loop(0, tc // per_row, body, 0)
    for k in range(TOP_K):
        pltpu.make_async_copy(y_hbm.at[pl.ds(0, tc * SUBLANES)], buf.at[k], sem_row).wait()

    gw = gw_ref[...]
    parts = []
    for s in range(D_MODEL // LANES):
        acc = None
        for k in range(TOP_K):
            term = buf[k, pl.ds(s, tc, stride=SUBLANES), :] * gw[:, k:k + 1]
            acc = term if acc is None else acc + term
        parts.append(acc)
    x2 = x1_ref[...] + jnp.concatenate(parts, axis=1)
    out_ref[...] = x2 * lax.rsqrt(jnp.mean(x2 * x2, axis=-1, keepdims=True) + RMS_EPS) * gfin_ref[...]


def _combine(dest2d, gw, x1, g_final, y_rows):
    n = x1.shape[0]
    tc = min(TC_COMBINE, n)
    drows = tc * TOP_K // LANES
    row = lambda w: pl.BlockSpec((tc, w), lambda i: (i, 0))
    return pl.pallas_call(
        _combine_kernel,
        grid=(n // tc,),
        in_specs=[pl.BlockSpec((drows, LANES), lambda i: (i, 0)), row(LANES), row(D_MODEL),
                  pl.BlockSpec(g_final.shape, lambda i: (0, 0)), pl.BlockSpec(memory_space=pl.ANY)],
        out_specs=row(D_MODEL),
        out_shape=jax.ShapeDtypeStruct((n, D_MODEL), F32),
        scratch_shapes=[pltpu.SMEM((drows, LANES), I32), pltpu.VMEM((TOP_K, tc * SUBLANES, LANES), F32),
                        pltpu.SemaphoreType.DMA, pltpu.SemaphoreType.DMA],
        compiler_params=_params(("arbitrary",)),
        name="moe_combine",
    )(dest2d, gw, x1, g_final, y_rows)


def _bucket_table(max_dist):
    n = np.arange(max_dist, dtype=np.int64)
    scaled = np.log(np.maximum(n, 1).astype(np.float64) / REL_EXACT) / math.log(REL_MAX_DIST / REL_EXACT)
    large = REL_EXACT + (scaled * (REL_BUCKETS - REL_EXACT)).astype(np.int64)
    return np.where(n < REL_EXACT, n, np.minimum(large, REL_BUCKETS - 1)).astype(np.int32)


def _bias_lookup(rel_bias, dist):
    d = np.maximum(dist, 0)
    buckets = _bucket_table(int(d.max()) + 1)[d]
    return jnp.take(rel_bias.astype(F32).T, jnp.asarray(buckets), axis=1)


def _skew(w, n_rows, step, width):
    h, l = w.shape
    flat = jnp.tile(w, (1, n_rows))[:, :n_rows * (l - step)]
    return flat.reshape(h, n_rows, l - step)[:, :, :width]


def _importance_matrix(n_rows, n_cmp, n_sel, n_cols):
    rc = CMP_BLOCK // CMP_STRIDE
    rs = SEL_BLOCK // CMP_STRIDE
    m = np.zeros((n_rows, n_cols), np.float32)
    for j in range(n_sel):
        for o in range(rs + rc - 1):
            w = min(o - (rc - 1) + rc, rs) - max(o - (rc - 1), 0)
            c = rs * j + o - (rc - 1)
            if 0 <= c < n_cmp:
                m[c, j] += w
    return jnp.asarray(m)


def _block_expand(n_blocks, n_rows=LANES):
    e = np.zeros((n_rows, n_blocks * SEL_BLOCK), np.float32)
    for j in range(n_blocks):
        e[j, j * SEL_BLOCK:(j + 1) * SEL_BLOCK] = 1.0
    return jnp.asarray(e, dtype=BF16)


def _gate_expand():
    e = np.zeros((LG_PAD, 3 * NSA_WIDTH), np.float32)
    for g in range(NSA_KV_HEADS):
        for r in range(NSA_GROUP):
            for c in range(3):
                lo = c * NSA_WIDTH + r * LANES + g * HEAD_DIM
                e[(g * NSA_GROUP + r) * 3 + c, lo:lo + HEAD_DIM] = 1.0
    return jnp.asarray(e)


def _prompt_tables(rel_bias, t):
    n_qb = t // Q_BLOCK
    n_ch = t // CMP_STRIDE
    tk = min(SEL_CHUNK, t)
    voff = max(t - Q_BLOCK, WINDOW)
    nv = voff // LANES + tk // LANES
    wd = nv * LANES
    wv = _bias_lookup(rel_bias, np.concatenate([voff - np.arange(wd), voff + np.arange(Q_BLOCK, 0, -1)]))
    gr = _skew(wv, Q_BLOCK, 1, wd).reshape(NSA_HEADS, Q_BLOCK, nv, LANES).transpose(2, 0, 1, 3)
    wc = _bias_lookup(rel_bias, np.concatenate([np.arange(t) - (CMP_BLOCK - 1), np.zeros(CMP_STRIDE * n_ch, np.int64)]))
    cend = np.arange(n_ch) * CMP_STRIDE + (CMP_BLOCK - 1)
    cmask = np.where(np.arange(t)[:, None] >= cend[None, :], 0.0, NEG_INF).astype(np.float32)
    bc = _skew(wc, n_ch, CMP_STRIDE, t).transpose(0, 2, 1) + cmask[None]
    bc = bc.reshape(NSA_HEADS, n_qb, Q_BLOCK, n_ch).transpose(1, 0, 2, 3)
    n_w = WINDOW + Q_BLOCK
    ww = _bias_lookup(rel_bias, np.concatenate([WINDOW - np.arange(n_w), WINDOW + np.arange(Q_BLOCK, 0, -1)]))
    dist_w = WINDOW + np.arange(Q_BLOCK)[:, None] - np.arange(n_w)[None, :]
    wmask = np.where((dist_w >= 0) & (dist_w < WINDOW), 0.0, NEG_INF).astype(np.float32)
    wt = _skew(ww, Q_BLOCK, 1, n_w) + wmask[None]
    return dict(gr=gr, bc=bc, wt=wt, voff=voff,
                mimp=_importance_matrix(n_ch, n_ch - 1, t // SEL_BLOCK, LANES),
                esel=_block_expand(t // SEL_BLOCK), eg=_gate_expand())


def _sample_tables(rel_bias, past, s_new, w_buf):
    n_ch = past // CMP_STRIDE
    n_sel = (past + s_new + SEL_BLOCK - 1) // SEL_BLOCK
    per_group = lambda a: a.reshape(NSA_KV_HEADS, NSA_GROUP, s_new, -1).reshape(NSA_KV_HEADS, NSA_GROUP * s_new, -1)
    qi = np.arange(s_new)

    def table(dist):
        return per_group(_bias_lookup(rel_bias, dist))
    ws = _bias_lookup(rel_bias, np.concatenate([past - np.arange(past), past + np.arange(s_new, 0, -1)]))
    gs = per_group(_skew(ws, s_new, 1, past))
    jn = np.arange(LANES)
    gn = table(np.where(jn[None, :] < s_new, qi[:, None] - jn[None, :], 0))
    cend = np.arange(n_ch) * CMP_STRIDE + (CMP_BLOCK - 1)
    bcs = table(past + qi[:, None] - cend[None, :])
    bws = table(w_buf + qi[:, None] - np.arange(w_buf)[None, :])
    sel_lanes = -(-n_sel // LANES) * LANES
    return dict(gs=gs, gn=gn, bcs=bcs, bws=bws,
                mimp_s=_importance_matrix(n_ch, n_ch - 1, n_sel, sel_lanes),
                esel_s=_block_expand(past // SEL_BLOCK, past // SEL_BLOCK), eg=_gate_expand())


def _compress_weights(pe, w1, w2):
    rc = CMP_BLOCK // CMP_STRIDE
    w1r = w1.reshape(rc, CMP_STRIDE, HEAD_DIM, CMP_HIDDEN)
    eye = jnp.eye(NSA_KV_HEADS, dtype=w1.dtype)
    wcat = jnp.einsum("rldh,ge->lgdreh", w1r, eye).reshape(CMP_STRIDE // 2, 2 * KV_WIDTH,
                                                           rc * NSA_KV_HEADS * CMP_HIDDEN)
    w2bd = jnp.einsum("hd,ge->ghed", w2, eye).reshape(NSA_KV_HEADS * CMP_HIDDEN, KV_WIDTH)
    pe_rows = jnp.concatenate([pe.reshape(1, CMP_BLOCK * HEAD_DIM),
                               jnp.zeros((SUBLANES - 1, CMP_BLOCK * HEAD_DIM), pe.dtype)], axis=0)
    return wcat.astype(BF16), pe_rows, w1.reshape(CMP_BLOCK * HEAD_DIM, CMP_HIDDEN), w2bd.astype(BF16)


def _layer_weights(w_in, gla_w_alpha, gla_b_alpha, w_branch_nsa, norm_ffn, router_w, router_b):
    offs = np.cumsum((0,) + IN_SPLITS)
    col = lambda j: w_in[:, offs[j]:offs[j + 1]]
    q_perm = col(0).reshape(D_MODEL, NSA_KV_HEADS, NSA_GROUP, HEAD_DIM).transpose(0, 2, 1, 3).reshape(D_MODEL, NSA_WIDTH)
    pad = jnp.zeros((D_MODEL, LG_PAD - 3 * NSA_HEADS - GLA_RANK), w_in.dtype)
    w_p = jnp.concatenate([q_perm, col(1), col(3), col(4), col(5), col(7), col(8), col(9), col(2), col(6), pad],
                          axis=1).astype(BF16)
    w_al = jnp.zeros((LG_PAD, GLA_K_WIDTH), F32).at[3 * NSA_HEADS:3 * NSA_HEADS + GLA_RANK].set(gla_w_alpha)
    wn = w_branch_nsa.reshape(NSA_KV_HEADS, NSA_GROUP, HEAD_DIM, D_MODEL).transpose(1, 0, 2, 3).reshape(NSA_WIDTH, D_MODEL)
    rw = jnp.concatenate([router_w, jnp.zeros((D_MODEL, LANES - N_EXPERTS), F32)], axis=1)
    rb = jnp.concatenate([router_b, jnp.full((LANES - N_EXPERTS,), NEG_INF, F32)]).reshape(1, LANES)
    return w_p, w_al, gla_b_alpha.reshape(1, GLA_K_WIDTH), wn.astype(BF16), norm_ffn.reshape(1, D_MODEL), rw, rb


def _moe(h2t, ei, gw, rk, counts, x1, g_final, ew):
    n = x1.shape[0]
    nk = n * TOP_K
    counts = counts[0, :N_EXPERTS].astype(I32)
    mrows = MOE_ROWS if nk // N_EXPERTS >= 4 * MOE_ROWS else MOE_ROWS_SMALL
    padded = (counts + mrows - 1) // mrows * mrows
    pends = jnp.cumsum(padded)
    pstarts = pends - padded
    n_blocks = (nk + N_EXPERTS * (mrows - 1) + mrows - 1) // mrows
    blk_start = jnp.arange(n_blocks, dtype=I32) * mrows
    block_e = jnp.minimum(jnp.sum((pends[None, :] <= blk_start[:, None]).astype(I32), axis=1), N_EXPERTS - 1)
    n_used = (pends[-1] // mrows).astype(I32).reshape(1)
    e_sel = ei[:, :TOP_K, None] == jnp.arange(N_EXPERTS, dtype=I32)
    dest = jnp.sum(jnp.where(e_sel, pstarts.astype(I32), 0), axis=-1) + rk[:, :TOP_K]
    dest2d = dest.reshape(nk // LANES, LANES)
    meta = jnp.concatenate([counts, pstarts, padded, n_used]).astype(I32)
    td = min(TD_DISPATCH, n)
    xs = _dispatch(meta, dest2d, h2t, n_blocks, mrows, td)
    y_rows = _moe_experts(block_e, n_used, xs, ew, n_blocks, mrows)
    return _combine(dest2d, gw, x1, g_final, y_rows)


def kernel(x_prompt, x_sample, cache_cmp_k, cache_cmp_v, cache_sel_k, cache_sel_v, state_win_k, state_win_v, state_gla, page_table, rel_bias, norm_mix, w_in, nsa_pe_k, nsa_pe_v, nsa_w1_k, nsa_w1_v, nsa_w2_k, nsa_w2_v, gla_w_alpha, gla_b_alpha, gla_norm, w_branch_nsa, w_branch_gla, w_out, norm_ffn, router_w, router_b, exp_w_gate, exp_b_gate, exp_w_up, exp_b_up, exp_w_down, exp_b_down, norm_final):
    depth = w_in.shape[0]
    assert depth == 1, "single-layer trunk"
    bsz, t, d = x_prompt.shape
    n_seq, s_new, _ = x_sample.shape
    n_pages = page_table.shape[1]
    past = n_pages * PAGE_SIZE
    w_buf = state_win_k.shape[2]
    assert d == D_MODEL and t % Q_BLOCK == 0 and t % GLA_CHUNK == 0

    w_p, w_al, b_al, wn, gf, rw, rb = _layer_weights(w_in[0], gla_w_alpha[0], gla_b_alpha[0], w_branch_nsa[0],
                                                     norm_ffn[0], router_w[0], router_b[0])
    g_mix = norm_mix[0].reshape(1, D_MODEL)
    g_fin = norm_final.reshape(1, D_MODEL)
    g_gla = gla_norm[0].reshape(1, GLA_DV)
    mw = dict(wn=wn, wg=w_branch_gla[0].astype(BF16), wo=w_out[0].astype(BF16), gf=gf, rw=rw, rb=rb)
    ew = dict(wg=exp_w_gate[0], bg=exp_b_gate[0].reshape(N_EXPERTS, 1, D_FF),
              wu=exp_w_up[0], bu=exp_b_up[0].reshape(N_EXPERTS, 1, D_FF),
              wd=exp_w_down[0], bd=exp_b_down[0].reshape(N_EXPERTS, 1, D_MODEL))
    cw = {}
    for nm, pe, w1, w2 in (("k", nsa_pe_k[0], nsa_w1_k[0], nsa_w2_k[0]), ("v", nsa_pe_v[0], nsa_w1_v[0], nsa_w2_v[0])):
        cw["wcat_" + nm], cw["pe_" + nm], cw["w1f_" + nm], cw["w2_" + nm] = _compress_weights(pe, w1, w2)

    xp = x_prompt.reshape(bsz * t, d)
    (q4, kck, kcv, ksk, ksv, kwk, kwv, gates, q_l, k_l, v_l, lg, r_l, m_a, m_b, *kv_t) = _in_projection(
        xp, g_mix, w_p, w_al, b_al, seq_len=t)
    kc, vc = _compress_prompt(kck, kcv, cw, bsz, t)
    o_nsa = _nsa_prompt(q4, gates, kc, vc, ksk, ksv, kwk, kwv, _prompt_tables(rel_bias, t), bsz, t)
    s_zero = jnp.zeros((bsz * GLA_HEADS * GLA_DK, GLA_DV), F32)
    o_gla, p_gla = _gla(q_l, k_l, v_l, lg, r_l, g_gla, s_zero, bsz, t // GLA_CHUNK, GLA_CHUNK)
    x1, h2t, ei, gw, rk, counts = _merge(xp, o_nsa, o_gla, m_a, m_b, mw)
    y_prompt = _moe(h2t, ei, gw, rk, counts, x1, g_fin, ew).reshape(bsz, t, d)

    w_len = min(WINDOW, t)
    rows_of = lambda a: a.reshape(bsz, NSA_KV_HEADS, HEAD_DIM, t).transpose(0, 3, 1, 2)[None]
    p_states = tuple(rows_of(a) for a in kv_t[:4]) + tuple(rows_of(a)[:, :, t - w_len:] for a in kv_t[4:]) + (
        p_gla.reshape(1, bsz, GLA_HEADS, GLA_DK, GLA_DV),)

    xs = x_sample.reshape(n_seq * s_new, d)
    (q4, kck, kcv, ksk, ksv, kwk, kwv, gates, q_l, k_l, v_l, lg, r_l, m_a, m_b) = _in_projection(xs, g_mix, w_p, w_al, b_al)
    pt_flat = page_table.reshape(n_seq * n_pages).astype(I32)
    pool = lambda c: c[0].transpose(0, 2, 3, 1).reshape(-1, PAGE_SIZE)
    kc, vc = _compress_sample(pt_flat, pool(cache_cmp_k), pool(cache_cmp_v), cw, n_seq, n_pages)
    buf_kw = state_win_k[0].transpose(0, 2, 3, 1).reshape(n_seq * KV_WIDTH, w_buf)
    buf_vw = state_win_v[0].transpose(0, 2, 3, 1).reshape(n_seq * KV_WIDTH, w_buf)
    o_nsa = _nsa_sample(pt_flat, q4, gates, kc, vc, ksk, ksv, kwk, kwv, buf_kw, buf_vw,
                        pool(cache_sel_k), pool(cache_sel_v), _sample_tables(rel_bias, past, s_new, w_buf),
                        n_seq, s_new, n_pages)
    cl = 16
    padc = lambda a: jnp.pad(a.reshape(n_seq, s_new, -1), ((0, 0), (0, cl - s_new), (0, 0))).reshape(n_seq * cl, -1)
    s_in = state_gla[0].reshape(n_seq * GLA_HEADS * GLA_DK, GLA_DV)
    o_gla, s_gla = _gla(padc(q_l), padc(k_l), padc(v_l), padc(lg), padc(r_l), g_gla, s_in, n_seq, 1, cl)
    o_gla = o_gla.reshape(n_seq, cl, GLA_V_WIDTH)[:, :s_new].reshape(n_seq * s_new, GLA_V_WIDTH)
    x1, h2t, ei, gw, rk, counts = _merge(xs, o_nsa, o_gla, m_a, m_b, mw)
    y_sample = _moe(h2t, ei, gw, rk, counts, x1, g_fin, ew).reshape(n_seq, s_new, d)

    kvs = (1, n_seq, s_new, NSA_KV_HEADS, HEAD_DIM)
    new_win = lambda buf, new: jnp.concatenate([buf, new.reshape(kvs).astype(buf.dtype)], axis=2)[:, :, s_new:]
    s_states = (kck.reshape(kvs), kcv.reshape(kvs), ksk.reshape(kvs), ksv.reshape(kvs),
                new_win(state_win_k, kwk), new_win(state_win_v, kwv),
                s_gla.reshape(1, n_seq, GLA_HEADS, GLA_DK, GLA_DV))
    return (y_prompt, y_sample) + p_states + s_states
```
